```python
import math
import jax, jax.numpy as jnp
from jax import lax
import numpy as np

D_MODEL = 2048
BATCH = 4
SEQ = 2048
DEPTH = 2
DEC_BATCH = 128
DEC_SEQ = 8
PAST_LEN = 16384
PAGE_SIZE = 128

N_EVEN = (DEPTH + 1) // 2
N_ODD = DEPTH // 2
M_HEADS = 4
M_HEAD_DIM = D_MODEL // 8
M_WIDTH = M_HEADS * M_HEAD_DIM
M_CONV = 4
G_HEADS = 8
G_HEAD_DIM = D_MODEL // 16
G_WIDTH = G_HEADS * G_HEAD_DIM
MIX_WIDTH = M_WIDTH + G_WIDTH
IN_SIZES = (M_WIDTH,) * 4 + (M_HEADS,) * 2 + (G_WIDTH,) * 4
IN_WIDTH = 4 * M_WIDTH + 2 * M_HEADS + 4 * G_WIDTH
R_HEAD_DIM = 64
R_HEADS = D_MODEL // R_HEAD_DIM
R_DECAY_LORA = 96
R_AAA_LORA = 96
R_GATE_LORA = 256
D_FF = 4 * D_MODEL
N_MOD = 6
CHUNK = 64
RMS_EPS = 1e-6
LN_X_EPS = 64e-5
F32 = jnp.float32

kernel_name = 'xlstm_hgrn2_rwkv7_hybrid_step'


def rmsnorm(x, g):
    x32 = x.astype(F32)
    y = x32 * lax.rsqrt(jnp.mean(x32 * x32, axis=-1, keepdims=True) + RMS_EPS)
    return (y * g.astype(F32)).astype(x.dtype)


def head_rms(y):
    return y * lax.rsqrt(jnp.mean(y * y, axis=-1, keepdims=True) + RMS_EPS)


def head_ln(y, eps):
    d = y - jnp.mean(y, axis=-1, keepdims=True)
    return d * lax.rsqrt(jnp.mean(d * d, axis=-1, keepdims=True) + eps)


def causal_conv(u, buf, w):
    T = u.shape[1]
    K = w.shape[0]
    ext = jnp.concatenate([buf.astype(u.dtype), u], axis=1)
    out = ext[:, K - 1:K - 1 + T] * w[K - 1]
    for j in range(K - 1):
        out = out + ext[:, j:j + T] * w[j]
    return out, ext[:, T:]


def to_chunks(a, L):
    B, T = a.shape[:2]
    return jnp.moveaxis(a.reshape((B, T // L, L) + a.shape[2:]), 1, 0)


def from_chunks(a):
    NC, B, L = a.shape[:3]
    return jnp.moveaxis(a, 0, 1).reshape((B, NC * L) + a.shape[3:])


def mlstm_chunkwise(q, k, v, ig, lf, C0, n0, m0):
    T = q.shape[1]
    L = math.gcd(T, CHUNK)
    mask = jnp.tril(jnp.ones((L, L), dtype=bool))[None, :, :, None]

    def step(carry, xs):
        C, n, m = carry
        qc, kc, vc, ic, fc = xs
        b = jnp.cumsum(fc, axis=1)
        dmat = b[:, :, None, :] - b[:, None, :, :] + ic[:, None, :, :]
        dmat = jnp.where(mask, dmat, -jnp.inf)
        inter = b + m[:, None, :]
        m_t = jnp.maximum(inter, jnp.max(dmat, axis=2))
        w_intra = jnp.exp(dmat - m_t[:, :, None, :])
        w_inter = jnp.exp(inter - m_t)
        s = jnp.einsum('bthd,bshd->btsh', qc, kc) * w_intra
        num = jnp.einsum('btsh,bshe->bthe', s, vc) + w_inter[..., None] * jnp.einsum('bthd,bhde->bthe', qc, C)
        den = jnp.sum(s, axis=2) + w_inter * jnp.einsum('bthd,bhd->bth', qc, n)
        h = num / jnp.maximum(jnp.abs(den), jnp.exp(-m_t))[..., None]
        m_new = m_t[:, -1]
        w_end = jnp.exp(b[:, -1:, :] - b + ic - m_new[:, None, :])
        decay = jnp.exp(b[:, -1] + m - m_new)
        C = decay[..., None, None] * C + jnp.einsum('bsh,bshd,bshe->bhde', w_end, kc, vc)
        n = decay[..., None] * n + jnp.einsum('bsh,bshd->bhd', w_end, kc)
        return (C, n, m_new), h

    xs = tuple(to_chunks(t, L) for t in (q, k, v, ig, lf))
    (C, n, m), hs = lax.scan(step, (C0, n0, m0), xs)
    return from_chunks(hs), C, n, m


def hgrn2_chunkwise(q, k, v, lg, S0):
    T = q.shape[1]
    L = math.gcd(T, CHUNK)
    mask = jnp.tril(jnp.ones((L, L), dtype=bool))[None, :, :, None, None]

    def step(S, xs):
        qc, kc, vc, gc = xs
        bc = jnp.cumsum(gc, axis=1)
        diff = jnp.where(mask, bc[:, :, None] - bc[:, None, :], -jnp.inf)
        A = jnp.einsum('btshk,bshk->btsh', qc[:, :, None] * jnp.exp(diff), kc)
        o = jnp.einsum('btsh,bshv->bthv', A, vc) + jnp.einsum('bthk,bhkv->bthv', qc * jnp.exp(bc), S)
        bl = bc[:, -1]
        S = jnp.exp(bl)[..., None] * S + jnp.einsum('bshk,bshv->bhkv', kc * jnp.exp(bl[:, None] - bc), vc)
        return S, o

    xs = tuple(to_chunks(t, L) for t in (q, k, v, lg))
    S, os_ = lax.scan(step, S0, xs)
    return from_chunks(os_), S


def rwkv7_recurrence(r, w, k, v, a, b, S0):
    def step(S, xs):
        rt, wt, kt, vt, at, bt = xs
        sa = jnp.einsum('bhij,bhj->bhi', S, at)
        S = S * wt[:, :, None, :] + sa[..., None] * bt[:, :, None, :] + vt[..., None] * kt[:, :, None, :]
        return S, jnp.einsum('bhij,bhj->bhi', S, rt)

    xs = tuple(jnp.moveaxis(t, 1, 0) for t in (r, w, k, v, a, b))
    S, ys = lax.scan(step, S0, xs)
    return jnp.moveaxis(ys, 0, 1), S


def ab_mixer(h, W_in, gate_b, conv_w, m_gain, lb, g_gain, W_out, C0, n0, m0, conv0, S0):
    B, T, _ = h.shape
    z = h @ W_in
    idx = np.cumsum(IN_SIZES)[:-1].tolist()
    mq, mk, mv, mo, mi, mf, gq, gf, gi, gg = jnp.split(z, idx, axis=-1)
    qk, conv_new = causal_conv(jnp.concatenate([mq, mk], axis=-1), conv0, conv_w)
    qk = jax.nn.silu(qk.astype(F32))
    mhd = lambda t: t.reshape(B, T, M_HEADS, M_HEAD_DIM)
    q = mhd(qk[..., :M_WIDTH])
    k = mhd(qk[..., M_WIDTH:]) * (M_HEAD_DIM ** -0.5)
    v = mhd(mv.astype(F32))
    ig = mi.astype(F32) + gate_b[:M_HEADS].astype(F32)
    lf = jax.nn.log_sigmoid(mf.astype(F32) + gate_b[M_HEADS:].astype(F32))
    hm, C, n, m = mlstm_chunkwise(q, k, v, ig, lf, C0.astype(F32), n0.astype(F32), m0.astype(F32))
    hm = jax.nn.sigmoid(mhd(mo.astype(F32))) * head_ln(hm, RMS_EPS) * m_gain.astype(F32).reshape(M_HEADS, M_HEAD_DIM)
    ghd = lambda t: t.reshape(B, T, G_HEADS, G_HEAD_DIM)
    gq_ = ghd(jax.nn.silu(gq.astype(F32))) * (G_HEAD_DIM ** -0.5)
    f = lb + (1.0 - lb) * jax.nn.sigmoid(gf.astype(F32))
    og, S = hgrn2_chunkwise(gq_, ghd(1.0 - f), ghd(gi.astype(F32)), ghd(jnp.log(f)), S0.astype(F32))
    og = head_rms(og) * g_gain.astype(F32).reshape(G_HEADS, G_HEAD_DIM) * ghd(jax.nn.silu(gg.astype(F32)))
    mixed = jnp.concatenate([hm.reshape(B, T, M_WIDTH), og.reshape(B, T, G_WIDTH)], axis=-1).astype(h.dtype)
    return mixed @ W_out, C, n, m, conv_new, S


def rwkv7_mixer(h, shift0, S0, mu, w0, w1, w2, a0, a1, a2, g1, g2, k_k, k_a, r_k, W_r, W_k, W_v, W_o, ln_w, ln_b):
    B, T, _ = h.shape
    prev = jnp.concatenate([shift0[:, None, :].astype(h.dtype), h[:, :-1]], axis=1)
    xx = prev - h
    xr, xw, xk, xv, xa, xg = (h + xx * mu[i] for i in range(6))
    r = (xr @ W_r).astype(F32)
    k = (xk @ W_k).astype(F32)
    v = (xv @ W_v).astype(F32)
    w = -jax.nn.softplus(-(w0.astype(F32) + (jnp.tanh(xw @ w1) @ w2).astype(F32))) - 0.5
    a = jax.nn.sigmoid(a0.astype(F32) + ((xa @ a1) @ a2).astype(F32))
    g = jax.nn.sigmoid(xg @ g1) @ g2
    hd = lambda t: t.reshape(B, T, R_HEADS, R_HEAD_DIM)
    kk = hd(k * k_k.astype(F32))
    kk = kk / jnp.maximum(jnp.linalg.norm(kk, axis=-1, keepdims=True), 1e-12)
    k = k * (1.0 + (a - 1.0) * k_a.astype(F32))
    decay = jnp.exp(-jnp.exp(w))
    r4, k4, v4, a4 = hd(r), hd(k), hd(v), hd(a)
    y, S = rwkv7_recurrence(r4, hd(decay), k4, v4, -kk, kk * a4, S0.astype(F32))
    y = head_ln(y, LN_X_EPS).reshape(B, T, D_MODEL) * ln_w.astype(F32) + ln_b.astype(F32)
    bonus = jnp.sum(r4 * k4 * r_k.astype(F32), axis=-1, keepdims=True) * v4
    y = y + bonus.reshape(B, T, D_MODEL)
    out = (y.astype(h.dtype) * g) @ W_o
    return out, S, h[:, -1]


def trunk(x, c, m_C, m_n, m_m, m_conv, g_S, r_S, r_shift, p):
    B = x.shape[0]
    sc = jax.nn.silu(c)
    lbs = jnp.cumsum(jax.nn.softmax(p['g_lb'].astype(F32), axis=0), axis=0)
    outC, outn, outm, outconv, outS, outrS, outsh = [], [], [], [], [], [], []
    for l in range(DEPTH):
        mod = (sc @ p['mod_w'][l] + p['mod_b'][l]).reshape(B, N_MOD, D_MODEL)
        sh1, sc1, gt1, sh2, sc2, gt2 = (mod[:, i, None, :] for i in range(N_MOD))
        h = rmsnorm(x, p['norm_mix'][l]) * (1.0 + sc1) + sh1
        j = l // 2
        if l % 2 == 0:
            out, C, n, m, conv, S = ab_mixer(h, p['ab_w_in'][j], p['ab_gate_b'][j], p['m_conv_w'][j], p['m_norm'][j],
                                             lbs[j], p['g_norm'][j], p['ab_w_out'][j],
                                             m_C[j], m_n[j], m_m[j], m_conv[j], g_S[j])
            outC.append(C); outn.append(n); outm.append(m); outconv.append(conv); outS.append(S)
        else:
            out, S, sh = rwkv7_mixer(h, r_shift[j], r_S[j], p['r_mu'][j], p['r_w0'][j], p['r_w1'][j], p['r_w2'][j],
                                     p['r_a0'][j], p['r_a1'][j], p['r_a2'][j], p['r_g1'][j], p['r_g2'][j],
                                     p['r_kk'][j], p['r_ka'][j], p['r_rk'][j], p['r_wr'][j], p['r_wk'][j],
                                     p['r_wv'][j], p['r_wo'][j], p['r_lnw'][j], p['r_lnb'][j])
            outrS.append(S); outsh.append(sh)
        x = x + (gt1 * out).astype(x.dtype)
        h = rmsnorm(x, p['norm_ffn'][l]) * (1.0 + sc2) + sh2
        ff = jnp.square(jax.nn.relu(h @ p['ffn_w1'][l])) @ p['ffn_w2'][l]
        x = x + (gt2 * ff).astype(x.dtype)
    y = rmsnorm(x, p['final_norm'])
    return y, (jnp.stack(outC), jnp.stack(outn), jnp.stack(outm), jnp.stack(outconv),
               jnp.stack(outS), jnp.stack(outrS), jnp.stack(outsh))


def setup_inputs(seed: int = 0) -> dict:
    key = jax.random.key(seed)
    ks = iter(jax.random.split(key, 64))
    D = D_MODEL

    def nrm(shape, scale):
        return jax.random.normal(next(ks), shape, F32) * scale

    def uni(shape, lo, hi):
        return jax.random.uniform(next(ks), shape, F32, lo, hi)

    return {
        'x_prompt': nrm((BATCH, SEQ, D), 1.0),
        'x_sample': nrm((DEC_BATCH, DEC_SEQ, D), 1.0),
        'c_prompt': nrm((BATCH, D), 1.0),
        'c_sample': nrm((DEC_BATCH, D), 1.0),
        'state_mlstm_C': nrm((N_EVEN, DEC_BATCH, M_HEADS, M_HEAD_DIM, M_HEAD_DIM), 0.05),
        'state_mlstm_n': nrm((N_EVEN, DEC_BATCH, M_HEADS, M_HEAD_DIM), 0.5),
        'state_mlstm_m': uni((N_EVEN, DEC_BATCH, M_HEADS), 0.0, 3.0),
        'state_mlstm_conv': nrm((N_EVEN, DEC_BATCH, M_CONV - 1, 2 * M_WIDTH), 1.0),
        'state_hgrn_S': nrm((N_EVEN, DEC_BATCH, G_HEADS, G_HEAD_DIM, G_HEAD_DIM), 0.3),
        'state_rwkv_S': nrm((N_ODD, DEC_BATCH, R_HEADS, R_HEAD_DIM, R_HEAD_DIM), 0.3),
        'state_rwkv_shift': nrm((N_ODD, DEC_BATCH, D), 1.0),
        'mod_w': nrm((DEPTH, D, N_MOD * D), 0.5 * D ** -0.5),
        'mod_b': nrm((DEPTH, N_MOD * D), 0.02),
        'norm_mix': 1.0 + nrm((DEPTH, D), 0.02),
        'norm_ffn': 1.0 + nrm((DEPTH, D), 0.02),
        'ffn_w1': nrm((DEPTH, D, D_FF), D ** -0.5),
        'ffn_w2': nrm((DEPTH, D_FF, D), D_FF ** -0.5),
        'final_norm': 1.0 + nrm((D,), 0.02),
        'ab_w_in': nrm((N_EVEN, D, IN_WIDTH), D ** -0.5),
        'ab_gate_b': jnp.concatenate([uni((N_EVEN, M_HEADS), -3.0, -1.0), uni((N_EVEN, M_HEADS), 3.0, 6.0)], axis=-1),
        'm_conv_w': nrm((N_EVEN, M_CONV, 2 * M_WIDTH), M_CONV ** -0.5),
        'm_norm': 1.0 + nrm((N_EVEN, M_WIDTH), 0.02),
        'g_lb': nrm((N_EVEN + 1, G_WIDTH), 0.1) + jnp.arange(N_EVEN + 1, dtype=F32)[:, None],
        'g_norm': 1.0 + nrm((N_EVEN, G_WIDTH), 0.02),
        'ab_w_out': nrm((N_EVEN, MIX_WIDTH, D), MIX_WIDTH ** -0.5),
        'r_mu': uni((N_ODD, 6, D), 0.0, 1.0),
        'r_w0': uni((N_ODD, D), -5.0, 1.0),
        'r_w1': nrm((N_ODD, D, R_DECAY_LORA), D ** -0.5),
        'r_w2': nrm((N_ODD, R_DECAY_LORA, D), 0.5 * R_DECAY_LORA ** -0.5),
        'r_a0': nrm((N_ODD, D), 0.5),
        'r_a1': nrm((N_ODD, D, R_AAA_LORA), D ** -0.5),
        'r_a2': nrm((N_ODD, R_AAA_LORA, D), R_AAA_LORA ** -0.5),
        'r_g1': nrm((N_ODD, D, R_GATE_LORA), D ** -0.5),
        'r_g2': nrm((N_ODD, R_GATE_LORA, D), R_GATE_LORA ** -0.5),
        'r_kk': 0.85 + nrm((N_ODD, D), 0.02),
        'r_ka': 1.0 + nrm((N_ODD, D), 0.02),
        'r_rk': nrm((N_ODD, R_HEADS, R_HEAD_DIM), 0.1),
        'r_wr': nrm((N_ODD, D, D), D ** -0.5),
        'r_wk': nrm((N_ODD, D, D), D ** -0.5),
        'r_wv': nrm((N_ODD, D, D), D ** -0.5),
        'r_wo': nrm((N_ODD, D, D), D ** -0.5),
        'r_lnw': 1.0 + nrm((N_ODD, D), 0.02),
        'r_lnb': nrm((N_ODD, D), 0.02),
    }


def reference(x_prompt, x_sample, c_prompt, c_sample, state_mlstm_C, state_mlstm_n, state_mlstm_m,
              state_mlstm_conv, state_hgrn_S, state_rwkv_S, state_rwkv_shift, mod_w, mod_b, norm_mix,
              norm_ffn, ffn_w1, ffn_w2, final_norm, ab_w_in, ab_gate_b, m_conv_w, m_norm, g_lb, g_norm,
              ab_w_out, r_mu, r_w0, r_w1, r_w2, r_a0, r_a1, r_a2, r_g1, r_g2, r_kk, r_ka, r_rk, r_wr,
              r_wk, r_wv, r_wo, r_lnw, r_lnb):
    p = dict(mod_w=mod_w, mod_b=mod_b, norm_mix=norm_mix, norm_ffn=norm_ffn, ffn_w1=ffn_w1, ffn_w2=ffn_w2,
             final_norm=final_norm, ab_w_in=ab_w_in, ab_gate_b=ab_gate_b, m_conv_w=m_conv_w, m_norm=m_norm,
             g_lb=g_lb, g_norm=g_norm, ab_w_out=ab_w_out, r_mu=r_mu, r_w0=r_w0, r_w1=r_w1, r_w2=r_w2,
             r_a0=r_a0, r_a1=r_a1, r_a2=r_a2, r_g1=r_g1, r_g2=r_g2, r_kk=r_kk, r_ka=r_ka, r_rk=r_rk,
             r_wr=r_wr, r_wk=r_wk, r_wv=r_wv, r_wo=r_wo, r_lnw=r_lnw, r_lnb=r_lnb)
    B = x_prompt.shape[0]
    z_C = jnp.zeros((N_EVEN, B, M_HEADS, M_HEAD_DIM, M_HEAD_DIM), F32)
    z_n = jnp.zeros((N_EVEN, B, M_HEADS, M_HEAD_DIM), F32)
    z_m = jnp.zeros((N_EVEN, B, M_HEADS), F32)
    z_conv = jnp.zeros((N_EVEN, B, M_CONV - 1, 2 * M_WIDTH), x_prompt.dtype)
    z_S = jnp.zeros((N_EVEN, B, G_HEADS, G_HEAD_DIM, G_HEAD_DIM), F32)
    z_rS = jnp.zeros((N_ODD, B, R_HEADS, R_HEAD_DIM, R_HEAD_DIM), F32)
    z_sh = jnp.zeros((N_ODD, B, D_MODEL), x_prompt.dtype)
    y_prompt, (pC, pn, pm, pconv, pS, prS, psh) = trunk(x_prompt, c_prompt, z_C, z_n, z_m, z_conv, z_S, z_rS, z_sh, p)
    y_sample, (sC, sn, sm, sconv, sS, srS, ssh) = trunk(x_sample, c_sample, state_mlstm_C, state_mlstm_n,
                                                        state_mlstm_m, state_mlstm_conv, state_hgrn_S,
                                                        state_rwkv_S, state_rwkv_shift, p)
    return (y_prompt, y_sample, pC, pn, pm, pconv, pS, prS, psh, sC, sn, sm, sconv, sS, srS, ssh)
```

```python
import functools
import math

import jax
import jax.numpy as jnp
from jax import lax
from jax.experimental import pallas as pl
from jax.experimental.pallas import tpu as pltpu

F32 = jnp.float32
BF16 = jnp.bfloat16

D_MODEL = 2048
M_HEADS = 4
M_HEAD_DIM = 256
M_WIDTH = M_HEADS * M_HEAD_DIM
M_CONV = 4
G_HEADS = 8
G_HEAD_DIM = 128
G_WIDTH = G_HEADS * G_HEAD_DIM
R_HEAD_DIM = 64
R_HEADS = D_MODEL // R_HEAD_DIM
R_GROUP = 4
R_GROUP_W = R_GROUP * R_HEAD_DIM
N_MOD = 6
RMS_EPS = 1e-6
LN_X_EPS = 64e-5
CHUNK = 64
NEG_BIG = -1e30

V7X_VMEM_LIMIT_BYTES = 56 * 1024 * 1024
SUBLANES = 8
LANES = 128


def _cparams(*sem):
    return pltpu.CompilerParams(dimension_semantics=sem, vmem_limit_bytes=V7X_VMEM_LIMIT_BYTES)


def _sigmoid(x):
    return 1.0 / (1.0 + jnp.exp(-x))


def _silu(x):
    return x * _sigmoid(x)


def _softplus(x):
    return jnp.maximum(x, 0.0) + jnp.log1p(jnp.exp(-jnp.abs(x)))


def _dot(a, b):
    return jnp.dot(a.astype(BF16), b.astype(BF16), preferred_element_type=F32)


def _dot_nt(a, b):
    return lax.dot_general(a.astype(BF16), b.astype(BF16), (((1,), (1,)), ((), ())), preferred_element_type=F32)


def _dot_tn(a, b):
    return lax.dot_general(a.astype(BF16), b.astype(BF16), (((0,), (0,)), ((), ())), preferred_element_type=F32)


def _cumsum_rows(x):
    n = x.shape[0]
    row = lax.broadcasted_iota(jnp.int32, x.shape, 0)
    s = 1
    while s < n:
        x = x + jnp.where(row >= s, pltpu.roll(x, s, axis=0), 0.0)
        s *= 2
    return x


def _row_blocking(G, R, tm):
    if R >= tm:
        assert R % tm == 0
        return 1, tm, R // tm
    assert tm % R == 0 and G % (tm // R) == 0
    return tm // R, R, 1


def _mod_body(c_ref, w_ref, b_ref, o_ref):
    sc = _silu(c_ref[...])
    o_ref[...] = _dot(sc, w_ref[...]) + b_ref[...]


def _modulation(c_all, mod_w, mod_b):
    L, K, N = mod_w.shape
    Mc = c_all.shape[0]
    tn = 1024
    return pl.pallas_call(
        _mod_body,
        grid=(L, N // tn),
        in_specs=[pl.BlockSpec((Mc, K), lambda l, j: (0, 0)),
                  pl.BlockSpec((None, K, tn), lambda l, j: (l, 0, j)),
                  pl.BlockSpec((None, 1, tn), lambda l, j: (l, 0, j))],
        out_specs=pl.BlockSpec((None, Mc, tn), lambda l, j: (l, 0, j)),
        out_shape=jax.ShapeDtypeStruct((L, Mc, N), F32),
        compiler_params=_cparams("arbitrary", "arbitrary"),
        name="modulation",
    )(c_all, mod_w, mod_b.reshape(L, 1, N))


def _rms(x, g):
    ms = jnp.mean(x * x, axis=-1, keepdims=True)
    return x * lax.rsqrt(ms + RMS_EPS) * g


def _modnorm_body(x_ref, g_ref, sc_ref, sh_ref, o_ref):
    y = _rms(x_ref[...], g_ref[...])
    o_ref[...] = (y * (1.0 + sc_ref[...]) + sh_ref[...]).astype(o_ref.dtype)


def _modnorm(x, g, mod, i_scale, i_shift, tm):
    G, R, D = x.shape
    gb, rb, nrb = _row_blocking(G, R, tm)
    return pl.pallas_call(
        _modnorm_body,
        grid=(G // gb, nrb),
        in_specs=[pl.BlockSpec((gb, rb, D), lambda a, b: (a, b, 0)),
                  pl.BlockSpec((1, 1, D), lambda a, b: (0, 0, 0)),
                  pl.BlockSpec((gb, 1, D), lambda a, b: (a, 0, i_scale)),
                  pl.BlockSpec((gb, 1, D), lambda a, b: (a, 0, i_shift))],
        out_specs=pl.BlockSpec((gb, rb, D), lambda a, b: (a, b, 0)),
        out_shape=jax.ShapeDtypeStruct((G, R, D), BF16),
        compiler_params=_cparams("arbitrary", "arbitrary"),
        name="modnorm",
    )(x, g.reshape(1, 1, D), mod, mod)


def _rmsnorm_body(x_ref, g_ref, o_ref):
    o_ref[...] = _rms(x_ref[...], g_ref[...])


def _rmsnorm(x, g, tm):
    G, R, D = x.shape
    gb, rb, nrb = _row_blocking(G, R, tm)
    return pl.pallas_call(
        _rmsnorm_body,
        grid=(G // gb, nrb),
        in_specs=[pl.BlockSpec((gb, rb, D), lambda a, b: (a, b, 0)),
                  pl.BlockSpec((1, 1, D), lambda a, b: (0, 0, 0))],
        out_specs=pl.BlockSpec((gb, rb, D), lambda a, b: (a, b, 0)),
        out_shape=jax.ShapeDtypeStruct((G, R, D), F32),
        compiler_params=_cparams("arbitrary", "arbitrary"),
        name="final_norm",
    )(x, g.reshape(1, 1, D))


def _mm_body(a_ref, w_ref, o_ref, wbf, *, act):
    @pl.when(pl.program_id(1) == 0)
    def _():
        wbf[...] = w_ref[...].astype(BF16)

    acc = jnp.dot(a_ref[...].astype(BF16), wbf[...], preferred_element_type=F32)
    if act == "relu2":
        acc = jnp.square(jnp.maximum(acc, 0.0))
    o_ref[...] = acc.astype(o_ref.dtype)


def _mm(a, w, *, n_cols, tm, tn, act=None, out_dtype=F32, name="mm"):
    G, R, K = a.shape
    M = G * R
    assert M % tm == 0 and n_cols % tn == 0 and w.shape[0] == K
    out = pl.pallas_call(
        functools.partial(_mm_body, act=act),
        grid=(n_cols // tn, M // tm),
        in_specs=[pl.BlockSpec((tm, K), lambda j, i: (i, 0)),
                  pl.BlockSpec((K, tn), lambda j, i: (0, j))],
        out_specs=pl.BlockSpec((tm, tn), lambda j, i: (i, j)),
        out_shape=jax.ShapeDtypeStruct((M, n_cols), out_dtype),
        scratch_shapes=[pltpu.VMEM((K, tn), BF16)],
        compiler_params=_cparams("arbitrary", "arbitrary"),
        name=name,
    )(a.reshape(M, K), w)
    return out.reshape(G, R, n_cols)


def _mm_res_body(*refs, n_lhs, gb, rb):
    a_refs = refs[:n_lhs]
    w_ref, x_ref, gt_ref, o_ref, wbf = refs[n_lhs:]

    @pl.when(pl.program_id(1) == 0)
    def _():
        wbf[...] = w_ref[...].astype(BF16)

    acc = None
    k0 = 0
    for a_ref in a_refs:
        kk = a_ref.shape[-1]
        part = jnp.dot(a_ref[...].astype(BF16), wbf[k0:k0 + kk, :], preferred_element_type=F32)
        acc = part if acc is None else acc + part
        k0 += kk
    tn = acc.shape[-1]
    o_ref[...] = x_ref[...] + gt_ref[...] * acc.reshape(gb, rb, tn)


def _mm_residual(a_list, w, x, mod, i_gate, *, tm, tn, name="mm_res"):
    G, R, N = x.shape
    M = G * R
    K = w.shape[0]
    assert sum(a.shape[-1] for a in a_list) == K and w.shape[1] == N
    gb, rb, nrb = _row_blocking(G, R, tm)
    ntn = N // tn

    def xmap(j, i):
        return (i // nrb, i % nrb, j)

    return pl.pallas_call(
        functools.partial(_mm_res_body, n_lhs=len(a_list), gb=gb, rb=rb),
        grid=(ntn, M // tm),
        in_specs=[pl.BlockSpec((tm, a.shape[-1]), lambda j, i: (i, 0)) for a in a_list]
        + [pl.BlockSpec((K, tn), lambda j, i: (0, j)),
           pl.BlockSpec((gb, rb, tn), xmap),
           pl.BlockSpec((gb, 1, tn), lambda j, i: (i // nrb, 0, i_gate * ntn + j))],
        out_specs=pl.BlockSpec((gb, rb, tn), xmap),
        out_shape=jax.ShapeDtypeStruct((G, R, N), F32),
        scratch_shapes=[pltpu.VMEM((K, tn), BF16)],
        compiler_params=_cparams("arbitrary", "arbitrary"),
        name=name,
    )(*[a.reshape(M, a.shape[-1]) for a in a_list], w, x, mod)


def _mlstm_body(zq_ref, zk_ref, zv_ref, zo_ref, zg_ref, cw_ref, gbias_ref, gain_ref, conv0_ref,
                C0_ref, n0_ref, m0_ref, hm_ref, C_ref, n_ref, m_ref, ext, *, L):
    W = M_WIDTH
    Dh = M_HEAD_DIM
    keep = M_CONV - 1
    c = pl.program_id(1)

    @pl.when(c == 0)
    def _():
        C_ref[...] = C0_ref[...]
        n_ref[...] = n0_ref[...]
        m_ref[...] = m0_ref[...]
        ext[SUBLANES - keep:SUBLANES, :] = conv0_ref[...]

    ext[SUBLANES:SUBLANES + L, 0:W] = zq_ref[...]
    ext[SUBLANES:SUBLANES + L, W:2 * W] = zk_ref[...]
    conv = ext[SUBLANES:SUBLANES + L, :] * cw_ref[keep:keep + 1, :]
    for j in range(keep):
        conv = conv + ext[SUBLANES - keep + j:SUBLANES - keep + j + L, :] * cw_ref[j:j + 1, :]
    ext[SUBLANES - keep:SUBLANES, :] = ext[SUBLANES + L - keep:SUBLANES + L, :]
    qk = _silu(conv)

    gates = zg_ref[...] + gbias_ref[...]
    row = lax.broadcasted_iota(jnp.int32, (L, L), 0)
    col = lax.broadcasted_iota(jnp.int32, (L, L), 1)
    eye = row == col
    tri = col <= row

    for h in range(M_HEADS):
        sl = slice(h * Dh, (h + 1) * Dh)
        q = qk[:, h * Dh:(h + 1) * Dh]
        k = qk[:, W + h * Dh:W + (h + 1) * Dh] * (Dh ** -0.5)
        v = zv_ref[:, sl]
        ig_col = gates[:, h:h + 1]
        fpre = gates[:, M_HEADS + h:M_HEADS + h + 1]
        lf_col = jnp.minimum(fpre, 0.0) - jnp.log1p(jnp.exp(-jnp.abs(fpre)))
        lf_row = jnp.sum(jnp.where(eye, lf_col, 0.0), axis=0, keepdims=True)
        ig_row = jnp.sum(jnp.where(eye, ig_col, 0.0), axis=0, keepdims=True)
        b_col = jnp.sum(jnp.where(tri, lf_row, 0.0), axis=1, keepdims=True)
        b_row = jnp.sum(jnp.where(row <= col, lf_col, 0.0), axis=0, keepdims=True)
        dmat = jnp.where(tri, b_col - b_row + ig_row, NEG_BIG)
        m_prev = m_ref[:, h:h + 1]
        inter = b_col + m_prev
        m_t = jnp.maximum(inter, jnp.max(dmat, axis=1, keepdims=True))
        w_intra = jnp.exp(dmat - m_t)
        w_inter = jnp.exp(inter - m_t)
        Cm = C_ref[h]
        n_row = n_ref[h:h + 1, :]
        s = _dot_nt(q, k) * w_intra
        num = _dot(s, v) + w_inter * _dot(q, Cm)
        den = jnp.sum(s, axis=1, keepdims=True) + w_inter * jnp.sum(q * n_row, axis=1, keepdims=True)
        hh = num / jnp.maximum(jnp.abs(den), jnp.exp(-m_t))
        m_new = m_t[L - 1:L, :]
        b_last = b_col[L - 1:L, :]
        w_end = jnp.exp(b_last - b_col + ig_col - m_new)
        decay = jnp.exp(b_last + m_prev - m_new)
        wk = w_end * k
        C_ref[h] = decay * Cm + _dot_tn(wk, v)
        n_ref[h:h + 1, :] = decay * n_row + jnp.sum(wk, axis=0, keepdims=True)
        m_ref[:, h:h + 1] = m_new
        dlt = hh - jnp.mean(hh, axis=-1, keepdims=True)
        ln = dlt * lax.rsqrt(jnp.mean(dlt * dlt, axis=-1, keepdims=True) + RMS_EPS)
        hm_ref[:, sl] = _sigmoid(zo_ref[:, sl]) * ln * gain_ref[:, sl]


def _mlstm(zm, zgate, conv_w, gate_b, m_gain, conv0, C0, n0, m0):
    G, R, _ = zm.shape
    L = math.gcd(R, CHUNK)
    W = M_WIDTH
    gbias = jnp.zeros((1, LANES), F32).at[0, :2 * M_HEADS].set(gate_b.astype(F32))
    zspec = lambda blk: pl.BlockSpec((None, L, W), lambda g, c: (g, c, blk))
    st4 = lambda g, c: (g, 0, 0, 0)
    st3 = lambda g, c: (g, 0, 0)
    return pl.pallas_call(
        functools.partial(_mlstm_body, L=L),
        grid=(G, R // L),
        in_specs=[zspec(0), zspec(1), zspec(2), zspec(3),
                  pl.BlockSpec((None, L, LANES), lambda g, c: (g, c, 0)),
                  pl.BlockSpec((M_CONV, 2 * W), lambda g, c: (0, 0)),
                  pl.BlockSpec((1, LANES), lambda g, c: (0, 0)),
                  pl.BlockSpec((1, W), lambda g, c: (0, 0)),
                  pl.BlockSpec((None, M_CONV - 1, 2 * W), st3),
                  pl.BlockSpec((None, M_HEADS, M_HEAD_DIM, M_HEAD_DIM), st4),
                  pl.BlockSpec((None, M_HEADS, M_HEAD_DIM), st3),
                  pl.BlockSpec((None, 1, M_HEADS), st3)],
        out_specs=[pl.BlockSpec((None, L, W), lambda g, c: (g, c, 0)),
                   pl.BlockSpec((None, M_HEADS, M_HEAD_DIM, M_HEAD_DIM), st4),
                   pl.BlockSpec((None, M_HEADS, M_HEAD_DIM), st3),
                   pl.BlockSpec((None, 1, M_HEADS), st3)],
        out_shape=[jax.ShapeDtypeStruct((G, R, W), F32),
                   jax.ShapeDtypeStruct((G, M_HEADS, M_HEAD_DIM, M_HEAD_DIM), F32),
                   jax.ShapeDtypeStruct((G, M_HEADS, M_HEAD_DIM), F32),
                   jax.ShapeDtypeStruct((G, 1, M_HEADS), F32)],
        scratch_shapes=[pltpu.VMEM((SUBLANES + L, 2 * W), F32)],
        compiler_params=_cparams("arbitrary", "arbitrary"),
        name="mlstm",
    )(zm, zm, zm, zm, zgate, conv_w, gbias, m_gain.reshape(1, W), conv0, C0, n0, m0.reshape(G, 1, M_HEADS))


def _hgrn_body(zq_ref, zf_ref, zi_ref, zgg_ref, lb_ref, gain_ref, S0_ref, og_ref, S_ref, ST, *, nblk):
    Dh = G_HEAD_DIM
    B = SUBLANES
    c = pl.program_id(1)

    @pl.when(c == 0)
    def _():
        for h in range(G_HEADS):
            ST[h] = S0_ref[h].T

    rowi = lax.broadcasted_iota(jnp.int32, (B, Dh), 0)

    def blk(i, carry):
        r0 = pl.multiple_of(i * B, B)
        for h in range(G_HEADS):
            sl = slice(h * Dh, (h + 1) * Dh)
            lb = lb_ref[:, sl]
            f = lb + (1.0 - lb) * _sigmoid(zf_ref[pl.ds(r0, B), sl])
            lg = jnp.log(f)
            kk = 1.0 - f
            qh = _silu(zq_ref[pl.ds(r0, B), sl]) * (Dh ** -0.5)
            v = zi_ref[pl.ds(r0, B), sl]
            bc = _cumsum_rows(lg)
            btot = bc[B - 1:B, :]
            st = ST[h]
            o = _dot_nt(qh * jnp.exp(bc), st)
            for s in range(B):
                diff = jnp.where(rowi >= s, bc - bc[s:s + 1, :], NEG_BIG)
                p = jnp.exp(diff) * qh * kk[s:s + 1, :]
                o = o + jnp.sum(p, axis=-1, keepdims=True) * v[s:s + 1, :]
            khat = kk * jnp.exp(btot - bc)
            ST[h] = st * jnp.exp(btot) + _dot_tn(v, khat)
            rms = o * lax.rsqrt(jnp.mean(o * o, axis=-1, keepdims=True) + RMS_EPS)
            og_ref[pl.ds(r0, B), sl] = rms * gain_ref[:, sl] * _silu(zgg_ref[pl.ds(r0, B), sl])
        return carry

    lax.fori_loop(0, nblk, blk, 0)

    @pl.when(c == pl.num_programs(1) - 1)
    def _():
        for h in range(G_HEADS):
            S_ref[h] = ST[h].T


def _hgrn(zg, lb, g_gain, S0):
    G, R, _ = zg.shape
    W = G_WIDTH
    Rc = min(R, 256)
    zspec = lambda blk: pl.BlockSpec((None, Rc, W), lambda g, c: (g, c, blk))
    st4 = lambda g, c: (g, 0, 0, 0)
    return pl.pallas_call(
        functools.partial(_hgrn_body, nblk=Rc // SUBLANES),
        grid=(G, R // Rc),
        in_specs=[zspec(0), zspec(1), zspec(2), zspec(3),
                  pl.BlockSpec((1, W), lambda g, c: (0, 0)),
                  pl.BlockSpec((1, W), lambda g, c: (0, 0)),
                  pl.BlockSpec((None, G_HEADS, G_HEAD_DIM, G_HEAD_DIM), st4)],
        out_specs=[pl.BlockSpec((None, Rc, W), lambda g, c: (g, c, 0)),
                   pl.BlockSpec((None, G_HEADS, G_HEAD_DIM, G_HEAD_DIM), st4)],
        out_shape=[jax.ShapeDtypeStruct((G, R, W), F32),
                   jax.ShapeDtypeStruct((G, G_HEADS, G_HEAD_DIM, G_HEAD_DIM), F32)],
        scratch_shapes=[pltpu.VMEM((G_HEADS, G_HEAD_DIM, G_HEAD_DIM), F32)],
        compiler_params=_cparams("arbitrary", "arbitrary"),
        name="hgrn2",
    )(zg, zg, zg, zg, lb.reshape(1, W), g_gain.reshape(1, W), S0)


def _rprep_body(x_ref, g_ref, sc_ref, sh_ref, shift0_ref, mu_ref, *refs, rb):
    outs = refs[:6]
    hlast_ref, hbuf = refs[6:]
    B = SUBLANES

    @pl.when(pl.program_id(1) == 0)
    def _():
        hbuf[:, B - 1:B, :] = shift0_ref[...]

    h = _rms(x_ref[...], g_ref[...]) * (1.0 + sc_ref[...]) + sh_ref[...]
    hbuf[:, B:B + rb, :] = h
    prev = hbuf[:, B - 1:B - 1 + rb, :]
    hbuf[:, B - 1:B, :] = h[:, rb - 1:rb, :]
    xx = prev - h
    for i in range(6):
        outs[i][...] = (h + xx * mu_ref[i:i + 1, :]).astype(BF16)
    hlast_ref[...] = h[:, rb - 1:rb, :]


def _rwkv_prep(x, g, mod, i_scale, i_shift, shift0, mu, tm):
    G, R, D = x.shape
    gb, rb, nrb = _row_blocking(G, R, tm)
    tok = pl.BlockSpec((gb, rb, D), lambda a, b: (a, b, 0))
    one = pl.BlockSpec((gb, 1, D), lambda a, b: (a, 0, 0))
    res = pl.pallas_call(
        functools.partial(_rprep_body, rb=rb),
        grid=(G // gb, nrb),
        in_specs=[tok,
                  pl.BlockSpec((1, 1, D), lambda a, b: (0, 0, 0)),
                  pl.BlockSpec((gb, 1, D), lambda a, b: (a, 0, i_scale)),
                  pl.BlockSpec((gb, 1, D), lambda a, b: (a, 0, i_shift)),
                  one,
                  pl.BlockSpec((6, D), lambda a, b: (0, 0))],
        out_specs=[tok] * 6 + [one],
        out_shape=[jax.ShapeDtypeStruct((G, R, D), BF16)] * 6 + [jax.ShapeDtypeStruct((G, 1, D), F32)],
        scratch_shapes=[pltpu.VMEM((gb, SUBLANES + rb, D), F32)],
        compiler_params=_cparams("arbitrary", "arbitrary"),
        name="rwkv_prep",
    )(x, g.reshape(1, 1, D), mod, mod, shift0.reshape(G, 1, D), mu)
    return res[:6], res[6]


def _lora_body(x_ref, w1_ref, w2_ref, o_ref, *, act):
    t = _dot(x_ref[...], w1_ref[...])
    if act == "tanh":
        t = jnp.tanh(t)
    elif act == "sigmoid":
        t = _sigmoid(t)
    o_ref[...] = _dot(t, w2_ref[...])


def _lora(x, w1, w2, act, tm):
    G, R, K = x.shape
    M = G * R
    r = w1.shape[1]
    rp = -(-r // LANES) * LANES
    w1p = jnp.zeros((K, rp), F32).at[:, :r].set(w1)
    w2p = jnp.zeros((rp, w2.shape[1]), F32).at[:r, :].set(w2)
    N = w2.shape[1]
    out = pl.pallas_call(
        functools.partial(_lora_body, act=act),
        grid=(M // tm,),
        in_specs=[pl.BlockSpec((tm, K), lambda i: (i, 0)),
                  pl.BlockSpec((K, rp), lambda i: (0, 0)),
                  pl.BlockSpec((rp, N), lambda i: (0, 0))],
        out_specs=pl.BlockSpec((tm, N), lambda i: (i, 0)),
        out_shape=jax.ShapeDtypeStruct((M, N), F32),
        compiler_params=_cparams("arbitrary"),
        name="lora_" + str(act),
    )(x.reshape(M, K), w1p, w2p)
    return out.reshape(G, R, N)


def _head_mask(rows, cols, rper, cper):
    r = lax.broadcasted_iota(jnp.int32, (rows, cols), 0)
    c = lax.broadcasted_iota(jnp.int32, (rows, cols), 1)
    return (r // rper) == (c // cper)


def _bd(y, mask):
    return jnp.where(mask, jnp.concatenate([y] * R_GROUP, axis=0), 0.0).astype(BF16)


def _segsum(x, ones_bd):
    hi = x.astype(BF16)
    lo = (x - hi.astype(F32)).astype(BF16)
    return (jnp.dot(hi, ones_bd, preferred_element_type=F32) + jnp.dot(lo, ones_bd, preferred_element_type=F32))


def _rwkv_body(r_ref, k_ref, v_ref, wl_ref, al_ref, g_ref, w0_ref, a0_ref, kkp_ref, kap_ref, rk_ref,
               lnw_ref, lnb_ref, S0_ref, y_ref, S_ref, Sbd, *, c, nchunk):
    N = R_HEAD_DIM
    GW = R_GROUP_W
    j = pl.program_id(2)
    m_state = _head_mask(GW, GW, N, N)

    @pl.when(j == 0)
    def _():
        s0 = S0_ref[...].reshape(GW, N)
        Sbd[...] = jnp.where(m_state, jnp.concatenate([s0] * R_GROUP, axis=1), 0.0)

    ones_bd = jnp.where(m_state, 1.0, 0.0).astype(BF16)
    m_vec = _head_mask(R_GROUP * c, GW, c, N)
    m_mat = _head_mask(R_GROUP * c, R_GROUP * c, c, c)
    t_idx = lax.broadcasted_iota(jnp.int32, (c, R_GROUP * c), 0)
    s_idx = lax.broadcasted_iota(jnp.int32, (c, R_GROUP * c), 1) % c
    strict = s_idx < t_idx
    incl = s_idx <= t_idx

    def chunk(ci, carry):
        r0 = pl.multiple_of(ci * c, c)
        rows = pl.ds(r0, c)
        r = r_ref[rows, :]
        k = k_ref[rows, :]
        v = v_ref[rows, :]
        w_raw = -_softplus(-(w0_ref[...] + wl_ref[rows, :])) - 0.5
        lw = -jnp.exp(w_raw)
        a = _sigmoid(a0_ref[...] + al_ref[rows, :])
        kk = k * kkp_ref[...]
        kk = kk / jnp.maximum(jnp.sqrt(_segsum(kk * kk, ones_bd)), 1e-12)
        k2 = k * (1.0 + (a - 1.0) * kap_ref[...])
        aa = -kk
        bb = kk * a

        cw = _cumsum_rows(lw)
        cwl = cw[c - 1:c, :]
        e_in = jnp.exp(cw)
        e_out = jnp.exp(-cw)
        e_end = jnp.exp(cwl - cw)
        at = aa * jnp.exp(cw - lw)
        rt = r * e_in
        ar = jnp.concatenate([at, rt], axis=0)
        pb = _dot_nt(ar, _bd(bb * e_out, m_vec))
        pk = _dot_nt(ar, _bd(k2 * e_out, m_vec))
        nmat = jnp.where(strict, pb[:c], 0.0)
        a_ak = jnp.where(strict, pk[:c], 0.0)
        a_rb = jnp.where(incl, pb[c:], 0.0)
        a_rk = jnp.where(incl, pk[c:], 0.0)

        sbd = Sbd[...]
        s0p = _dot_nt(ar, sbd)
        bdv = _bd(v, m_vec)
        u = s0p[:c] + jnp.dot(a_ak.astype(BF16), bdv, preferred_element_type=F32)
        p = nmat
        step = 1
        while step < c:
            u = u + jnp.dot(p.astype(BF16), _bd(u, m_vec), preferred_element_type=F32)
            step *= 2
            if step < c:
                p = jnp.dot(p.astype(BF16), _bd(p, m_mat), preferred_element_type=F32)
        y = (s0p[c:] + jnp.dot(a_rb.astype(BF16), _bd(u, m_vec), preferred_element_type=F32)
             + jnp.dot(a_rk.astype(BF16), bdv, preferred_element_type=F32))

        uv = jnp.concatenate([u, v], axis=0)
        bk = jnp.concatenate([bb * e_end, k2 * e_end], axis=0)
        Sbd[...] = sbd * jnp.exp(cwl) + jnp.where(m_state, _dot_tn(uv, bk), 0.0)

        mean = _segsum(y, ones_bd) * (1.0 / N)
        dlt = y - mean
        var = _segsum(dlt * dlt, ones_bd) * (1.0 / N)
        yn = dlt * lax.rsqrt(var + LN_X_EPS) * lnw_ref[...] + lnb_ref[...]
        bonus = _segsum(r * k2 * rk_ref[...], ones_bd) * v
        y_ref[rows, :] = ((yn + bonus) * g_ref[rows, :]).astype(y_ref.dtype)
        return carry

    lax.fori_loop(0, nchunk, chunk, 0)

    @pl.when(j == pl.num_programs(2) - 1)
    def _():
        sbd = Sbd[...]
        for h in range(R_GROUP):
            S_ref[h] = sbd[h * N:(h + 1) * N, h * N:(h + 1) * N]


def _rwkv(r, k, v, wl, al, g, w0, a0, kkp, kap, rk, lnw, lnb, S0):
    G, R, D = r.shape
    c = math.gcd(R, CHUNK)
    Rc = min(R, 256)
    GW = R_GROUP_W
    ngrp = D // GW
    tok = pl.BlockSpec((None, Rc, GW), lambda a, b, j: (a, j, b))
    par = pl.BlockSpec((1, GW), lambda a, b, j: (0, b))
    st = pl.BlockSpec((None, R_GROUP, R_HEAD_DIM, R_HEAD_DIM), lambda a, b, j: (a, b, 0, 0))
    row = lambda p: p.reshape(1, D).astype(F32)
    return pl.pallas_call(
        functools.partial(_rwkv_body, c=c, nchunk=Rc // c),
        grid=(G, ngrp, R // Rc),
        in_specs=[tok] * 6 + [par] * 7 + [st],
        out_specs=[tok, st],
        out_shape=[jax.ShapeDtypeStruct((G, R, D), BF16),
                   jax.ShapeDtypeStruct((G, R_HEADS, R_HEAD_DIM, R_HEAD_DIM), F32)],
        scratch_shapes=[pltpu.VMEM((GW, GW), F32)],
        compiler_params=_cparams("arbitrary", "arbitrary", "arbitrary"),
        name="rwkv7",
    )(r, k, v, wl, al, g, row(w0), row(a0), row(kkp), row(kap), row(rk), row(lnw), row(lnb), S0)


def _trunk(x, mod, m_C, m_n, m_m, m_conv, g_S, r_S, r_shift, p, lbs, tm):
    G, R, D = x.shape
    md = mod[0]
    h = _modnorm(x, p["norm_mix"][0], md, 1, 0, tm)
    w_in = p["ab_w_in"][0]
    w_gate = jnp.zeros((D, LANES), F32).at[:, :2 * M_HEADS].set(w_in[:, 4 * M_WIDTH:4 * M_WIDTH + 2 * M_HEADS])
    w_g = w_in[:, 4 * M_WIDTH + 2 * M_HEADS:]
    zm = _mm(h, w_in, n_cols=4 * M_WIDTH, tm=tm, tn=512, name="mm_in_m")
    zgate = _mm(h, w_gate, n_cols=LANES, tm=tm, tn=LANES, name="mm_in_gate")
    zg = _mm(h, w_g, n_cols=4 * G_WIDTH, tm=tm, tn=512, name="mm_in_g")
    hm, C, n, m = _mlstm(zm, zgate, p["m_conv_w"][0], p["ab_gate_b"][0], p["m_norm"][0],
                         m_conv[0], m_C[0], m_n[0], m_m[0])
    conv_new = zm[:, R - (M_CONV - 1):, :2 * M_WIDTH]
    og, S = _hgrn(zg, lbs[0], p["g_norm"][0], g_S[0])
    x = _mm_residual([hm, og], p["ab_w_out"][0], x, md, 2, tm=tm, tn=512, name="mm_out0")
    h = _modnorm(x, p["norm_ffn"][0], md, 4, 3, tm)
    act = _mm(h, p["ffn_w1"][0], n_cols=4 * D, tm=tm, tn=512, act="relu2", out_dtype=BF16, name="ffn_up")
    x = _mm_residual([act], p["ffn_w2"][0], x, md, 5, tm=min(tm, 512), tn=256, name="ffn_down")
    md = mod[1]
    (xr, xw, xk, xv, xa, xg), shift_new = _rwkv_prep(x, p["norm_mix"][1], md, 1, 0, r_shift[0], p["r_mu"][0],
                                                     min(tm, 256))
    r = _mm(xr, p["r_wr"][0], n_cols=D, tm=tm, tn=512, name="mm_r")
    k = _mm(xk, p["r_wk"][0], n_cols=D, tm=tm, tn=512, name="mm_k")
    v = _mm(xv, p["r_wv"][0], n_cols=D, tm=tm, tn=512, name="mm_v")
    wl = _lora(xw, p["r_w1"][0], p["r_w2"][0], "tanh", min(tm, 512))
    al = _lora(xa, p["r_a1"][0], p["r_a2"][0], None, min(tm, 512))
    gg = _lora(xg, p["r_g1"][0], p["r_g2"][0], "sigmoid", min(tm, 512))
    yg, rS = _rwkv(r, k, v, wl, al, gg, p["r_w0"][0], p["r_a0"][0], p["r_kk"][0], p["r_ka"][0],
                   p["r_rk"][0].reshape(-1), p["r_lnw"][0], p["r_lnb"][0], r_S[0])
    x = _mm_residual([yg], p["r_wo"][0], x, md, 2, tm=tm, tn=512, name="mm_out1")
    h = _modnorm(x, p["norm_ffn"][1], md, 4, 3, tm)
    act = _mm(h, p["ffn_w1"][1], n_cols=4 * D, tm=tm, tn=512, act="relu2", out_dtype=BF16, name="ffn_up")
    x = _mm_residual([act], p["ffn_w2"][1], x, md, 5, tm=min(tm, 512), tn=256, name="ffn_down")
    y = _rmsnorm(x, p["final_norm"], min(tm, 512))
    return y, (C[None], n[None], m.reshape(1, G, M_HEADS), conv_new[None], S[None], rS[None],
               shift_new.reshape(1, G, D))


def kernel(x_prompt, x_sample, c_prompt, c_sample, state_mlstm_C, state_mlstm_n, state_mlstm_m, state_mlstm_conv, state_hgrn_S, state_rwkv_S, state_rwkv_shift, mod_w, mod_b, norm_mix, norm_ffn, ffn_w1, ffn_w2, final_norm, ab_w_in, ab_gate_b, m_conv_w, m_norm, g_lb, g_norm, ab_w_out, r_mu, r_w0, r_w1, r_w2, r_a0, r_a1, r_a2, r_g1, r_g2, r_kk, r_ka, r_rk, r_wr, r_wk, r_wv, r_wo, r_lnw, r_lnb):
    p = dict(norm_mix=norm_mix, norm_ffn=norm_ffn, ffn_w1=ffn_w1, ffn_w2=ffn_w2, final_norm=final_norm,
             ab_w_in=ab_w_in, ab_gate_b=ab_gate_b, m_conv_w=m_conv_w, m_norm=m_norm, g_norm=g_norm,
             ab_w_out=ab_w_out, r_mu=r_mu, r_w0=r_w0, r_w1=r_w1, r_w2=r_w2, r_a0=r_a0, r_a1=r_a1, r_a2=r_a2,
             r_g1=r_g1, r_g2=r_g2, r_kk=r_kk, r_ka=r_ka, r_rk=r_rk, r_wr=r_wr, r_wk=r_wk, r_wv=r_wv,
             r_wo=r_wo, r_lnw=r_lnw, r_lnb=r_lnb)
    B, T, D = x_prompt.shape
    Bs = x_sample.shape[0]
    depth = mod_w.shape[0]
    pad = (-(Bs + B)) % SUBLANES
    c_all = jnp.concatenate([c_sample, c_prompt, jnp.zeros((pad, D), F32)], axis=0)
    mod = _modulation(c_all, mod_w, mod_b)
    mod_s = mod[:, :Bs].reshape(depth, Bs, 1, N_MOD * D)
    mod_p = mod[:, Bs:Bs + B].reshape(depth, B, 1, N_MOD * D)
    lbs = jnp.cumsum(jax.nn.softmax(g_lb.astype(F32), axis=0), axis=0)

    n_even = state_mlstm_C.shape[0]
    n_odd = state_rwkv_S.shape[0]
    z = lambda *s: jnp.zeros(s, F32)
    yp, sp = _trunk(x_prompt, mod_p,
                    z(n_even, B, M_HEADS, M_HEAD_DIM, M_HEAD_DIM), z(n_even, B, M_HEADS, M_HEAD_DIM),
                    z(n_even, B, M_HEADS), z(n_even, B, M_CONV - 1, 2 * M_WIDTH),
                    z(n_even, B, G_HEADS, G_HEAD_DIM, G_HEAD_DIM), z(n_odd, B, R_HEADS, R_HEAD_DIM, R_HEAD_DIM),
                    z(n_odd, B, D), p, lbs, 1024)
    ys, ss = _trunk(x_sample, mod_s, state_mlstm_C, state_mlstm_n, state_mlstm_m, state_mlstm_conv,
                    state_hgrn_S, state_rwkv_S, state_rwkv_shift, p, lbs, 1024)
    return (yp, ys) + tuple(sp) + tuple(ss)
```

```python
import functools
import math

import jax
import jax.numpy as jnp
from jax import lax
from jax.experimental import pallas as pl
from jax.experimental.pallas import tpu as pltpu

F32 = jnp.float32
BF16 = jnp.bfloat16

D_MODEL = 2048
M_HEADS = 4
M_HEAD_DIM = 256
M_WIDTH = M_HEADS * M_HEAD_DIM
M_CONV = 4
G_HEADS = 8
G_HEAD_DIM = 128
G_WIDTH = G_HEADS * G_HEAD_DIM
R_HEAD_DIM = 64
R_HEADS = D_MODEL // R_HEAD_DIM
R_GROUP = 4
R_GROUP_W = R_GROUP * R_HEAD_DIM
N_MOD = 6
RMS_EPS = 1e-6
LN_X_EPS = 64e-5
CHUNK = 64
NEG_BIG = -1e30

V7X_VMEM_LIMIT_BYTES = 56 * 1024 * 1024
SUBLANES = 8
LANES = 128


def _cparams(*sem):
    return pltpu.CompilerParams(dimension_semantics=sem, vmem_limit_bytes=V7X_VMEM_LIMIT_BYTES)


def _sigmoid(x):
    return 1.0 / (1.0 + jnp.exp(-x))


def _silu(x):
    return x * _sigmoid(x)


def _softplus(x):
    return jnp.maximum(x, 0.0) + jnp.log1p(jnp.exp(-jnp.abs(x)))


def _dot(a, b):
    return jnp.dot(a.astype(BF16), b.astype(BF16), preferred_element_type=F32)


def _dot_nt(a, b):
    return lax.dot_general(a.astype(BF16), b.astype(BF16), (((1,), (1,)), ((), ())), preferred_element_type=F32)


def _dot_tn(a, b):
    return lax.dot_general(a.astype(BF16), b.astype(BF16), (((0,), (0,)), ((), ())), preferred_element_type=F32)


def _cumsum_rows(x):
    n = x.shape[0]
    row = lax.broadcasted_iota(jnp.int32, x.shape, 0)
    s = 1
    while s < n:
        x = x + jnp.where(row >= s, pltpu.roll(x, s, axis=0), 0.0)
        s *= 2
    return x


def _row_blocking(G, R, tm):
    if R >= tm:
        assert R % tm == 0
        return 1, tm, R // tm
    assert tm % R == 0 and G % (tm // R) == 0
    return tm // R, R, 1


def _mod_body(c_ref, w_ref, b_ref, o_ref):
    sc = _silu(c_ref[...])
    o_ref[...] = _dot(sc, w_ref[...]) + b_ref[...]


def _modulation(c_all, mod_w, mod_b):
    L, K, N = mod_w.shape
    Mc = c_all.shape[0]
    tn = 1024
    return pl.pallas_call(
        _mod_body,
        grid=(L, N // tn),
        in_specs=[pl.BlockSpec((Mc, K), lambda l, j: (0, 0)),
                  pl.BlockSpec((None, K, tn), lambda l, j: (l, 0, j)),
                  pl.BlockSpec((None, 1, tn), lambda l, j: (l, 0, j))],
        out_specs=pl.BlockSpec((None, Mc, tn), lambda l, j: (l, 0, j)),
        out_shape=jax.ShapeDtypeStruct((L, Mc, N), F32),
        compiler_params=_cparams("arbitrary", "arbitrary"),
        name="modulation",
    )(c_all, mod_w, mod_b.reshape(L, 1, N))


def _rms(x, g):
    ms = jnp.mean(x * x, axis=-1, keepdims=True)
    return x * lax.rsqrt(ms + RMS_EPS) * g


def _modnorm_body(x_ref, g_ref, sc_ref, sh_ref, o_ref):
    y = _rms(x_ref[...], g_ref[...])
    o_ref[...] = (y * (1.0 + sc_ref[...]) + sh_ref[...]).astype(o_ref.dtype)


def _modnorm(x, g, mod, i_scale, i_shift, tm):
    G, R, D = x.shape
    gb, rb, nrb = _row_blocking(G, R, tm)
    return pl.pallas_call(
        _modnorm_body,
        grid=(G // gb, nrb),
        in_specs=[pl.BlockSpec((gb, rb, D), lambda a, b: (a, b, 0)),
                  pl.BlockSpec((1, 1, D), lambda a, b: (0, 0, 0)),
                  pl.BlockSpec((gb, 1, D), lambda a, b: (a, 0, i_scale)),
                  pl.BlockSpec((gb, 1, D), lambda a, b: (a, 0, i_shift))],
        out_specs=pl.BlockSpec((gb, rb, D), lambda a, b: (a, b, 0)),
        out_shape=jax.ShapeDtypeStruct((G, R, D), BF16),
        compiler_params=_cparams("arbitrary", "arbitrary"),
        name="modnorm",
    )(x, g.reshape(1, 1, D), mod, mod)


def _rmsnorm_body(x_ref, g_ref, o_ref):
    o_ref[...] = _rms(x_ref[...], g_ref[...])


def _rmsnorm(x, g, tm):
    G, R, D = x.shape
    gb, rb, nrb = _row_blocking(G, R, tm)
    return pl.pallas_call(
        _rmsnorm_body,
        grid=(G // gb, nrb),
        in_specs=[pl.BlockSpec((gb, rb, D), lambda a, b: (a, b, 0)),
                  pl.BlockSpec((1, 1, D), lambda a, b: (0, 0, 0))],
        out_specs=pl.BlockSpec((gb, rb, D), lambda a, b: (a, b, 0)),
        out_shape=jax.ShapeDtypeStruct((G, R, D), F32),
        compiler_params=_cparams("arbitrary", "arbitrary"),
        name="final_norm",
    )(x, g.reshape(1, 1, D))


def _mm_body(a_ref, w_ref, o_ref, wbf, *, act):
    @pl.when(pl.program_id(1) == 0)
    def _():
        wbf[...] = w_ref[...].astype(BF16)

    acc = jnp.dot(a_ref[...].astype(BF16), wbf[...], preferred_element_type=F32)
    if act == "relu2":
        acc = jnp.square(jnp.maximum(acc, 0.0))
    o_ref[...] = acc.astype(o_ref.dtype)


def _mm(a, w, layer, *, n_cols, tm, tn, act=None, out_dtype=F32, name="mm"):
    G, R, K = a.shape
    M = G * R
    assert M % tm == 0 and n_cols % tn == 0 and w.shape[1] == K
    out = pl.pallas_call(
        functools.partial(_mm_body, act=act),
        grid=(n_cols // tn, M // tm),
        in_specs=[pl.BlockSpec((tm, K), lambda j, i: (i, 0)),
                  pl.BlockSpec((None, K, tn), lambda j, i: (layer, 0, j))],
        out_specs=pl.BlockSpec((tm, tn), lambda j, i: (i, j)),
        out_shape=jax.ShapeDtypeStruct((M, n_cols), out_dtype),
        scratch_shapes=[pltpu.VMEM((K, tn), BF16)],
        compiler_params=_cparams("arbitrary", "arbitrary"),
        name=name,
    )(a.reshape(M, K), w)
    return out.reshape(G, R, n_cols)


def _mm_res_body(*refs, n_lhs, gb, rb, cast_w):
    a_refs = refs[:n_lhs]
    w_ref, x_ref, gt_ref, o_ref = refs[n_lhs:n_lhs + 4]
    if cast_w:
        wbf = refs[n_lhs + 4]

        @pl.when(pl.program_id(1) == 0)
        def _():
            wbf[...] = w_ref[...].astype(BF16)
    else:
        wbf = w_ref

    acc = None
    k0 = 0
    for a_ref in a_refs:
        kk = a_ref.shape[-1]
        part = jnp.dot(a_ref[...].astype(BF16), wbf[k0:k0 + kk, :], preferred_element_type=F32)
        acc = part if acc is None else acc + part
        k0 += kk
    tn = acc.shape[-1]
    o_ref[...] = x_ref[...] + gt_ref[...] * acc.reshape(gb, rb, tn)


def _mm_residual(a_list, w, layer, x, mod, i_gate, *, tm, tn, name="mm_res"):
    G, R, N = x.shape
    M = G * R
    K = w.shape[1]
    assert sum(a.shape[-1] for a in a_list) == K and w.shape[2] == N
    gb, rb, nrb = _row_blocking(G, R, tm)
    ntn = N // tn
    cast_w = w.dtype != BF16

    def xmap(j, i):
        return (i // nrb, i % nrb, j)

    return pl.pallas_call(
        functools.partial(_mm_res_body, n_lhs=len(a_list), gb=gb, rb=rb, cast_w=cast_w),
        grid=(ntn, M // tm),
        in_specs=[pl.BlockSpec((tm, a.shape[-1]), lambda j, i: (i, 0)) for a in a_list]
        + [pl.BlockSpec((None, K, tn), lambda j, i: (layer, 0, j)),
           pl.BlockSpec((gb, rb, tn), xmap),
           pl.BlockSpec((gb, 1, tn), lambda j, i: (i // nrb, 0, i_gate * ntn + j))],
        out_specs=pl.BlockSpec((gb, rb, tn), xmap),
        out_shape=jax.ShapeDtypeStruct((G, R, N), F32),
        scratch_shapes=[pltpu.VMEM((K, tn), BF16)] if cast_w else [],
        compiler_params=_cparams("arbitrary", "arbitrary"),
        name=name,
    )(*[a.reshape(M, a.shape[-1]) for a in a_list], w, x, mod)


def _mlstm_body(zq_ref, zk_ref, zv_ref, zo_ref, zg_ref, cw_ref, gbias_ref, gain_ref, conv0_ref,
                C0_ref, n0_ref, m0_ref, hm_ref, C_ref, n_ref, m_ref, ext, *, L):
    W = M_WIDTH
    Dh = M_HEAD_DIM
    keep = M_CONV - 1
    c = pl.program_id(1)

    @pl.when(c == 0)
    def _():
        C_ref[...] = C0_ref[...]
        n_ref[...] = n0_ref[...]
        m_ref[...] = m0_ref[...]
        ext[SUBLANES - keep:SUBLANES, :] = conv0_ref[...]

    ext[SUBLANES:SUBLANES + L, 0:W] = zq_ref[...]
    ext[SUBLANES:SUBLANES + L, W:2 * W] = zk_ref[...]
    conv = ext[SUBLANES:SUBLANES + L, :] * cw_ref[keep:keep + 1, :]
    for j in range(keep):
        conv = conv + ext[SUBLANES - keep + j:SUBLANES - keep + j + L, :] * cw_ref[j:j + 1, :]
    ext[SUBLANES - keep:SUBLANES, :] = ext[SUBLANES + L - keep:SUBLANES + L, :]
    qk = _silu(conv)

    gates = zg_ref[...] + gbias_ref[...]
    row = lax.broadcasted_iota(jnp.int32, (L, L), 0)
    col = lax.broadcasted_iota(jnp.int32, (L, L), 1)
    eye = row == col
    tri = col <= row

    for h in range(M_HEADS):
        sl = slice(h * Dh, (h + 1) * Dh)
        q = qk[:, h * Dh:(h + 1) * Dh]
        k = qk[:, W + h * Dh:W + (h + 1) * Dh] * (Dh ** -0.5)
        v = zv_ref[:, sl]
        ig_col = gates[:, h:h + 1]
        fpre = gates[:, M_HEADS + h:M_HEADS + h + 1]
        lf_col = jnp.minimum(fpre, 0.0) - jnp.log1p(jnp.exp(-jnp.abs(fpre)))
        lf_row = jnp.sum(jnp.where(eye, lf_col, 0.0), axis=0, keepdims=True)
        ig_row = jnp.sum(jnp.where(eye, ig_col, 0.0), axis=0, keepdims=True)
        b_col = jnp.sum(jnp.where(tri, lf_row, 0.0), axis=1, keepdims=True)
        b_row = jnp.sum(jnp.where(row <= col, lf_col, 0.0), axis=0, keepdims=True)
        dmat = jnp.where(tri, b_col - b_row + ig_row, NEG_BIG)
        m_prev = m_ref[:, h:h + 1]
        inter = b_col + m_prev
        m_t = jnp.maximum(inter, jnp.max(dmat, axis=1, keepdims=True))
        w_intra = jnp.exp(dmat - m_t)
        w_inter = jnp.exp(inter - m_t)
        Cm = C_ref[h]
        n_row = n_ref[h:h + 1, :]
        s = _dot_nt(q, k) * w_intra
        num = _dot(s, v) + w_inter * _dot(q, Cm)
        den = jnp.sum(s, axis=1, keepdims=True) + w_inter * jnp.sum(q * n_row, axis=1, keepdims=True)
        hh = num / jnp.maximum(jnp.abs(den), jnp.exp(-m_t))
        m_new = m_t[L - 1:L, :]
        b_last = b_col[L - 1:L, :]
        w_end = jnp.exp(b_last - b_col + ig_col - m_new)
        decay = jnp.exp(b_last + m_prev - m_new)
        wk = w_end * k
        C_ref[h] = decay * Cm + _dot_tn(wk, v)
        n_ref[h:h + 1, :] = decay * n_row + jnp.sum(wk, axis=0, keepdims=True)
        m_ref[:, h:h + 1] = m_new
        dlt = hh - jnp.mean(hh, axis=-1, keepdims=True)
        ln = dlt * lax.rsqrt(jnp.mean(dlt * dlt, axis=-1, keepdims=True) + RMS_EPS)
        hm_ref[:, sl] = _sigmoid(zo_ref[:, sl]) * ln * gain_ref[:, sl]


def _mlstm(zm, zgate, conv_w, gate_b, m_gain, conv0, C0, n0, m0):
    G, R, _ = zm.shape
    L = math.gcd(R, CHUNK)
    W = M_WIDTH
    gbias = jnp.zeros((1, LANES), F32).at[0, :2 * M_HEADS].set(gate_b.astype(F32))
    zspec = lambda blk: pl.BlockSpec((None, L, W), lambda g, c: (g, c, blk))
    st4 = lambda g, c: (g, 0, 0, 0)
    st3 = lambda g, c: (g, 0, 0)
    return pl.pallas_call(
        functools.partial(_mlstm_body, L=L),
        grid=(G, R // L),
        in_specs=[zspec(0), zspec(1), zspec(2), zspec(3),
                  pl.BlockSpec((None, L, LANES), lambda g, c: (g, c, 0)),
                  pl.BlockSpec((M_CONV, 2 * W), lambda g, c: (0, 0)),
                  pl.BlockSpec((1, LANES), lambda g, c: (0, 0)),
                  pl.BlockSpec((1, W), lambda g, c: (0, 0)),
                  pl.BlockSpec((None, M_CONV - 1, 2 * W), st3),
                  pl.BlockSpec((None, M_HEADS, M_HEAD_DIM, M_HEAD_DIM), st4),
                  pl.BlockSpec((None, M_HEADS, M_HEAD_DIM), st3),
                  pl.BlockSpec((None, 1, M_HEADS), st3)],
        out_specs=[pl.BlockSpec((None, L, W), lambda g, c: (g, c, 0)),
                   pl.BlockSpec((None, M_HEADS, M_HEAD_DIM, M_HEAD_DIM), st4),
                   pl.BlockSpec((None, M_HEADS, M_HEAD_DIM), st3),
                   pl.BlockSpec((None, 1, M_HEADS), st3)],
        out_shape=[jax.ShapeDtypeStruct((G, R, W), F32),
                   jax.ShapeDtypeStruct((G, M_HEADS, M_HEAD_DIM, M_HEAD_DIM), F32),
                   jax.ShapeDtypeStruct((G, M_HEADS, M_HEAD_DIM), F32),
                   jax.ShapeDtypeStruct((G, 1, M_HEADS), F32)],
        scratch_shapes=[pltpu.VMEM((SUBLANES + L, 2 * W), F32)],
        compiler_params=_cparams("arbitrary", "arbitrary"),
        name="mlstm",
    )(zm, zm, zm, zm, zgate, conv_w, gbias, m_gain.reshape(1, W), conv0, C0, n0, m0.reshape(G, 1, M_HEADS))


def _hgrn_body(zq_ref, zf_ref, zi_ref, zgg_ref, lb_ref, gain_ref, S0_ref, og_ref, S_ref, ST, *, nblk):
    Dh = G_HEAD_DIM
    B = SUBLANES
    c = pl.program_id(1)

    @pl.when(c == 0)
    def _():
        for h in range(G_HEADS):
            ST[h] = S0_ref[h].T

    rowi = lax.broadcasted_iota(jnp.int32, (B, Dh), 0)

    def blk(i, carry):
        r0 = pl.multiple_of(i * B, B)
        for h in range(G_HEADS):
            sl = slice(h * Dh, (h + 1) * Dh)
            lb = lb_ref[:, sl]
            f = lb + (1.0 - lb) * _sigmoid(zf_ref[pl.ds(r0, B), sl])
            lg = jnp.log(f)
            kk = 1.0 - f
            qh = _silu(zq_ref[pl.ds(r0, B), sl]) * (Dh ** -0.5)
            v = zi_ref[pl.ds(r0, B), sl]
            bc = _cumsum_rows(lg)
            btot = bc[B - 1:B, :]
            st = ST[h]
            o = _dot_nt(qh * jnp.exp(bc), st)
            for s in range(B):
                diff = jnp.where(rowi >= s, bc - bc[s:s + 1, :], NEG_BIG)
                p = jnp.exp(diff) * qh * kk[s:s + 1, :]
                o = o + jnp.sum(p, axis=-1, keepdims=True) * v[s:s + 1, :]
            khat = kk * jnp.exp(btot - bc)
            ST[h] = st * jnp.exp(btot) + _dot_tn(v, khat)
            rms = o * lax.rsqrt(jnp.mean(o * o, axis=-1, keepdims=True) + RMS_EPS)
            og_ref[pl.ds(r0, B), sl] = rms * gain_ref[:, sl] * _silu(zgg_ref[pl.ds(r0, B), sl])
        return carry

    lax.fori_loop(0, nblk, blk, 0, unroll=min(2, nblk))

    @pl.when(c == pl.num_programs(1) - 1)
    def _():
        for h in range(G_HEADS):
            S_ref[h] = ST[h].T


def _hgrn(zg, lb, g_gain, S0):
    G, R, _ = zg.shape
    W = G_WIDTH
    Rc = min(R, 256)
    zspec = lambda blk: pl.BlockSpec((None, Rc, W), lambda g, c: (g, c, blk))
    st4 = lambda g, c: (g, 0, 0, 0)
    return pl.pallas_call(
        functools.partial(_hgrn_body, nblk=Rc // SUBLANES),
        grid=(G, R // Rc),
        in_specs=[zspec(0), zspec(1), zspec(2), zspec(3),
                  pl.BlockSpec((1, W), lambda g, c: (0, 0)),
                  pl.BlockSpec((1, W), lambda g, c: (0, 0)),
                  pl.BlockSpec((None, G_HEADS, G_HEAD_DIM, G_HEAD_DIM), st4)],
        out_specs=[pl.BlockSpec((None, Rc, W), lambda g, c: (g, c, 0)),
                   pl.BlockSpec((None, G_HEADS, G_HEAD_DIM, G_HEAD_DIM), st4)],
        out_shape=[jax.ShapeDtypeStruct((G, R, W), F32),
                   jax.ShapeDtypeStruct((G, G_HEADS, G_HEAD_DIM, G_HEAD_DIM), F32)],
        scratch_shapes=[pltpu.VMEM((G_HEADS, G_HEAD_DIM, G_HEAD_DIM), F32)],
        compiler_params=_cparams("arbitrary", "arbitrary"),
        name="hgrn2",
    )(zg, zg, zg, zg, lb.reshape(1, W), g_gain.reshape(1, W), S0)


def _rprep_body(x_ref, g_ref, sc_ref, sh_ref, shift0_ref, mu_ref, *refs, rb):
    outs = refs[:6]
    hlast_ref, hbuf = refs[6:]
    B = SUBLANES

    @pl.when(pl.program_id(1) == 0)
    def _():
        hbuf[:, B - 1:B, :] = shift0_ref[...]

    h = _rms(x_ref[...], g_ref[...]) * (1.0 + sc_ref[...]) + sh_ref[...]
    hbuf[:, B:B + rb, :] = h
    prev = hbuf[:, B - 1:B - 1 + rb, :]
    hbuf[:, B - 1:B, :] = h[:, rb - 1:rb, :]
    xx = prev - h
    for i in range(6):
        outs[i][...] = (h + xx * mu_ref[i:i + 1, :]).astype(BF16)
    hlast_ref[...] = h[:, rb - 1:rb, :]


def _rwkv_prep(x, g, mod, i_scale, i_shift, shift0, mu, tm):
    G, R, D = x.shape
    gb, rb, nrb = _row_blocking(G, R, tm)
    tok = pl.BlockSpec((gb, rb, D), lambda a, b: (a, b, 0))
    one = pl.BlockSpec((gb, 1, D), lambda a, b: (a, 0, 0))
    res = pl.pallas_call(
        functools.partial(_rprep_body, rb=rb),
        grid=(G // gb, nrb),
        in_specs=[tok,
                  pl.BlockSpec((1, 1, D), lambda a, b: (0, 0, 0)),
                  pl.BlockSpec((gb, 1, D), lambda a, b: (a, 0, i_scale)),
                  pl.BlockSpec((gb, 1, D), lambda a, b: (a, 0, i_shift)),
                  one,
                  pl.BlockSpec((6, D), lambda a, b: (0, 0))],
        out_specs=[tok] * 6 + [one],
        out_shape=[jax.ShapeDtypeStruct((G, R, D), BF16)] * 6 + [jax.ShapeDtypeStruct((G, 1, D), F32)],
        scratch_shapes=[pltpu.VMEM((gb, SUBLANES + rb, D), F32)],
        compiler_params=_cparams("arbitrary", "arbitrary"),
        name="rwkv_prep",
    )(x, g.reshape(1, 1, D), mod, mod, shift0.reshape(G, 1, D), mu)
    return res[:6], res[6]


def _lora_body(x_ref, w1_ref, w2_ref, o_ref, *, act):
    t = _dot(x_ref[...], w1_ref[...])
    if act == "tanh":
        t = jnp.tanh(t)
    elif act == "sigmoid":
        t = _sigmoid(t)
    o_ref[...] = _dot(t, w2_ref[...])


def _lora(x, w1, w2, act, tm):
    G, R, K = x.shape
    M = G * R
    r = w1.shape[1]
    rp = -(-r // LANES) * LANES
    w1p = jnp.zeros((K, rp), F32).at[:, :r].set(w1)
    w2p = jnp.zeros((rp, w2.shape[1]), F32).at[:r, :].set(w2)
    N = w2.shape[1]
    out = pl.pallas_call(
        functools.partial(_lora_body, act=act),
        grid=(M // tm,),
        in_specs=[pl.BlockSpec((tm, K), lambda i: (i, 0)),
                  pl.BlockSpec((K, rp), lambda i: (0, 0)),
                  pl.BlockSpec((rp, N), lambda i: (0, 0))],
        out_specs=pl.BlockSpec((tm, N), lambda i: (i, 0)),
        out_shape=jax.ShapeDtypeStruct((M, N), F32),
        compiler_params=_cparams("arbitrary"),
        name="lora_" + str(act),
    )(x.reshape(M, K), w1p, w2p)
    return out.reshape(G, R, N)


def _head_mask(rows, cols, rper, cper):
    r = lax.broadcasted_iota(jnp.int32, (rows, cols), 0)
    c = lax.broadcasted_iota(jnp.int32, (rows, cols), 1)
    return (r // rper) == (c // cper)


def _bd(y, mask01):
    if y.shape[0] % (2 * SUBLANES) == 0:
        return jnp.concatenate([y.astype(BF16)] * R_GROUP, axis=0) * mask01
    return jnp.concatenate([y] * R_GROUP, axis=0).astype(BF16) * mask01


def _segsums(xs, ones_bd):
    c = xs[0].shape[0]
    parts = []
    for x in xs:
        hi = x.astype(BF16).astype(F32)
        parts += [hi, x - hi]
    res = jnp.dot(jnp.concatenate(parts, axis=0).astype(BF16), ones_bd, preferred_element_type=F32)
    return [res[2 * i * c:(2 * i + 1) * c] + res[(2 * i + 1) * c:(2 * i + 2) * c] for i in range(len(xs))]


def _rwkv_body(r_ref, k_ref, v_ref, wl_ref, al_ref, g_ref, w0_ref, a0_ref, kkp_ref, kap_ref, rk_ref,
               lnw_ref, lnb_ref, S0_ref, y_ref, S_ref, Sbd, *, c, nchunk, ng):
    N = R_HEAD_DIM
    GW = R_GROUP_W
    j = pl.program_id(2)
    m_state = _head_mask(GW, GW, N, N)

    @pl.when(j == 0)
    def _():
        for gi in range(ng):
            s0 = S0_ref[gi * R_GROUP:(gi + 1) * R_GROUP].reshape(GW, N)
            Sbd[gi] = jnp.where(m_state, jnp.concatenate([s0] * R_GROUP, axis=1), 0.0)

    ones_bd = jnp.where(m_state, 1.0, 0.0).astype(BF16)
    m_vec = jnp.where(_head_mask(R_GROUP * c, GW, c, N), 1.0, 0.0).astype(BF16)
    m_mat = jnp.where(_head_mask(R_GROUP * c, R_GROUP * c, c, c), 1.0, 0.0).astype(BF16)
    t_idx = lax.broadcasted_iota(jnp.int32, (2 * c, R_GROUP * c), 0)
    s_idx = lax.broadcasted_iota(jnp.int32, (2 * c, R_GROUP * c), 1) % c
    causal = jnp.where(t_idx < c, jnp.where(s_idx < t_idx, 1.0, 0.0), jnp.where(s_idx <= t_idx - c, 1.0, 0.0))

    def chunk(ci, carry):
        r0 = pl.multiple_of(ci * c, c)
        rows = pl.ds(r0, c)
        gs = [slice(gi * GW, (gi + 1) * GW) for gi in range(ng)]
        grp = range(ng)
        nt = (((1,), (1,)), ((), ()))
        mm = lambda a, b: jnp.dot(a.astype(BF16), b, preferred_element_type=F32)
        r = [r_ref[rows, s] for s in gs]
        k = [k_ref[rows, s] for s in gs]
        v = [v_ref[rows, s] for s in gs]
        lw = [-jnp.exp(-_softplus(-(w0_ref[:, s] + wl_ref[rows, s])) - 0.5) for s in gs]
        a = [_sigmoid(a0_ref[:, s] + al_ref[rows, s]) for s in gs]
        kk = [k[i] * kkp_ref[:, gs[i]] for i in grp]
        k2 = [k[i] * (1.0 + (a[i] - 1.0) * kap_ref[:, gs[i]]) for i in grp]
        sums = [_segsums([kk[i] * kk[i], r[i] * k2[i] * rk_ref[:, gs[i]]], ones_bd) for i in grp]
        kk = [kk[i] / jnp.maximum(jnp.sqrt(sums[i][0]), 1e-12) for i in grp]
        bb = [kk[i] * a[i] for i in grp]
        cw = [_cumsum_rows(x) for x in lw]
        cwl = [x[c - 1:c, :] for x in cw]
        e_out = [jnp.exp(-x) for x in cw]
        e_end = [jnp.exp(cwl[i] - cw[i]) for i in grp]
        ar = [jnp.concatenate([-kk[i] * jnp.exp(cw[i] - lw[i]), r[i] * jnp.exp(cw[i])], axis=0).astype(BF16)
              for i in grp]
        bdb = [_bd(bb[i] * e_out[i], m_vec) for i in grp]
        bdk = [_bd(k2[i] * e_out[i], m_vec) for i in grp]
        bdv = [_bd(v[i], m_vec) for i in grp]
        sbd = [Sbd[i] for i in grp]
        pb = [lax.dot_general(ar[i], bdb[i], nt, preferred_element_type=F32) * causal for i in grp]
        pk = [lax.dot_general(ar[i], bdk[i], nt, preferred_element_type=F32) * causal for i in grp]
        s0p = [lax.dot_general(ar[i], sbd[i].astype(BF16), nt, preferred_element_type=F32) for i in grp]
        u = [s0p[i][:c] + mm(pk[i][:c], bdv[i]) for i in grp]
        p = [pb[i][:c] for i in grp]
        step = 1
        while step < c:
            bdu = [_bd(u[i], m_vec) for i in grp]
            u = [u[i] + mm(p[i], bdu[i]) for i in grp]
            step *= 2
            if step < c:
                bdp = [_bd(p[i], m_mat) for i in grp]
                p = [mm(p[i], bdp[i]) for i in grp]
        bdu = [_bd(u[i], m_vec) for i in grp]
        y = [s0p[i][c:] + mm(pb[i][c:], bdu[i]) + mm(pk[i][c:], bdv[i]) for i in grp]
        upd = [_dot_tn(jnp.concatenate([u[i], v[i]], axis=0),
                       jnp.concatenate([bb[i] * e_end[i], k2[i] * e_end[i]], axis=0)) for i in grp]
        for i in grp:
            Sbd[i] = sbd[i] * jnp.exp(cwl[i]) + jnp.where(m_state, upd[i], 0.0)
        ysum = [_segsums([y[i]], ones_bd)[0] for i in grp]
        dlt = [y[i] - ysum[i] * (1.0 / N) for i in grp]
        vsum = [_segsums([dlt[i] * dlt[i]], ones_bd)[0] for i in grp]
        for i in grp:
            yn = dlt[i] * lax.rsqrt(vsum[i] * (1.0 / N) + LN_X_EPS) * lnw_ref[:, gs[i]] + lnb_ref[:, gs[i]]
            y_ref[rows, gs[i]] = ((yn + sums[i][1] * v[i]) * g_ref[rows, gs[i]]).astype(y_ref.dtype)
        return carry

    lax.fori_loop(0, nchunk, chunk, 0)

    @pl.when(j == pl.num_programs(2) - 1)
    def _():
        for gi in range(ng):
            sbd = Sbd[gi]
            for h in range(R_GROUP):
                S_ref[gi * R_GROUP + h] = sbd[h * N:(h + 1) * N, h * N:(h + 1) * N]


def _rwkv(r, k, v, wl, al, g, w0, a0, kkp, kap, rk, lnw, lnb, S0):
    G, R, D = r.shape
    c = math.gcd(R, CHUNK)
    Rc = min(R, 256)
    GW = R_GROUP_W
    ng = 4 if R > SUBLANES else D // GW
    bw = ng * GW
    tok = pl.BlockSpec((None, Rc, bw), lambda a, b, j: (a, j, b))
    par = pl.BlockSpec((1, bw), lambda a, b, j: (0, b))
    st = pl.BlockSpec((None, ng * R_GROUP, R_HEAD_DIM, R_HEAD_DIM), lambda a, b, j: (a, b, 0, 0))
    row = lambda p: p.reshape(1, D).astype(F32)
    return pl.pallas_call(
        functools.partial(_rwkv_body, c=c, nchunk=Rc // c, ng=ng),
        grid=(G, D // bw, R // Rc),
        in_specs=[tok] * 6 + [par] * 7 + [st],
        out_specs=[tok, st],
        out_shape=[jax.ShapeDtypeStruct((G, R, D), BF16),
                   jax.ShapeDtypeStruct((G, R_HEADS, R_HEAD_DIM, R_HEAD_DIM), F32)],
        scratch_shapes=[pltpu.VMEM((ng, GW, GW), F32)],
        compiler_params=_cparams("arbitrary", "arbitrary", "arbitrary"),
        name="rwkv7",
    )(r, k, v, wl, al, g, row(w0), row(a0), row(kkp), row(kap), row(rk), row(lnw), row(lnb), S0)


def _trunk(x, mod, m_C, m_n, m_m, m_conv, g_S, r_S, r_shift, p, lbs, tm):
    G, R, D = x.shape
    md = mod[0]
    h = _modnorm(x, p["norm_mix"][0], md, 1, 0, tm)
    w_in = p["ab_w_in"]
    w_gate = jnp.zeros((1, D, LANES), F32).at[0, :, :2 * M_HEADS].set(w_in[0, :, 4 * M_WIDTH:4 * M_WIDTH + 2 * M_HEADS])
    w_g = w_in[:1, :, 4 * M_WIDTH + 2 * M_HEADS:]
    zm = _mm(h, w_in, 0, n_cols=4 * M_WIDTH, tm=tm, tn=1024, name="mm_in_m")
    zgate = _mm(h, w_gate, 0, n_cols=LANES, tm=tm, tn=LANES, name="mm_in_gate")
    zg = _mm(h, w_g, 0, n_cols=4 * G_WIDTH, tm=tm, tn=1024, name="mm_in_g")
    hm, C, n, m = _mlstm(zm, zgate, p["m_conv_w"][0], p["ab_gate_b"][0], p["m_norm"][0],
                         m_conv[0], m_C[0], m_n[0], m_m[0])
    conv_new = zm[:, R - (M_CONV - 1):, :2 * M_WIDTH]
    og, S = _hgrn(zg, lbs[0], p["g_norm"][0], g_S[0])
    x = _mm_residual([hm, og], p["ab_w_out"], 0, x, md, 2, tm=tm, tn=512, name="mm_out0")
    h = _modnorm(x, p["norm_ffn"][0], md, 4, 3, tm)
    act = _mm(h, p["ffn_w1"], 0, n_cols=4 * D, tm=tm, tn=1024, act="relu2", out_dtype=BF16, name="ffn_up")
    x = _mm_residual([act], p["ffn_w2_bf16"], 0, x, md, 5, tm=256, tn=1024, name="ffn_down")
    md = mod[1]
    (xr, xw, xk, xv, xa, xg), shift_new = _rwkv_prep(x, p["norm_mix"][1], md, 1, 0, r_shift[0], p["r_mu"][0],
                                                     min(tm, 256))
    r = _mm(xr, p["r_wr"], 0, n_cols=D, tm=tm, tn=1024, name="mm_r")
    k = _mm(xk, p["r_wk"], 0, n_cols=D, tm=tm, tn=1024, name="mm_k")
    v = _mm(xv, p["r_wv"], 0, n_cols=D, tm=tm, tn=1024, name="mm_v")
    wl = _lora(xw, p["r_w1"][0], p["r_w2"][0], "tanh", min(tm, 512))
    al = _lora(xa, p["r_a1"][0], p["r_a2"][0], None, min(tm, 512))
    gg = _lora(xg, p["r_g1"][0], p["r_g2"][0], "sigmoid", min(tm, 512))
    yg, rS = _rwkv(r, k, v, wl, al, gg, p["r_w0"][0], p["r_a0"][0], p["r_kk"][0], p["r_ka"][0],
                   p["r_rk"][0].reshape(-1), p["r_lnw"][0], p["r_lnb"][0], r_S[0])
    x = _mm_residual([yg], p["r_wo"], 0, x, md, 2, tm=tm, tn=1024, name="mm_out1")
    h = _modnorm(x, p["norm_ffn"][1], md, 4, 3, tm)
    act = _mm(h, p["ffn_w1"], 1, n_cols=4 * D, tm=tm, tn=1024, act="relu2", out_dtype=BF16, name="ffn_up")
    x = _mm_residual([act], p["ffn_w2_bf16"], 1, x, md, 5, tm=256, tn=1024, name="ffn_down")
    y = _rmsnorm(x, p["final_norm"], min(tm, 512))
    return y, (C[None], n[None], m.reshape(1, G, M_HEADS), conv_new[None], S[None], rS[None],
               shift_new.reshape(1, G, D))


def kernel(x_prompt, x_sample, c_prompt, c_sample, state_mlstm_C, state_mlstm_n, state_mlstm_m, state_mlstm_conv, state_hgrn_S, state_rwkv_S, state_rwkv_shift, mod_w, mod_b, norm_mix, norm_ffn, ffn_w1, ffn_w2, final_norm, ab_w_in, ab_gate_b, m_conv_w, m_norm, g_lb, g_norm, ab_w_out, r_mu, r_w0, r_w1, r_w2, r_a0, r_a1, r_a2, r_g1, r_g2, r_kk, r_ka, r_rk, r_wr, r_wk, r_wv, r_wo, r_lnw, r_lnb):
    p = dict(norm_mix=norm_mix, norm_ffn=norm_ffn, ffn_w1=ffn_w1, ffn_w2=ffn_w2, final_norm=final_norm,
             ab_w_in=ab_w_in, ab_gate_b=ab_gate_b, m_conv_w=m_conv_w, m_norm=m_norm, g_norm=g_norm,
             ab_w_out=ab_w_out, r_mu=r_mu, r_w0=r_w0, r_w1=r_w1, r_w2=r_w2, r_a0=r_a0, r_a1=r_a1, r_a2=r_a2,
             r_g1=r_g1, r_g2=r_g2, r_kk=r_kk, r_ka=r_ka, r_rk=r_rk, r_wr=r_wr, r_wk=r_wk, r_wv=r_wv,
             r_wo=r_wo, r_lnw=r_lnw, r_lnb=r_lnb)
    p["ffn_w2_bf16"] = ffn_w2.astype(BF16)
    B, T, D = x_prompt.shape
    Bs = x_sample.shape[0]
    depth = mod_w.shape[0]
    pad = (-(Bs + B)) % SUBLANES
    c_all = jnp.concatenate([c_sample, c_prompt, jnp.zeros((pad, D), F32)], axis=0)
    mod = _modulation(c_all, mod_w, mod_b)
    mod_s = mod[:, :Bs].reshape(depth, Bs, 1, N_MOD * D)
    mod_p = mod[:, Bs:Bs + B].reshape(depth, B, 1, N_MOD * D)
    lbs = jnp.cumsum(jax.nn.softmax(g_lb.astype(F32), axis=0), axis=0)

    n_even = state_mlstm_C.shape[0]
    n_odd = state_rwkv_S.shape[0]
    z = lambda *s: jnp.zeros(s, F32)
    yp, sp = _trunk(x_prompt, mod_p,
                    z(n_even, B, M_HEADS, M_HEAD_DIM, M_HEAD_DIM), z(n_even, B, M_HEADS, M_HEAD_DIM),
                    z(n_even, B, M_HEADS), z(n_even, B, M_CONV - 1, 2 * M_WIDTH),
                    z(n_even, B, G_HEADS, G_HEAD_DIM, G_HEAD_DIM), z(n_odd, B, R_HEADS, R_HEAD_DIM, R_HEAD_DIM),
                    z(n_odd, B, D), p, lbs, 1024)
    ys, ss = _trunk(x_sample, mod_s, state_mlstm_C, state_mlstm_n, state_mlstm_m, state_mlstm_conv,
                    state_hgrn_S, state_rwkv_S, state_rwkv_shift, p, lbs, 1024)
    return (yp, ys) + tuple(sp) + tuple(ss)
```

```python
import functools
import math

import jax
import jax.numpy as jnp
from jax import lax
from jax.experimental import pallas as pl
from jax.experimental.pallas import tpu as pltpu

F32 = jnp.float32
BF16 = jnp.bfloat16

D_MODEL = 2048
M_HEADS = 4
M_HEAD_DIM = 256
M_WIDTH = M_HEADS * M_HEAD_DIM
M_CONV = 4
G_HEADS = 8
G_HEAD_DIM = 128
G_WIDTH = G_HEADS * G_HEAD_DIM
R_HEAD_DIM = 64
R_HEADS = D_MODEL // R_HEAD_DIM
R_GROUP = 4
R_GROUP_W = R_GROUP * R_HEAD_DIM
N_MOD = 6
RMS_EPS = 1e-6
LN_X_EPS = 64e-5
CHUNK = 64
NEG_BIG = -1e30

V7X_VMEM_LIMIT_BYTES = 56 * 1024 * 1024
SUBLANES = 8
LANES = 128


def _cparams(*sem):
    return pltpu.CompilerParams(dimension_semantics=sem, vmem_limit_bytes=V7X_VMEM_LIMIT_BYTES)


def _sigmoid(x):
    return 1.0 / (1.0 + jnp.exp(-x))


def _silu(x):
    return x * _sigmoid(x)


def _softplus(x):
    return jnp.maximum(x, 0.0) + jnp.log1p(jnp.exp(-jnp.abs(x)))


def _dot(a, b):
    return jnp.dot(a.astype(BF16), b.astype(BF16), preferred_element_type=F32)


def _dot_nt(a, b):
    return lax.dot_general(a.astype(BF16), b.astype(BF16), (((1,), (1,)), ((), ())), preferred_element_type=F32)


def _dot_tn(a, b):
    return lax.dot_general(a.astype(BF16), b.astype(BF16), (((0,), (0,)), ((), ())), preferred_element_type=F32)


def _cumsum_rows(x):
    n = x.shape[0]
    row = lax.broadcasted_iota(jnp.int32, x.shape, 0)
    s = 1
    while s < n:
        x = x + jnp.where(row >= s, pltpu.roll(x, s, axis=0), 0.0)
        s *= 2
    return x


def _row_blocking(G, R, tm):
    if R >= tm:
        assert R % tm == 0
        return 1, tm, R // tm
    assert tm % R == 0 and G % (tm // R) == 0
    return tm // R, R, 1


def _mod_body(c_ref, w_ref, b_ref, o_ref):
    sc = _silu(c_ref[...])
    o_ref[...] = _dot(sc, w_ref[...]) + b_ref[...]


def _modulation(c_all, mod_w, mod_b):
    L, K, N = mod_w.shape
    Mc = c_all.shape[0]
    tn = 1024
    return pl.pallas_call(
        _mod_body,
        grid=(L, N // tn),
        in_specs=[pl.BlockSpec((Mc, K), lambda l, j: (0, 0)),
                  pl.BlockSpec((None, K, tn), lambda l, j: (l, 0, j)),
                  pl.BlockSpec((None, 1, tn), lambda l, j: (l, 0, j))],
        out_specs=pl.BlockSpec((None, Mc, tn), lambda l, j: (l, 0, j)),
        out_shape=jax.ShapeDtypeStruct((L, Mc, N), F32),
        compiler_params=_cparams("arbitrary", "arbitrary"),
        name="modulation",
    )(c_all, mod_w, mod_b.reshape(L, 1, N))


def _rms(x, g):
    ms = jnp.mean(x * x, axis=-1, keepdims=True)
    return x * lax.rsqrt(ms + RMS_EPS) * g


STRIP_ROWS = 2 * SUBLANES


def _strips(gb, rb):
    if gb == 1:
        n = rb // STRIP_ROWS
        tok = lambda i: (slice(None), pl.ds(pl.multiple_of(i * STRIP_ROWS, STRIP_ROWS), STRIP_ROWS), slice(None))
        seq = lambda i: (slice(None), slice(None), slice(None))
    else:
        assert STRIP_ROWS % rb == 0
        per = STRIP_ROWS // rb
        n = gb // per
        tok = seq = lambda i: (pl.ds(i * per, per), slice(None), slice(None))
    return n, tok, seq


def _modnorm_body(x_ref, g_ref, sc_ref, sh_ref, o_ref, *, gb, rb):
    n, tok, seq = _strips(gb, rb)

    def strip(i, carry):
        y = _rms(x_ref[tok(i)], g_ref[...])
        o_ref[tok(i)] = (y * (1.0 + sc_ref[seq(i)]) + sh_ref[seq(i)]).astype(o_ref.dtype)
        return carry

    lax.fori_loop(0, n, strip, 0, unroll=4)


def _modnorm(x, g, mod, i_scale, i_shift, tm):
    G, R, D = x.shape
    gb, rb, nrb = _row_blocking(G, R, tm)
    return pl.pallas_call(
        functools.partial(_modnorm_body, gb=gb, rb=rb),
        grid=(G // gb, nrb),
        in_specs=[pl.BlockSpec((gb, rb, D), lambda a, b: (a, b, 0)),
                  pl.BlockSpec((1, 1, D), lambda a, b: (0, 0, 0)),
                  pl.BlockSpec((gb, 1, D), lambda a, b: (a, 0, i_scale)),
                  pl.BlockSpec((gb, 1, D), lambda a, b: (a, 0, i_shift))],
        out_specs=pl.BlockSpec((gb, rb, D), lambda a, b: (a, b, 0)),
        out_shape=jax.ShapeDtypeStruct((G, R, D), BF16),
        compiler_params=_cparams("arbitrary", "arbitrary"),
        name="modnorm",
    )(x, g.reshape(1, 1, D), mod, mod)


def _rmsnorm_body(x_ref, g_ref, o_ref, *, gb, rb):
    n, tok, _ = _strips(gb, rb)

    def strip(i, carry):
        o_ref[tok(i)] = _rms(x_ref[tok(i)], g_ref[...])
        return carry

    lax.fori_loop(0, n, strip, 0, unroll=4)


def _rmsnorm(x, g, tm):
    G, R, D = x.shape
    gb, rb, nrb = _row_blocking(G, R, tm)
    return pl.pallas_call(
        functools.partial(_rmsnorm_body, gb=gb, rb=rb),
        grid=(G // gb, nrb),
        in_specs=[pl.BlockSpec((gb, rb, D), lambda a, b: (a, b, 0)),
                  pl.BlockSpec((1, 1, D), lambda a, b: (0, 0, 0))],
        out_specs=pl.BlockSpec((gb, rb, D), lambda a, b: (a, b, 0)),
        out_shape=jax.ShapeDtypeStruct((G, R, D), F32),
        compiler_params=_cparams("arbitrary", "arbitrary"),
        name="final_norm",
    )(x, g.reshape(1, 1, D))


def _mm_body(a_ref, w_ref, o_ref, wbf, *, act, w_transposed):
    @pl.when(pl.program_id(1) == 0)
    def _():
        wbf[...] = (w_ref[0].T if w_transposed else w_ref[...]).astype(BF16)

    acc = jnp.dot(a_ref[...].astype(BF16), wbf[...], preferred_element_type=F32)
    if act == "relu2":
        acc = jnp.square(jnp.maximum(acc, 0.0))
    o_ref[...] = acc.astype(o_ref.dtype)


def _mm(a, w, layer, *, n_cols, tm, tn, act=None, out_dtype=F32, name="mm", w_transposed=False, col0=0):
    G, R, K = a.shape
    M = G * R
    assert M % tm == 0 and n_cols % tn == 0 and w.shape[2 if w_transposed else 1] == K
    if w_transposed:
        assert col0 % SUBLANES == 0
        w_spec = pl.BlockSpec((pl.Element(1), pl.Element(tn), pl.Element(K)),
                              lambda j, i: (layer, pl.multiple_of(col0 + j * tn, SUBLANES), 0))
    else:
        assert col0 == 0
        w_spec = pl.BlockSpec((None, K, tn), lambda j, i: (layer, 0, j))
    out = pl.pallas_call(
        functools.partial(_mm_body, act=act, w_transposed=w_transposed),
        grid=(n_cols // tn, M // tm),
        in_specs=[pl.BlockSpec((tm, K), lambda j, i: (i, 0)), w_spec],
        out_specs=pl.BlockSpec((tm, tn), lambda j, i: (i, j)),
        out_shape=jax.ShapeDtypeStruct((M, n_cols), out_dtype),
        scratch_shapes=[pltpu.VMEM((K, tn), BF16)],
        compiler_params=_cparams("arbitrary", "arbitrary"),
        name=name,
    )(a.reshape(M, K), w)
    return out.reshape(G, R, n_cols)


def _mm_res_body(*refs, n_lhs, gb, rb, cast_w):
    a_refs = refs[:n_lhs]
    w_ref, x_ref, gt_ref, o_ref = refs[n_lhs:n_lhs + 4]
    if cast_w:
        wbf = refs[n_lhs + 4]

        @pl.when(pl.program_id(1) == 0)
        def _():
            wbf[...] = w_ref[...].astype(BF16)
    else:
        wbf = w_ref

    acc = None
    k0 = 0
    for a_ref in a_refs:
        kk = a_ref.shape[-1]
        part = jnp.dot(a_ref[...].astype(BF16), wbf[k0:k0 + kk, :], preferred_element_type=F32)
        acc = part if acc is None else acc + part
        k0 += kk
    tn = acc.shape[-1]
    o_ref[...] = x_ref[...] + gt_ref[...] * acc.reshape(gb, rb, tn)


def _mm_residual(a_list, w, layer, x, mod, i_gate, *, tm, tn, name="mm_res"):
    G, R, N = x.shape
    M = G * R
    K = w.shape[1]
    assert sum(a.shape[-1] for a in a_list) == K and w.shape[2] == N
    gb, rb, nrb = _row_blocking(G, R, tm)
    ntn = N // tn
    cast_w = w.dtype != BF16

    def xmap(j, i):
        return (i // nrb, i % nrb, j)

    return pl.pallas_call(
        functools.partial(_mm_res_body, n_lhs=len(a_list), gb=gb, rb=rb, cast_w=cast_w),
        grid=(ntn, M // tm),
        in_specs=[pl.BlockSpec((tm, a.shape[-1]), lambda j, i: (i, 0)) for a in a_list]
        + [pl.BlockSpec((None, K, tn), lambda j, i: (layer, 0, j)),
           pl.BlockSpec((gb, rb, tn), xmap),
           pl.BlockSpec((gb, 1, tn), lambda j, i: (i // nrb, 0, i_gate * ntn + j))],
        out_specs=pl.BlockSpec((gb, rb, tn), xmap),
        out_shape=jax.ShapeDtypeStruct((G, R, N), F32),
        scratch_shapes=[pltpu.VMEM((K, tn), BF16)] if cast_w else [],
        compiler_params=_cparams("arbitrary", "arbitrary"),
        name=name,
    )(*[a.reshape(M, a.shape[-1]) for a in a_list], w, x, mod)


def _mlstm_body(zq_ref, zk_ref, zv_ref, zo_ref, zg_ref, cw_ref, gbias_ref, gain_ref, conv0_ref,
                C0_ref, n0_ref, m0_ref, hm_ref, C_ref, n_ref, m_ref, ext, *, L, gs):
    W = M_WIDTH
    Dh = M_HEAD_DIM
    H = M_HEADS
    keep = M_CONV - 1
    c = pl.program_id(1)

    @pl.when(c == 0)
    def _():
        C_ref[...] = C0_ref[...]
        n_ref[...] = n0_ref[...]
        m_ref[...] = m0_ref[...]
        ext[:, SUBLANES - keep:SUBLANES, :] = conv0_ref[...]

    ext[:, SUBLANES:SUBLANES + L, 0:W] = zq_ref[...]
    ext[:, SUBLANES:SUBLANES + L, W:2 * W] = zk_ref[...]
    row = lax.broadcasted_iota(jnp.int32, (L, L), 0)
    col = lax.broadcasted_iota(jnp.int32, (L, L), 1)
    eye = row == col
    tri = col <= row
    chains = [(g, h) for g in range(gs) for h in range(H)]
    ids = range(len(chains))
    qk, gates = [], []
    for g in range(gs):
        conv = ext[g, SUBLANES:SUBLANES + L, :] * cw_ref[keep:keep + 1, :]
        for j in range(keep):
            conv = conv + ext[g, SUBLANES - keep + j:SUBLANES - keep + j + L, :] * cw_ref[j:j + 1, :]
        qk.append(_silu(conv))
        gates.append(zg_ref[g] + gbias_ref[...])
    ext[:, SUBLANES - keep:SUBLANES, :] = ext[:, SUBLANES + L - keep:SUBLANES + L, :]

    sl = [slice(h * Dh, (h + 1) * Dh) for _, h in chains]
    q = [qk[g][:, h * Dh:(h + 1) * Dh] for g, h in chains]
    k = [qk[g][:, W + h * Dh:W + (h + 1) * Dh] * (Dh ** -0.5) for g, h in chains]
    v = [zv_ref[g, :, sl[i]] for i, (g, h) in enumerate(chains)]
    ig_col = [gates[g][:, h:h + 1] for g, h in chains]
    fpre = [gates[g][:, H + h:H + h + 1] for g, h in chains]
    lf_col = [jnp.minimum(x, 0.0) - jnp.log1p(jnp.exp(-jnp.abs(x))) for x in fpre]
    lf_row = [jnp.sum(jnp.where(eye, x, 0.0), axis=0, keepdims=True) for x in lf_col]
    ig_row = [jnp.sum(jnp.where(eye, x, 0.0), axis=0, keepdims=True) for x in ig_col]
    b_col = [jnp.sum(jnp.where(tri, x, 0.0), axis=1, keepdims=True) for x in lf_row]
    b_row = [jnp.sum(jnp.where(row <= col, x, 0.0), axis=0, keepdims=True) for x in lf_col]
    dmat = [jnp.where(tri, b_col[i] - b_row[i] + ig_row[i], NEG_BIG) for i in ids]
    m_prev = [m_ref[g, :, h:h + 1] for g, h in chains]
    inter = [b_col[i] + m_prev[i] for i in ids]
    m_t = [jnp.maximum(inter[i], jnp.max(dmat[i], axis=1, keepdims=True)) for i in ids]
    w_inter = [jnp.exp(inter[i] - m_t[i]) for i in ids]
    Cm = [C_ref[g, h] for g, h in chains]
    n_row = [n_ref[g, h:h + 1, :] for g, h in chains]
    s = [_dot_nt(q[i], k[i]) * jnp.exp(dmat[i] - m_t[i]) for i in ids]
    qc = [_dot(q[i], Cm[i]) for i in ids]
    num = [_dot(s[i], v[i]) + w_inter[i] * qc[i] for i in ids]
    den = [jnp.sum(s[i], axis=1, keepdims=True) + w_inter[i] * jnp.sum(q[i] * n_row[i], axis=1, keepdims=True)
           for i in ids]
    hh = [num[i] / jnp.maximum(jnp.abs(den[i]), jnp.exp(-m_t[i])) for i in ids]
    m_new = [x[L - 1:L, :] for x in m_t]
    b_last = [x[L - 1:L, :] for x in b_col]
    wk = [jnp.exp(b_last[i] - b_col[i] + ig_col[i] - m_new[i]) * k[i] for i in ids]
    decay = [jnp.exp(b_last[i] + m_prev[i] - m_new[i]) for i in ids]
    upd = [_dot_tn(wk[i], v[i]) for i in ids]
    for i, (g, h) in enumerate(chains):
        C_ref[g, h] = decay[i] * Cm[i] + upd[i]
        n_ref[g, h:h + 1, :] = decay[i] * n_row[i] + jnp.sum(wk[i], axis=0, keepdims=True)
        m_ref[g, :, h:h + 1] = m_new[i]
        dlt = hh[i] - jnp.mean(hh[i], axis=-1, keepdims=True)
        ln = dlt * lax.rsqrt(jnp.mean(dlt * dlt, axis=-1, keepdims=True) + RMS_EPS)
        hm_ref[g, :, sl[i]] = _sigmoid(zo_ref[g, :, sl[i]]) * ln * gain_ref[:, sl[i]]


def _mlstm(zm, zgate, conv_w, gate_b, m_gain, conv0, C0, n0, m0):
    G, R, _ = zm.shape
    L = math.gcd(R, CHUNK)
    W = M_WIDTH
    gs = 2 if R > SUBLANES else 4
    gbias = jnp.zeros((1, LANES), F32).at[0, :2 * M_HEADS].set(gate_b.astype(F32))
    zspec = lambda blk: pl.BlockSpec((gs, L, W), lambda g, c: (g, c, blk))
    st4 = lambda g, c: (g, 0, 0, 0)
    st3 = lambda g, c: (g, 0, 0)
    return pl.pallas_call(
        functools.partial(_mlstm_body, L=L, gs=gs),
        grid=(G // gs, R // L),
        in_specs=[zspec(0), zspec(1), zspec(2), zspec(3),
                  pl.BlockSpec((gs, L, LANES), lambda g, c: (g, c, 0)),
                  pl.BlockSpec((M_CONV, 2 * W), lambda g, c: (0, 0)),
                  pl.BlockSpec((1, LANES), lambda g, c: (0, 0)),
                  pl.BlockSpec((1, W), lambda g, c: (0, 0)),
                  pl.BlockSpec((gs, M_CONV - 1, 2 * W), st3),
                  pl.BlockSpec((gs, M_HEADS, M_HEAD_DIM, M_HEAD_DIM), st4),
                  pl.BlockSpec((gs, M_HEADS, M_HEAD_DIM), st3),
                  pl.BlockSpec((gs, 1, M_HEADS), st3)],
        out_specs=[pl.BlockSpec((gs, L, W), lambda g, c: (g, c, 0)),
                   pl.BlockSpec((gs, M_HEADS, M_HEAD_DIM, M_HEAD_DIM), st4),
                   pl.BlockSpec((gs, M_HEADS, M_HEAD_DIM), st3),
                   pl.BlockSpec((gs, 1, M_HEADS), st3)],
        out_shape=[jax.ShapeDtypeStruct((G, R, W), F32),
                   jax.ShapeDtypeStruct((G, M_HEADS, M_HEAD_DIM, M_HEAD_DIM), F32),
                   jax.ShapeDtypeStruct((G, M_HEADS, M_HEAD_DIM), F32),
                   jax.ShapeDtypeStruct((G, 1, M_HEADS), F32)],
        scratch_shapes=[pltpu.VMEM((gs, SUBLANES + L, 2 * W), F32)],
        compiler_params=_cparams("arbitrary", "arbitrary"),
        name="mlstm",
    )(zm, zm, zm, zm, zgate, conv_w, gbias, m_gain.reshape(1, W), conv0, C0, n0, m0.reshape(G, 1, M_HEADS))


def _hgrn_body(zq_ref, zf_ref, zi_ref, zgg_ref, lb_ref, gain_ref, S0_ref, og_ref, S_ref, ST, *, nblk, gs):
    Dh = G_HEAD_DIM
    B = SUBLANES
    c = pl.program_id(1)
    chains = [(g, h) for g in range(gs) for h in range(G_HEADS)]

    @pl.when(c == 0)
    def _():
        for i, (g, h) in enumerate(chains):
            ST[i] = S0_ref[g, h].T

    rowi = lax.broadcasted_iota(jnp.int32, (B, Dh), 0)

    def blk(bi, carry):
        r0 = pl.multiple_of(bi * B, B)
        rows = pl.ds(r0, B)
        ids = range(len(chains))
        sl = [slice(h * Dh, (h + 1) * Dh) for _, h in chains]
        gi = [g for g, _ in chains]
        f = [lb_ref[:, sl[i]] + (1.0 - lb_ref[:, sl[i]]) * _sigmoid(zf_ref[gi[i], rows, sl[i]]) for i in ids]
        kk = [1.0 - x for x in f]
        bc = [_cumsum_rows(jnp.log(x)) for x in f]
        btot = [x[B - 1:B, :] for x in bc]
        qh = [_silu(zq_ref[gi[i], rows, sl[i]]) * (Dh ** -0.5) for i in ids]
        v = [zi_ref[gi[i], rows, sl[i]] for i in ids]
        st = [ST[i] for i in ids]
        o = [_dot_nt(qh[i] * jnp.exp(bc[i]), st[i]) for i in ids]
        upd = [_dot_tn(v[i], kk[i] * jnp.exp(btot[i] - bc[i])) for i in ids]
        for s in range(B):
            p = [jnp.exp(jnp.where(rowi >= s, bc[i] - bc[i][s:s + 1, :], NEG_BIG)) * qh[i] * kk[i][s:s + 1, :]
                 for i in ids]
            o = [o[i] + jnp.sum(p[i], axis=-1, keepdims=True) * v[i][s:s + 1, :] for i in ids]
        for i in ids:
            ST[i] = st[i] * jnp.exp(btot[i]) + upd[i]
            rms = o[i] * lax.rsqrt(jnp.mean(o[i] * o[i], axis=-1, keepdims=True) + RMS_EPS)
            og_ref[gi[i], rows, sl[i]] = rms * gain_ref[:, sl[i]] * _silu(zgg_ref[gi[i], rows, sl[i]])
        return carry

    lax.fori_loop(0, nblk, blk, 0)

    @pl.when(c == pl.num_programs(1) - 1)
    def _():
        for i, (g, h) in enumerate(chains):
            S_ref[g, h] = ST[i].T


def _hgrn(zg, lb, g_gain, S0):
    G, R, _ = zg.shape
    W = G_WIDTH
    gs = 4
    Rc = min(R, 128)
    zspec = lambda blk: pl.BlockSpec((gs, Rc, W), lambda g, c: (g, c, blk))
    st4 = lambda g, c: (g, 0, 0, 0)
    return pl.pallas_call(
        functools.partial(_hgrn_body, nblk=Rc // SUBLANES, gs=gs),
        grid=(G // gs, R // Rc),
        in_specs=[zspec(0), zspec(1), zspec(2), zspec(3),
                  pl.BlockSpec((1, W), lambda g, c: (0, 0)),
                  pl.BlockSpec((1, W), lambda g, c: (0, 0)),
                  pl.BlockSpec((gs, G_HEADS, G_HEAD_DIM, G_HEAD_DIM), st4)],
        out_specs=[pl.BlockSpec((gs, Rc, W), lambda g, c: (g, c, 0)),
                   pl.BlockSpec((gs, G_HEADS, G_HEAD_DIM, G_HEAD_DIM), st4)],
        out_shape=[jax.ShapeDtypeStruct((G, R, W), F32),
                   jax.ShapeDtypeStruct((G, G_HEADS, G_HEAD_DIM, G_HEAD_DIM), F32)],
        scratch_shapes=[pltpu.VMEM((gs * G_HEADS, G_HEAD_DIM, G_HEAD_DIM), F32)],
        compiler_params=_cparams("arbitrary", "arbitrary"),
        name="hgrn2",
    )(zg, zg, zg, zg, lb.reshape(1, W), g_gain.reshape(1, W), S0)


def _rprep_body(x_ref, g_ref, sc_ref, sh_ref, shift0_ref, mu_ref, *refs, gb, rb):
    outs = refs[:6]
    hlast_ref, hbuf = refs[6:]
    B = SUBLANES
    n, tok, seq = _strips(gb, rb)
    S = hbuf.shape[1] - B
    whole_sequences = gb > 1

    if not whole_sequences:
        @pl.when(pl.program_id(1) == 0)
        def _():
            hbuf[:, B - 1:B, :] = shift0_ref[...]

    def strip(i, carry):
        if whole_sequences:
            hbuf[:, B - 1:B, :] = shift0_ref[seq(i)]
        h = _rms(x_ref[tok(i)], g_ref[...]) * (1.0 + sc_ref[seq(i)]) + sh_ref[seq(i)]
        hbuf[:, B:B + S, :] = h
        xx = hbuf[:, B - 1:B - 1 + S, :] - h
        last = h[:, S - 1:S, :]
        if whole_sequences:
            hlast_ref[seq(i)] = last
        else:
            hbuf[:, B - 1:B, :] = last
        for m in range(6):
            outs[m][tok(i)] = (h + xx * mu_ref[m:m + 1, :]).astype(BF16)
        return carry

    lax.fori_loop(0, n, strip, 0, unroll=4)
    if not whole_sequences:
        hlast_ref[...] = hbuf[:, B - 1:B, :]


def _rwkv_prep(x, g, mod, i_scale, i_shift, shift0, mu, tm):
    G, R, D = x.shape
    gb, rb, nrb = _row_blocking(G, R, tm)
    tok = pl.BlockSpec((gb, rb, D), lambda a, b: (a, b, 0))
    one = pl.BlockSpec((gb, 1, D), lambda a, b: (a, 0, 0))
    strip_shape = (1, SUBLANES + STRIP_ROWS, D) if gb == 1 else (STRIP_ROWS // rb, SUBLANES + rb, D)
    res = pl.pallas_call(
        functools.partial(_rprep_body, gb=gb, rb=rb),
        grid=(G // gb, nrb),
        in_specs=[tok,
                  pl.BlockSpec((1, 1, D), lambda a, b: (0, 0, 0)),
                  pl.BlockSpec((gb, 1, D), lambda a, b: (a, 0, i_scale)),
                  pl.BlockSpec((gb, 1, D), lambda a, b: (a, 0, i_shift)),
                  one,
                  pl.BlockSpec((6, D), lambda a, b: (0, 0))],
        out_specs=[tok] * 6 + [one],
        out_shape=[jax.ShapeDtypeStruct((G, R, D), BF16)] * 6 + [jax.ShapeDtypeStruct((G, 1, D), F32)],
        scratch_shapes=[pltpu.VMEM(strip_shape, F32)],
        compiler_params=_cparams("arbitrary", "arbitrary"),
        name="rwkv_prep",
    )(x, g.reshape(1, 1, D), mod, mod, shift0.reshape(G, 1, D), mu)
    return res[:6], res[6]


def _lora_body(x_ref, w1_ref, w2_ref, o_ref, *, act):
    t = _dot(x_ref[...], w1_ref[...])
    if act == "tanh":
        t = jnp.tanh(t)
    elif act == "sigmoid":
        t = _sigmoid(t)
    o_ref[...] = _dot(t, w2_ref[...])


def _lora(x, w1, w2, act, tm):
    G, R, K = x.shape
    M = G * R
    r = w1.shape[1]
    rp = -(-r // LANES) * LANES
    w1p = jnp.zeros((K, rp), F32).at[:, :r].set(w1)
    w2p = jnp.zeros((rp, w2.shape[1]), F32).at[:r, :].set(w2)
    N = w2.shape[1]
    out = pl.pallas_call(
        functools.partial(_lora_body, act=act),
        grid=(M // tm,),
        in_specs=[pl.BlockSpec((tm, K), lambda i: (i, 0)),
                  pl.BlockSpec((K, rp), lambda i: (0, 0)),
                  pl.BlockSpec((rp, N), lambda i: (0, 0))],
        out_specs=pl.BlockSpec((tm, N), lambda i: (i, 0)),
        out_shape=jax.ShapeDtypeStruct((M, N), F32),
        compiler_params=_cparams("arbitrary"),
        name="lora_" + str(act),
    )(x.reshape(M, K), w1p, w2p)
    return out.reshape(G, R, N)


def _head_mask(rows, cols, rper, cper):
    r = lax.broadcasted_iota(jnp.int32, (rows, cols), 0)
    c = lax.broadcasted_iota(jnp.int32, (rows, cols), 1)
    return (r // rper) == (c // cper)


def _bd(y, mask01):
    if y.shape[0] % (2 * SUBLANES) == 0:
        return jnp.concatenate([y.astype(BF16)] * R_GROUP, axis=0) * mask01
    return jnp.concatenate([y] * R_GROUP, axis=0).astype(BF16) * mask01


def _segsums(xs, ones_bd):
    c = xs[0].shape[0]
    parts = []
    for x in xs:
        hi = x.astype(BF16).astype(F32)
        parts += [hi, x - hi]
    res = jnp.dot(jnp.concatenate(parts, axis=0).astype(BF16), ones_bd, preferred_element_type=F32)
    return [res[2 * i * c:(2 * i + 1) * c] + res[(2 * i + 1) * c:(2 * i + 2) * c] for i in range(len(xs))]


def _rwkv_body(r_ref, k_ref, v_ref, wl_ref, al_ref, g_ref, w0_ref, a0_ref, kkp_ref, kap_ref, rk_ref,
               lnw_ref, lnb_ref, S0_ref, y_ref, S_ref, Sbd, *, c, nchunk, ng):
    N = R_HEAD_DIM
    GW = R_GROUP_W
    j = pl.program_id(2)
    m_state = _head_mask(GW, GW, N, N)

    @pl.when(j == 0)
    def _():
        for gi in range(ng):
            s0 = S0_ref[gi * R_GROUP:(gi + 1) * R_GROUP].reshape(GW, N)
            Sbd[gi] = jnp.where(m_state, jnp.concatenate([s0] * R_GROUP, axis=1), 0.0)

    ones_bd = jnp.where(m_state, 1.0, 0.0).astype(BF16)
    m_vec = jnp.where(_head_mask(R_GROUP * c, GW, c, N), 1.0, 0.0).astype(BF16)
    m_mat = jnp.where(_head_mask(R_GROUP * c, R_GROUP * c, c, c), 1.0, 0.0).astype(BF16)
    t_idx = lax.broadcasted_iota(jnp.int32, (2 * c, R_GROUP * c), 0)
    s_idx = lax.broadcasted_iota(jnp.int32, (2 * c, R_GROUP * c), 1) % c
    causal = jnp.where(t_idx < c, jnp.where(s_idx < t_idx, 1.0, 0.0), jnp.where(s_idx <= t_idx - c, 1.0, 0.0))

    def chunk(ci, carry):
        r0 = pl.multiple_of(ci * c, c)
        rows = pl.ds(r0, c)
        gs = [slice(gi * GW, (gi + 1) * GW) for gi in range(ng)]
        grp = range(ng)
        nt = (((1,), (1,)), ((), ()))
        mm = lambda a, b: jnp.dot(a.astype(BF16), b, preferred_element_type=F32)
        r = [r_ref[rows, s] for s in gs]
        k = [k_ref[rows, s] for s in gs]
        v = [v_ref[rows, s] for s in gs]
        lw = [-jnp.exp(-_softplus(-(w0_ref[:, s] + wl_ref[rows, s])) - 0.5) for s in gs]
        a = [_sigmoid(a0_ref[:, s] + al_ref[rows, s]) for s in gs]
        kk = [k[i] * kkp_ref[:, gs[i]] for i in grp]
        k2 = [k[i] * (1.0 + (a[i] - 1.0) * kap_ref[:, gs[i]]) for i in grp]
        sums = [_segsums([kk[i] * kk[i], r[i] * k2[i] * rk_ref[:, gs[i]]], ones_bd) for i in grp]
        kk = [kk[i] / jnp.maximum(jnp.sqrt(sums[i][0]), 1e-12) for i in grp]
        bb = [kk[i] * a[i] for i in grp]
        cw = [_cumsum_rows(x) for x in lw]
        cwl = [x[c - 1:c, :] for x in cw]
        e_out = [jnp.exp(-x) for x in cw]
        e_end = [jnp.exp(cwl[i] - cw[i]) for i in grp]
        ar = [jnp.concatenate([-kk[i] * jnp.exp(cw[i] - lw[i]), r[i] * jnp.exp(cw[i])], axis=0).astype(BF16)
              for i in grp]
        bdb = [_bd(bb[i] * e_out[i], m_vec) for i in grp]
        bdk = [_bd(k2[i] * e_out[i], m_vec) for i in grp]
        bdv = [_bd(v[i], m_vec) for i in grp]
        sbd = [Sbd[i] for i in grp]
        pb = [lax.dot_general(ar[i], bdb[i], nt, preferred_element_type=F32) * causal for i in grp]
        pk = [lax.dot_general(ar[i], bdk[i], nt, preferred_element_type=F32) * causal for i in grp]
        s0p = [lax.dot_general(ar[i], sbd[i].astype(BF16), nt, preferred_element_type=F32) for i in grp]
        u = [s0p[i][:c] + mm(pk[i][:c], bdv[i]) for i in grp]
        p = [pb[i][:c] for i in grp]
        step = 1
        while step < c:
            bdu = [_bd(u[i], m_vec) for i in grp]
            u = [u[i] + mm(p[i], bdu[i]) for i in grp]
            step *= 2
            if step < c:
                bdp = [_bd(p[i], m_mat) for i in grp]
                p = [mm(p[i], bdp[i]) for i in grp]
        bdu = [_bd(u[i], m_vec) for i in grp]
        y = [s0p[i][c:] + mm(pb[i][c:], bdu[i]) + mm(pk[i][c:], bdv[i]) for i in grp]
        upd = [_dot_tn(jnp.concatenate([u[i], v[i]], axis=0),
                       jnp.concatenate([bb[i] * e_end[i], k2[i] * e_end[i]], axis=0)) for i in grp]
        for i in grp:
            Sbd[i] = sbd[i] * jnp.exp(cwl[i]) + jnp.where(m_state, upd[i], 0.0)
        ysum = [_segsums([y[i]], ones_bd)[0] for i in grp]
        dlt = [y[i] - ysum[i] * (1.0 / N) for i in grp]
        vsum = [_segsums([dlt[i] * dlt[i]], ones_bd)[0] for i in grp]
        for i in grp:
            yn = dlt[i] * lax.rsqrt(vsum[i] * (1.0 / N) + LN_X_EPS) * lnw_ref[:, gs[i]] + lnb_ref[:, gs[i]]
            y_ref[rows, gs[i]] = ((yn + sums[i][1] * v[i]) * g_ref[rows, gs[i]]).astype(y_ref.dtype)
        return carry

    lax.fori_loop(0, nchunk, chunk, 0)

    @pl.when(j == pl.num_programs(2) - 1)
    def _():
        for gi in range(ng):
            sbd = Sbd[gi]
            for h in range(R_GROUP):
                S_ref[gi * R_GROUP + h] = sbd[h * N:(h + 1) * N, h * N:(h + 1) * N]


def _rwkv(r, k, v, wl, al, g, w0, a0, kkp, kap, rk, lnw, lnb, S0):
    G, R, D = r.shape
    c = math.gcd(R, CHUNK)
    Rc = min(R, 256)
    GW = R_GROUP_W
    ng = 4 if R > SUBLANES else D // GW
    bw = ng * GW
    tok = pl.BlockSpec((None, Rc, bw), lambda a, b, j: (a, j, b))
    par = pl.BlockSpec((1, bw), lambda a, b, j: (0, b))
    st = pl.BlockSpec((None, ng * R_GROUP, R_HEAD_DIM, R_HEAD_DIM), lambda a, b, j: (a, b, 0, 0))
    row = lambda p: p.reshape(1, D).astype(F32)
    return pl.pallas_call(
        functools.partial(_rwkv_body, c=c, nchunk=Rc // c, ng=ng),
        grid=(G, D // bw, R // Rc),
        in_specs=[tok] * 6 + [par] * 7 + [st],
        out_specs=[tok, st],
        out_shape=[jax.ShapeDtypeStruct((G, R, D), BF16),
                   jax.ShapeDtypeStruct((G, R_HEADS, R_HEAD_DIM, R_HEAD_DIM), F32)],
        scratch_shapes=[pltpu.VMEM((ng, GW, GW), F32)],
        compiler_params=_cparams("arbitrary", "arbitrary", "arbitrary"),
        name="rwkv7",
    )(r, k, v, wl, al, g, row(w0), row(a0), row(kkp), row(kap), row(rk), row(lnw), row(lnb), S0)


def _trunk(x, mod, m_C, m_n, m_m, m_conv, g_S, r_S, r_shift, p, lbs, tm):
    G, R, D = x.shape
    md = mod[0]
    h = _modnorm(x, p["norm_mix"][0], md, 1, 0, tm)
    w_in_t = jnp.swapaxes(p["ab_w_in"], 1, 2)
    zm = _mm(h, w_in_t, 0, n_cols=4 * M_WIDTH, tm=tm, tn=1024, name="mm_in_m", w_transposed=True)
    zgate = _mm(h, w_in_t, 0, n_cols=LANES, tm=tm, tn=LANES, name="mm_in_gate", w_transposed=True, col0=4 * M_WIDTH)
    zg = _mm(h, w_in_t, 0, n_cols=4 * G_WIDTH, tm=tm, tn=1024, name="mm_in_g", w_transposed=True,
             col0=4 * M_WIDTH + 2 * M_HEADS)
    hm, C, n, m = _mlstm(zm, zgate, p["m_conv_w"][0], p["ab_gate_b"][0], p["m_norm"][0],
                         m_conv[0], m_C[0], m_n[0], m_m[0])
    conv_new = zm[:, R - (M_CONV - 1):, :2 * M_WIDTH]
    og, S = _hgrn(zg, lbs[0], p["g_norm"][0], g_S[0])
    x = _mm_residual([hm, og], p["ab_w_out"], 0, x, md, 2, tm=tm, tn=512, name="mm_out0")
    h = _modnorm(x, p["norm_ffn"][0], md, 4, 3, tm)
    act = _mm(h, p["ffn_w1"], 0, n_cols=4 * D, tm=tm, tn=1024, act="relu2", out_dtype=BF16, name="ffn_up")
    x = _mm_residual([act], p["ffn_w2_bf16"], 0, x, md, 5, tm=256, tn=1024, name="ffn_down")
    md = mod[1]
    (xr, xw, xk, xv, xa, xg), shift_new = _rwkv_prep(x, p["norm_mix"][1], md, 1, 0, r_shift[0], p["r_mu"][0],
                                                     min(tm, 256))
    r = _mm(xr, p["r_wr"], 0, n_cols=D, tm=tm, tn=1024, name="mm_r")
    k = _mm(xk, p["r_wk"], 0, n_cols=D, tm=tm, tn=1024, name="mm_k")
    v = _mm(xv, p["r_wv"], 0, n_cols=D, tm=tm, tn=1024, name="mm_v")
    wl = _lora(xw, p["r_w1"][0], p["r_w2"][0], "tanh", min(tm, 512))
    al = _lora(xa, p["r_a1"][0], p["r_a2"][0], None, min(tm, 512))
    gg = _lora(xg, p["r_g1"][0], p["r_g2"][0], "sigmoid", min(tm, 512))
    yg, rS = _rwkv(r, k, v, wl, al, gg, p["r_w0"][0], p["r_a0"][0], p["r_kk"][0], p["r_ka"][0],
                   p["r_rk"][0].reshape(-1), p["r_lnw"][0], p["r_lnb"][0], r_S[0])
    x = _mm_residual([yg], p["r_wo"], 0, x, md, 2, tm=tm, tn=1024, name="mm_out1")
    h = _modnorm(x, p["norm_ffn"][1], md, 4, 3, tm)
    act = _mm(h, p["ffn_w1"], 1, n_cols=4 * D, tm=tm, tn=1024, act="relu2", out_dtype=BF16, name="ffn_up")
    x = _mm_residual([act], p["ffn_w2_bf16"], 1, x, md, 5, tm=256, tn=1024, name="ffn_down")
    y = _rmsnorm(x, p["final_norm"], min(tm, 512))
    return y, (C[None], n[None], m.reshape(1, G, M_HEADS), conv_new[None], S[None], rS[None],
               shift_new.reshape(1, G, D))


def kernel(x_prompt, x_sample, c_prompt, c_sample, state_mlstm_C, state_mlstm_n, state_mlstm_m, state_mlstm_conv, state_hgrn_S, state_rwkv_S, state_rwkv_shift, mod_w, mod_b, norm_mix, norm_ffn, ffn_w1, ffn_w2, final_norm, ab_w_in, ab_gate_b, m_conv_w, m_norm, g_lb, g_norm, ab_w_out, r_mu, r_w0, r_w1, r_w2, r_a0, r_a1, r_a2, r_g1, r_g2, r_kk, r_ka, r_rk, r_wr, r_wk, r_wv, r_wo, r_lnw, r_lnb):
    p = dict(norm_mix=norm_mix, norm_ffn=norm_ffn, ffn_w1=ffn_w1, ffn_w2=ffn_w2, final_norm=final_norm,
             ab_w_in=ab_w_in, ab_gate_b=ab_gate_b, m_conv_w=m_conv_w, m_norm=m_norm, g_norm=g_norm,
             ab_w_out=ab_w_out, r_mu=r_mu, r_w0=r_w0, r_w1=r_w1, r_w2=r_w2, r_a0=r_a0, r_a1=r_a1, r_a2=r_a2,
             r_g1=r_g1, r_g2=r_g2, r_kk=r_kk, r_ka=r_ka, r_rk=r_rk, r_wr=r_wr, r_wk=r_wk, r_wv=r_wv,
             r_wo=r_wo, r_lnw=r_lnw, r_lnb=r_lnb)
    p["ffn_w2_bf16"] = ffn_w2.astype(BF16)
    B, T, D = x_prompt.shape
    Bs = x_sample.shape[0]
    depth = mod_w.shape[0]
    pad = (-(Bs + B)) % SUBLANES
    c_all = jnp.concatenate([c_sample, c_prompt, jnp.zeros((pad, D), F32)], axis=0)
    mod = _modulation(c_all, mod_w, mod_b)
    mod_s = mod[:, :Bs].reshape(depth, Bs, 1, N_MOD * D)
    mod_p = mod[:, Bs:Bs + B].reshape(depth, B, 1, N_MOD * D)
    lbs = jnp.cumsum(jax.nn.softmax(g_lb.astype(F32), axis=0), axis=0)

    n_even = state_mlstm_C.shape[0]
    n_odd = state_rwkv_S.shape[0]
    z = lambda *s: jnp.zeros(s, F32)
    yp, sp = _trunk(x_prompt, mod_p,
                    z(n_even, B, M_HEADS, M_HEAD_DIM, M_HEAD_DIM), z(n_even, B, M_HEADS, M_HEAD_DIM),
                    z(n_even, B, M_HEADS), z(n_even, B, M_CONV - 1, 2 * M_WIDTH),
                    z(n_even, B, G_HEADS, G_HEAD_DIM, G_HEAD_DIM), z(n_odd, B, R_HEADS, R_HEAD_DIM, R_HEAD_DIM),
                    z(n_odd, B, D), p, lbs, 1024)
    ys, ss = _trunk(x_sample, mod_s, state_mlstm_C, state_mlstm_n, state_mlstm_m, state_mlstm_conv,
                    state_hgrn_S, state_rwkv_S, state_rwkv_shift, p, lbs, 1024)
    return (yp, ys) + tuple(sp) + tuple(ss)
```

```python
import functools
import math

import jax
import jax.numpy as jnp
from jax import lax
from jax.experimental import pallas as pl
from jax.experimental.pallas import tpu as pltpu

F32 = jnp.float32
BF16 = jnp.bfloat16

D_MODEL = 2048
M_HEADS = 4
M_HEAD_DIM = 256
M_WIDTH = M_HEADS * M_HEAD_DIM
M_CONV = 4
G_HEADS = 8
G_HEAD_DIM = 128
G_WIDTH = G_HEADS * G_HEAD_DIM
R_HEAD_DIM = 64
R_HEADS = D_MODEL // R_HEAD_DIM
R_GROUP = 4
R_GROUP_W = R_GROUP * R_HEAD_DIM
N_MOD = 6
RMS_EPS = 1e-6
LN_X_EPS = 64e-5
CHUNK = 64
NEG_BIG = -1e30

V7X_VMEM_LIMIT_BYTES = 56 * 1024 * 1024
SUBLANES = 8
LANES = 128


def _cparams(*sem):
    return pltpu.CompilerParams(dimension_semantics=sem, vmem_limit_bytes=V7X_VMEM_LIMIT_BYTES)


def _sigmoid(x):
    return 1.0 / (1.0 + jnp.exp(-x))


def _silu(x):
    return x * _sigmoid(x)


def _softplus(x):
    return jnp.maximum(x, 0.0) + jnp.log1p(jnp.exp(-jnp.abs(x)))


def _dot(a, b):
    return jnp.dot(a.astype(BF16), b.astype(BF16), preferred_element_type=F32)


def _dot_nt(a, b):
    return lax.dot_general(a.astype(BF16), b.astype(BF16), (((1,), (1,)), ((), ())), preferred_element_type=F32)


def _dot_tn(a, b):
    return lax.dot_general(a.astype(BF16), b.astype(BF16), (((0,), (0,)), ((), ())), preferred_element_type=F32)


def _cumsum_rows(x):
    n = x.shape[0]
    row = lax.broadcasted_iota(jnp.int32, x.shape, 0)
    s = 1
    while s < n:
        x = x + jnp.where(row >= s, pltpu.roll(x, s, axis=0), 0.0)
        s *= 2
    return x


def _row_blocking(G, R, tm):
    if R >= tm:
        assert R % tm == 0
        return 1, tm, R // tm
    assert tm % R == 0 and G % (tm // R) == 0
    return tm // R, R, 1


def _mod_body(c_ref, w_ref, b_ref, o_ref):
    sc = _silu(c_ref[...])
    o_ref[...] = _dot(sc, w_ref[...]) + b_ref[...]


def _modulation(c_all, mod_w, mod_b):
    L, K, N = mod_w.shape
    Mc = c_all.shape[0]
    tn = 1024
    return pl.pallas_call(
        _mod_body,
        grid=(L, N // tn),
        in_specs=[pl.BlockSpec((Mc, K), lambda l, j: (0, 0)),
                  pl.BlockSpec((None, K, tn), lambda l, j: (l, 0, j)),
                  pl.BlockSpec((None, 1, tn), lambda l, j: (l, 0, j))],
        out_specs=pl.BlockSpec((None, Mc, tn), lambda l, j: (l, 0, j)),
        out_shape=jax.ShapeDtypeStruct((L, Mc, N), F32),
        compiler_params=_cparams("arbitrary", "arbitrary"),
        name="modulation",
    )(c_all, mod_w, mod_b.reshape(L, 1, N))


def _rms(x, g):
    ms = jnp.mean(x * x, axis=-1, keepdims=True)
    return x * lax.rsqrt(ms + RMS_EPS) * g


STRIP_ROWS = 2 * SUBLANES


def _strips(gb, rb):
    if gb == 1:
        n = rb // STRIP_ROWS
        tok = lambda i: (slice(None), pl.ds(pl.multiple_of(i * STRIP_ROWS, STRIP_ROWS), STRIP_ROWS), slice(None))
        seq = lambda i: (slice(None), slice(None), slice(None))
    else:
        assert STRIP_ROWS % rb == 0
        per = STRIP_ROWS // rb
        n = gb // per
        tok = seq = lambda i: (pl.ds(i * per, per), slice(None), slice(None))
    return n, tok, seq


def _modnorm_body(x_ref, g_ref, sc_ref, sh_ref, o_ref, *, gb, rb):
    n, tok, seq = _strips(gb, rb)

    def strip(i, carry):
        y = _rms(x_ref[tok(i)], g_ref[...])
        o_ref[tok(i)] = (y * (1.0 + sc_ref[seq(i)]) + sh_ref[seq(i)]).astype(o_ref.dtype)
        return carry

    lax.fori_loop(0, n, strip, 0, unroll=4)


def _modnorm(x, g, mod, i_scale, i_shift, tm):
    G, R, D = x.shape
    gb, rb, nrb = _row_blocking(G, R, tm)
    return pl.pallas_call(
        functools.partial(_modnorm_body, gb=gb, rb=rb),
        grid=(G // gb, nrb),
        in_specs=[pl.BlockSpec((gb, rb, D), lambda a, b: (a, b, 0)),
                  pl.BlockSpec((1, 1, D), lambda a, b: (0, 0, 0)),
                  pl.BlockSpec((gb, 1, D), lambda a, b: (a, 0, i_scale)),
                  pl.BlockSpec((gb, 1, D), lambda a, b: (a, 0, i_shift))],
        out_specs=pl.BlockSpec((gb, rb, D), lambda a, b: (a, b, 0)),
        out_shape=jax.ShapeDtypeStruct((G, R, D), BF16),
        compiler_params=_cparams("arbitrary", "arbitrary"),
        name="modnorm",
    )(x, g.reshape(1, 1, D), mod, mod)


def _rmsnorm_body(x_ref, g_ref, o_ref, *, gb, rb):
    n, tok, _ = _strips(gb, rb)

    def strip(i, carry):
        o_ref[tok(i)] = _rms(x_ref[tok(i)], g_ref[...])
        return carry

    lax.fori_loop(0, n, strip, 0, unroll=4)


def _rmsnorm(x, g, tm):
    G, R, D = x.shape
    gb, rb, nrb = _row_blocking(G, R, tm)
    return pl.pallas_call(
        functools.partial(_rmsnorm_body, gb=gb, rb=rb),
        grid=(G // gb, nrb),
        in_specs=[pl.BlockSpec((gb, rb, D), lambda a, b: (a, b, 0)),
                  pl.BlockSpec((1, 1, D), lambda a, b: (0, 0, 0))],
        out_specs=pl.BlockSpec((gb, rb, D), lambda a, b: (a, b, 0)),
        out_shape=jax.ShapeDtypeStruct((G, R, D), F32),
        compiler_params=_cparams("arbitrary", "arbitrary"),
        name="final_norm",
    )(x, g.reshape(1, 1, D))


def _mm_body(a_ref, w_ref, o_ref, wbf, *, act, w_transposed):
    @pl.when(pl.program_id(1) == 0)
    def _():
        wbf[...] = (w_ref[0].T if w_transposed else w_ref[...]).astype(BF16)

    acc = jnp.dot(a_ref[...].astype(BF16), wbf[...], preferred_element_type=F32)
    if act == "relu2":
        acc = jnp.square(jnp.maximum(acc, 0.0))
    o_ref[...] = acc.astype(o_ref.dtype)


def _mm(a, w, layer, *, n_cols, tm, tn, act=None, out_dtype=F32, name="mm", w_transposed=False, col0=0):
    G, R, K = a.shape
    M = G * R
    assert M % tm == 0 and n_cols % tn == 0 and w.shape[2 if w_transposed else 1] == K
    if w_transposed:
        assert col0 % SUBLANES == 0
        w_spec = pl.BlockSpec((pl.Element(1), pl.Element(tn), pl.Element(K)),
                              lambda j, i: (layer, pl.multiple_of(col0 + j * tn, SUBLANES), 0))
    else:
        assert col0 == 0
        w_spec = pl.BlockSpec((None, K, tn), lambda j, i: (layer, 0, j))
    out = pl.pallas_call(
        functools.partial(_mm_body, act=act, w_transposed=w_transposed),
        grid=(n_cols // tn, M // tm),
        in_specs=[pl.BlockSpec((tm, K), lambda j, i: (i, 0)), w_spec],
        out_specs=pl.BlockSpec((tm, tn), lambda j, i: (i, j)),
        out_shape=jax.ShapeDtypeStruct((M, n_cols), out_dtype),
        scratch_shapes=[pltpu.VMEM((K, tn), BF16)],
        compiler_params=_cparams("arbitrary", "arbitrary"),
        name=name,
    )(a.reshape(M, K), w)
    return out.reshape(G, R, n_cols)


def _mm_res_body(*refs, n_lhs, gb, rb, cast_w):
    a_refs = refs[:n_lhs]
    w_ref, x_ref, gt_ref, o_ref = refs[n_lhs:n_lhs + 4]
    if cast_w:
        wbf = refs[n_lhs + 4]

        @pl.when(pl.program_id(1) == 0)
        def _():
            wbf[...] = w_ref[...].astype(BF16)
    else:
        wbf = w_ref

    acc = None
    k0 = 0
    for a_ref in a_refs:
        kk = a_ref.shape[-1]
        part = jnp.dot(a_ref[...].astype(BF16), wbf[k0:k0 + kk, :], preferred_element_type=F32)
        acc = part if acc is None else acc + part
        k0 += kk
    tn = acc.shape[-1]
    o_ref[...] = x_ref[...] + gt_ref[...] * acc.reshape(gb, rb, tn)


def _mm_residual(a_list, w, layer, x, mod, i_gate, *, tm, tn, name="mm_res"):
    G, R, N = x.shape
    M = G * R
    K = w.shape[1]
    assert sum(a.shape[-1] for a in a_list) == K and w.shape[2] == N
    gb, rb, nrb = _row_blocking(G, R, tm)
    ntn = N // tn
    cast_w = w.dtype != BF16

    def xmap(j, i):
        return (i // nrb, i % nrb, j)

    return pl.pallas_call(
        functools.partial(_mm_res_body, n_lhs=len(a_list), gb=gb, rb=rb, cast_w=cast_w),
        grid=(ntn, M // tm),
        in_specs=[pl.BlockSpec((tm, a.shape[-1]), lambda j, i: (i, 0)) for a in a_list]
        + [pl.BlockSpec((None, K, tn), lambda j, i: (layer, 0, j)),
           pl.BlockSpec((gb, rb, tn), xmap),
           pl.BlockSpec((gb, 1, tn), lambda j, i: (i // nrb, 0, i_gate * ntn + j))],
        out_specs=pl.BlockSpec((gb, rb, tn), xmap),
        out_shape=jax.ShapeDtypeStruct((G, R, N), F32),
        scratch_shapes=[pltpu.VMEM((K, tn), BF16)] if cast_w else [],
        compiler_params=_cparams("arbitrary", "arbitrary"),
        name=name,
    )(*[a.reshape(M, a.shape[-1]) for a in a_list], w, x, mod)


def _mlstm_body(zq_ref, zk_ref, zv_ref, zo_ref, zg_ref, cw_ref, gbias_ref, gain_ref, conv0_ref,
                C0_ref, n0_ref, m0_ref, hm_ref, C_ref, n_ref, m_ref, ext, *, L, gs):
    W = M_WIDTH
    Dh = M_HEAD_DIM
    H = M_HEADS
    keep = M_CONV - 1
    c = pl.program_id(1)

    @pl.when(c == 0)
    def _():
        C_ref[...] = C0_ref[...]
        n_ref[...] = n0_ref[...]
        m_ref[...] = m0_ref[...]
        ext[:, SUBLANES - keep:SUBLANES, :] = conv0_ref[...]

    ext[:, SUBLANES:SUBLANES + L, 0:W] = zq_ref[...]
    ext[:, SUBLANES:SUBLANES + L, W:2 * W] = zk_ref[...]
    row = lax.broadcasted_iota(jnp.int32, (L, L), 0)
    col = lax.broadcasted_iota(jnp.int32, (L, L), 1)
    eye = row == col
    tri = col <= row
    chains = [(g, h) for g in range(gs) for h in range(H)]
    ids = range(len(chains))
    qk, gates = [], []
    for g in range(gs):
        conv = ext[g, SUBLANES:SUBLANES + L, :] * cw_ref[keep:keep + 1, :]
        for j in range(keep):
            conv = conv + ext[g, SUBLANES - keep + j:SUBLANES - keep + j + L, :] * cw_ref[j:j + 1, :]
        qk.append(_silu(conv))
        gates.append(zg_ref[g] + gbias_ref[...])
    ext[:, SUBLANES - keep:SUBLANES, :] = ext[:, SUBLANES + L - keep:SUBLANES + L, :]

    sl = [slice(h * Dh, (h + 1) * Dh) for _, h in chains]
    q = [qk[g][:, h * Dh:(h + 1) * Dh] for g, h in chains]
    k = [qk[g][:, W + h * Dh:W + (h + 1) * Dh] * (Dh ** -0.5) for g, h in chains]
    v = [zv_ref[g, :, sl[i]] for i, (g, h) in enumerate(chains)]
    ig_col = [gates[g][:, h:h + 1] for g, h in chains]
    fpre = [gates[g][:, H + h:H + h + 1] for g, h in chains]
    lf_col = [jnp.minimum(x, 0.0) - jnp.log1p(jnp.exp(-jnp.abs(x))) for x in fpre]
    lf_row = [jnp.sum(jnp.where(eye, x, 0.0), axis=0, keepdims=True) for x in lf_col]
    ig_row = [jnp.sum(jnp.where(eye, x, 0.0), axis=0, keepdims=True) for x in ig_col]
    b_col = [jnp.sum(jnp.where(tri, x, 0.0), axis=1, keepdims=True) for x in lf_row]
    b_row = [jnp.sum(jnp.where(row <= col, x, 0.0), axis=0, keepdims=True) for x in lf_col]
    dmat = [jnp.where(tri, b_col[i] - b_row[i] + ig_row[i], NEG_BIG) for i in ids]
    m_prev = [m_ref[g, :, h:h + 1] for g, h in chains]
    inter = [b_col[i] + m_prev[i] for i in ids]
    m_t = [jnp.maximum(inter[i], jnp.max(dmat[i], axis=1, keepdims=True)) for i in ids]
    w_inter = [jnp.exp(inter[i] - m_t[i]) for i in ids]
    Cm = [C_ref[g, h] for g, h in chains]
    n_row = [n_ref[g, h:h + 1, :] for g, h in chains]
    s = [_dot_nt(q[i], k[i]) * jnp.exp(dmat[i] - m_t[i]) for i in ids]
    qc = [_dot(q[i], Cm[i]) for i in ids]
    num = [_dot(s[i], v[i]) + w_inter[i] * qc[i] for i in ids]
    den = [jnp.sum(s[i], axis=1, keepdims=True) + w_inter[i] * jnp.sum(q[i] * n_row[i], axis=1, keepdims=True)
           for i in ids]
    hh = [num[i] / jnp.maximum(jnp.abs(den[i]), jnp.exp(-m_t[i])) for i in ids]
    m_new = [x[L - 1:L, :] for x in m_t]
    b_last = [x[L - 1:L, :] for x in b_col]
    wk = [jnp.exp(b_last[i] - b_col[i] + ig_col[i] - m_new[i]) * k[i] for i in ids]
    decay = [jnp.exp(b_last[i] + m_prev[i] - m_new[i]) for i in ids]
    upd = [_dot_tn(wk[i], v[i]) for i in ids]
    for i, (g, h) in enumerate(chains):
        C_ref[g, h] = decay[i] * Cm[i] + upd[i]
        n_ref[g, h:h + 1, :] = decay[i] * n_row[i] + jnp.sum(wk[i], axis=0, keepdims=True)
        m_ref[g, :, h:h + 1] = m_new[i]
        dlt = hh[i] - jnp.mean(hh[i], axis=-1, keepdims=True)
        ln = dlt * lax.rsqrt(jnp.mean(dlt * dlt, axis=-1, keepdims=True) + RMS_EPS)
        hm_ref[g, :, sl[i]] = _sigmoid(zo_ref[g, :, sl[i]]) * ln * gain_ref[:, sl[i]]


def _mlstm(zm, zgate, conv_w, gate_b, m_gain, conv0, C0, n0, m0):
    G, R, _ = zm.shape
    L = math.gcd(R, CHUNK)
    W = M_WIDTH
    gs = 2 if R > SUBLANES else 4
    gbias = jnp.zeros((1, LANES), F32).at[0, :2 * M_HEADS].set(gate_b.astype(F32))
    zspec = lambda blk: pl.BlockSpec((gs, L, W), lambda g, c: (g, c, blk))
    st4 = lambda g, c: (g, 0, 0, 0)
    st3 = lambda g, c: (g, 0, 0)
    return pl.pallas_call(
        functools.partial(_mlstm_body, L=L, gs=gs),
        grid=(G // gs, R // L),
        in_specs=[zspec(0), zspec(1), zspec(2), zspec(3),
                  pl.BlockSpec((gs, L, LANES), lambda g, c: (g, c, 0)),
                  pl.BlockSpec((M_CONV, 2 * W), lambda g, c: (0, 0)),
                  pl.BlockSpec((1, LANES), lambda g, c: (0, 0)),
                  pl.BlockSpec((1, W), lambda g, c: (0, 0)),
                  pl.BlockSpec((gs, M_CONV - 1, 2 * W), st3),
                  pl.BlockSpec((gs, M_HEADS, M_HEAD_DIM, M_HEAD_DIM), st4),
                  pl.BlockSpec((gs, M_HEADS, M_HEAD_DIM), st3),
                  pl.BlockSpec((gs, 1, M_HEADS), st3)],
        out_specs=[pl.BlockSpec((gs, L, W), lambda g, c: (g, c, 0)),
                   pl.BlockSpec((gs, M_HEADS, M_HEAD_DIM, M_HEAD_DIM), st4),
                   pl.BlockSpec((gs, M_HEADS, M_HEAD_DIM), st3),
                   pl.BlockSpec((gs, 1, M_HEADS), st3)],
        out_shape=[jax.ShapeDtypeStruct((G, R, W), F32),
                   jax.ShapeDtypeStruct((G, M_HEADS, M_HEAD_DIM, M_HEAD_DIM), F32),
                   jax.ShapeDtypeStruct((G, M_HEADS, M_HEAD_DIM), F32),
                   jax.ShapeDtypeStruct((G, 1, M_HEADS), F32)],
        scratch_shapes=[pltpu.VMEM((gs, SUBLANES + L, 2 * W), F32)],
        compiler_params=_cparams("arbitrary", "arbitrary"),
        name="mlstm",
    )(zm, zm, zm, zm, zgate, conv_w, gbias, m_gain.reshape(1, W), conv0, C0, n0, m0.reshape(G, 1, M_HEADS))


def _hgrn_body(zq_ref, zf_ref, zi_ref, zgg_ref, lb_ref, gain_ref, S0_ref, og_ref, S_ref, ST, *, nblk, gs):
    Dh = G_HEAD_DIM
    B = SUBLANES
    c = pl.program_id(1)
    chains = [(g, h) for g in range(gs) for h in range(G_HEADS)]

    @pl.when(c == 0)
    def _():
        for i, (g, h) in enumerate(chains):
            ST[i] = S0_ref[g, h].T

    rowi = lax.broadcasted_iota(jnp.int32, (B, Dh), 0)

    def blk(bi, carry):
        r0 = pl.multiple_of(bi * B, B)
        rows = pl.ds(r0, B)
        ids = range(len(chains))
        sl = [slice(h * Dh, (h + 1) * Dh) for _, h in chains]
        gi = [g for g, _ in chains]
        f = [lb_ref[:, sl[i]] + (1.0 - lb_ref[:, sl[i]]) * _sigmoid(zf_ref[gi[i], rows, sl[i]]) for i in ids]
        kk = [1.0 - x for x in f]
        bc = [_cumsum_rows(jnp.log(x)) for x in f]
        btot = [x[B - 1:B, :] for x in bc]
        qh = [_silu(zq_ref[gi[i], rows, sl[i]]) * (Dh ** -0.5) for i in ids]
        v = [zi_ref[gi[i], rows, sl[i]] for i in ids]
        st = [ST[i] for i in ids]
        o = [_dot_nt(qh[i] * jnp.exp(bc[i]), st[i]) for i in ids]
        upd = [_dot_tn(v[i], kk[i] * jnp.exp(btot[i] - bc[i])) for i in ids]
        for s in range(B):
            p = [jnp.exp(jnp.where(rowi >= s, bc[i] - bc[i][s:s + 1, :], NEG_BIG)) * qh[i] * kk[i][s:s + 1, :]
                 for i in ids]
            o = [o[i] + jnp.sum(p[i], axis=-1, keepdims=True) * v[i][s:s + 1, :] for i in ids]
        for i in ids:
            ST[i] = st[i] * jnp.exp(btot[i]) + upd[i]
            rms = o[i] * lax.rsqrt(jnp.mean(o[i] * o[i], axis=-1, keepdims=True) + RMS_EPS)
            og_ref[gi[i], rows, sl[i]] = rms * gain_ref[:, sl[i]] * _silu(zgg_ref[gi[i], rows, sl[i]])
        return carry

    lax.fori_loop(0, nblk, blk, 0)

    @pl.when(c == pl.num_programs(1) - 1)
    def _():
        for i, (g, h) in enumerate(chains):
            S_ref[g, h] = ST[i].T


def _hgrn(zg, lb, g_gain, S0):
    G, R, _ = zg.shape
    W = G_WIDTH
    gs = 4
    Rc = min(R, 128)
    zspec = lambda blk: pl.BlockSpec((gs, Rc, W), lambda g, c: (g, c, blk))
    st4 = lambda g, c: (g, 0, 0, 0)
    return pl.pallas_call(
        functools.partial(_hgrn_body, nblk=Rc // SUBLANES, gs=gs),
        grid=(G // gs, R // Rc),
        in_specs=[zspec(0), zspec(1), zspec(2), zspec(3),
                  pl.BlockSpec((1, W), lambda g, c: (0, 0)),
                  pl.BlockSpec((1, W), lambda g, c: (0, 0)),
                  pl.BlockSpec((gs, G_HEADS, G_HEAD_DIM, G_HEAD_DIM), st4)],
        out_specs=[pl.BlockSpec((gs, Rc, W), lambda g, c: (g, c, 0)),
                   pl.BlockSpec((gs, G_HEADS, G_HEAD_DIM, G_HEAD_DIM), st4)],
        out_shape=[jax.ShapeDtypeStruct((G, R, W), F32),
                   jax.ShapeDtypeStruct((G, G_HEADS, G_HEAD_DIM, G_HEAD_DIM), F32)],
        scratch_shapes=[pltpu.VMEM((gs * G_HEADS, G_HEAD_DIM, G_HEAD_DIM), F32)],
        compiler_params=_cparams("arbitrary", "arbitrary"),
        name="hgrn2",
    )(zg, zg, zg, zg, lb.reshape(1, W), g_gain.reshape(1, W), S0)


def _rprep_body(x_ref, g_ref, sc_ref, sh_ref, shift0_ref, mu_ref, *refs, gb, rb):
    outs = refs[:6]
    hlast_ref, hbuf = refs[6:]
    B = SUBLANES
    n, tok, seq = _strips(gb, rb)
    S = hbuf.shape[1] - B
    whole_sequences = gb > 1

    if not whole_sequences:
        @pl.when(pl.program_id(1) == 0)
        def _():
            hbuf[:, B - 1:B, :] = shift0_ref[...]

    def strip(i, carry):
        if whole_sequences:
            hbuf[:, B - 1:B, :] = shift0_ref[seq(i)]
        h = _rms(x_ref[tok(i)], g_ref[...]) * (1.0 + sc_ref[seq(i)]) + sh_ref[seq(i)]
        hbuf[:, B:B + S, :] = h
        xx = hbuf[:, B - 1:B - 1 + S, :] - h
        last = h[:, S - 1:S, :]
        if whole_sequences:
            hlast_ref[seq(i)] = last
        else:
            hbuf[:, B - 1:B, :] = last
        for m in range(6):
            outs[m][tok(i)] = (h + xx * mu_ref[m:m + 1, :]).astype(BF16)
        return carry

    lax.fori_loop(0, n, strip, 0, unroll=4)
    if not whole_sequences:
        hlast_ref[...] = hbuf[:, B - 1:B, :]


def _rwkv_prep(x, g, mod, i_scale, i_shift, shift0, mu, tm):
    G, R, D = x.shape
    gb, rb, nrb = _row_blocking(G, R, tm)
    tok = pl.BlockSpec((gb, rb, D), lambda a, b: (a, b, 0))
    one = pl.BlockSpec((gb, 1, D), lambda a, b: (a, 0, 0))
    strip_shape = (1, SUBLANES + STRIP_ROWS, D) if gb == 1 else (STRIP_ROWS // rb, SUBLANES + rb, D)
    res = pl.pallas_call(
        functools.partial(_rprep_body, gb=gb, rb=rb),
        grid=(G // gb, nrb),
        in_specs=[tok,
                  pl.BlockSpec((1, 1, D), lambda a, b: (0, 0, 0)),
                  pl.BlockSpec((gb, 1, D), lambda a, b: (a, 0, i_scale)),
                  pl.BlockSpec((gb, 1, D), lambda a, b: (a, 0, i_shift)),
                  one,
                  pl.BlockSpec((6, D), lambda a, b: (0, 0))],
        out_specs=[tok] * 6 + [one],
        out_shape=[jax.ShapeDtypeStruct((G, R, D), BF16)] * 6 + [jax.ShapeDtypeStruct((G, 1, D), F32)],
        scratch_shapes=[pltpu.VMEM(strip_shape, F32)],
        compiler_params=_cparams("arbitrary", "arbitrary"),
        name="rwkv_prep",
    )(x, g.reshape(1, 1, D), mod, mod, shift0.reshape(G, 1, D), mu)
    return res[:6], res[6]


def _lora_body(x_ref, w1_ref, w2_ref, b_ref, o_ref, *, act, post):
    t = _dot(x_ref[...], w1_ref[...])
    if act == "tanh":
        t = jnp.tanh(t)
    elif act == "sigmoid":
        t = _sigmoid(t)
    y = _dot(t, w2_ref[...])
    if post == "log_decay":
        y = -jnp.exp(-_softplus(-(b_ref[...] + y)) - 0.5)
    elif post == "sigmoid":
        y = _sigmoid(b_ref[...] + y)
    o_ref[...] = y


def _lora(x, w1, w2, act, tm, post=None, bias=None):
    G, R, K = x.shape
    M = G * R
    r = w1.shape[1]
    rp = -(-r // LANES) * LANES
    w1p = jnp.zeros((K, rp), F32).at[:, :r].set(w1)
    w2p = jnp.zeros((rp, w2.shape[1]), F32).at[:r, :].set(w2)
    N = w2.shape[1]
    b = jnp.zeros((1, N), F32) if bias is None else bias.reshape(1, N).astype(F32)
    out = pl.pallas_call(
        functools.partial(_lora_body, act=act, post=post),
        grid=(M // tm,),
        in_specs=[pl.BlockSpec((tm, K), lambda i: (i, 0)),
                  pl.BlockSpec((K, rp), lambda i: (0, 0)),
                  pl.BlockSpec((rp, N), lambda i: (0, 0)),
                  pl.BlockSpec((1, N), lambda i: (0, 0))],
        out_specs=pl.BlockSpec((tm, N), lambda i: (i, 0)),
        out_shape=jax.ShapeDtypeStruct((M, N), F32),
        compiler_params=_cparams("arbitrary"),
        name="lora_" + str(act),
    )(x.reshape(M, K), w1p, w2p, b)
    return out.reshape(G, R, N)


def _head_mask(rows, cols, rper, cper):
    r = lax.broadcasted_iota(jnp.int32, (rows, cols), 0)
    c = lax.broadcasted_iota(jnp.int32, (rows, cols), 1)
    return (r // rper) == (c // cper)


def _bd(y, mask01):
    if y.shape[0] % (2 * SUBLANES) == 0:
        return jnp.concatenate([y.astype(BF16)] * R_GROUP, axis=0) * mask01
    return jnp.concatenate([y] * R_GROUP, axis=0).astype(BF16) * mask01


def _segsums(xs, ones_bd):
    c = xs[0].shape[0]
    if c % (2 * SUBLANES) == 0:
        lhs = jnp.concatenate([x.astype(BF16) for x in xs], axis=0)
    else:
        lhs = jnp.concatenate(xs, axis=0).astype(BF16)
    res = jnp.dot(lhs, ones_bd, preferred_element_type=F32)
    return [res[i * c:(i + 1) * c] for i in range(len(xs))]


def _rwkv_body(r_ref, k_ref, v_ref, lw_ref, a_ref, g_ref, kkp_ref, kap_ref, rk_ref,
               lnw_ref, lnb_ref, S0_ref, y_ref, S_ref, Sbd, *, c, nchunk, ng, gs):
    N = R_HEAD_DIM
    GW = R_GROUP_W
    j = pl.program_id(2)
    m_state = _head_mask(GW, GW, N, N)
    chains = [(q, gi) for q in range(gs) for gi in range(ng)]
    ids = range(len(chains))
    lanes = [slice(gi * GW, (gi + 1) * GW) for _, gi in chains]
    seqs = [q for q, _ in chains]

    @pl.when(j == 0)
    def _():
        for i, (q, gi) in enumerate(chains):
            s0 = S0_ref[q, gi * R_GROUP:(gi + 1) * R_GROUP].reshape(GW, N)
            Sbd[i] = jnp.where(m_state, jnp.concatenate([s0] * R_GROUP, axis=1), 0.0)

    ones_bd = jnp.where(m_state, 1.0, 0.0).astype(BF16)
    m_vec = jnp.where(_head_mask(R_GROUP * c, GW, c, N), 1.0, 0.0).astype(BF16)
    m_mat = jnp.where(_head_mask(R_GROUP * c, R_GROUP * c, c, c), 1.0, 0.0).astype(BF16)
    t_idx = lax.broadcasted_iota(jnp.int32, (2 * c, R_GROUP * c), 0)
    s_idx = lax.broadcasted_iota(jnp.int32, (2 * c, R_GROUP * c), 1) % c
    causal = jnp.where(t_idx < c, jnp.where(s_idx < t_idx, 1.0, 0.0), jnp.where(s_idx <= t_idx - c, 1.0, 0.0))
    nt = (((1,), (1,)), ((), ()))
    mm = lambda a, b: jnp.dot(a.astype(BF16), b, preferred_element_type=F32)

    keys = ("ar", "bdb", "bdk", "bdv", "vb", "gam", "tail")

    def prepare_steps(ci, out):
        rows = pl.ds(pl.multiple_of(ci * c, c), c)
        for i in ids:
            ld = lambda ref: ref[seqs[i], rows, lanes[i]]
            r, k, v, lw, a, g = ld(r_ref), ld(k_ref), ld(v_ref), ld(lw_ref), ld(a_ref), ld(g_ref)
            kk = k * kkp_ref[:, lanes[i]]
            k2 = k * (1.0 + (a - 1.0) * kap_ref[:, lanes[i]])
            nrm2, rk_sum = _segsums([kk * kk, r * k2 * rk_ref[:, lanes[i]]], ones_bd)
            cw = _cumsum_rows(lw)
            cwl = cw[c - 1:c, :]
            yield
            kk = kk / jnp.maximum(jnp.sqrt(nrm2), 1e-12)
            bb = kk * a
            e_out = jnp.exp(-cw)
            e_end = jnp.exp(cwl - cw)
            out["ar"].append(jnp.concatenate([-kk * jnp.exp(cw - lw), r * jnp.exp(cw)], axis=0).astype(BF16))
            out["gam"].append(jnp.exp(cwl))
            out["tail"].append(jnp.concatenate([rk_sum * v, g], axis=0))
            yield
            out["bdb"].append(_bd(bb * e_out, m_vec))
            out["bdk"].append(_bd(k2 * e_out, m_vec))
            out["bdv"].append(_bd(v, m_vec))
            out["vb"].append(jnp.concatenate([v, bb * e_end, k2 * e_end], axis=0))
            yield

    def prepare(ci):
        out = {key: [] for key in keys}
        for _ in prepare_steps(ci, out):
            pass
        return out

    def chain(ci, p, side_work=None):
        def tick():
            if side_work is not None:
                next(side_work, None)

        rows = pl.ds(pl.multiple_of(ci * c, c), c)
        ar, bdv = p["ar"], p["bdv"]
        sbd = [Sbd[i] for i in ids]
        pb = [lax.dot_general(ar[i], p["bdb"][i], nt, preferred_element_type=F32) * causal for i in ids]
        pk = [lax.dot_general(ar[i], p["bdk"][i], nt, preferred_element_type=F32) * causal for i in ids]
        tick()
        s0p = [lax.dot_general(ar[i], sbd[i].astype(BF16), nt, preferred_element_type=F32) for i in ids]
        u = [s0p[i][:c] + mm(pk[i][:c], bdv[i]) for i in ids]
        tick()
        pw = [pb[i][:c] for i in ids]
        step = 1
        while step < c:
            bdu = [_bd(u[i], m_vec) for i in ids]
            u = [u[i] + mm(pw[i], bdu[i]) for i in ids]
            tick()
            step *= 2
            if step < c:
                bdp = [_bd(pw[i], m_mat) for i in ids]
                pw = [mm(pw[i], bdp[i]) for i in ids]
                tick()
        bdu = [_bd(u[i], m_vec) for i in ids]
        y = [s0p[i][c:] + mm(pb[i][c:], bdu[i]) + mm(pk[i][c:], bdv[i]) for i in ids]
        tick()
        upd = [_dot_tn(jnp.concatenate([u[i], p["vb"][i][:c]], axis=0), p["vb"][i][c:]) for i in ids]
        for i in ids:
            Sbd[i] = sbd[i] * p["gam"][i] + jnp.where(m_state, upd[i], 0.0)
        tick()
        ysum = [_segsums([y[i]], ones_bd)[0] for i in ids]
        dlt = [y[i] - ysum[i] * (1.0 / N) for i in ids]
        vsum = [_segsums([dlt[i] * dlt[i]], ones_bd)[0] for i in ids]
        for i in ids:
            yn = dlt[i] * lax.rsqrt(vsum[i] * (1.0 / N) + LN_X_EPS) * lnw_ref[:, lanes[i]] + lnb_ref[:, lanes[i]]
            y_ref[seqs[i], rows, lanes[i]] = ((yn + p["tail"][i][:c]) * p["tail"][i][c:]).astype(y_ref.dtype)
        if side_work is not None:
            for _ in side_work:
                pass

    if nchunk == 1:
        chain(0, prepare(0))
    else:
        def body(ci, p):
            nxt = {key: [] for key in keys}
            chain(ci, p, prepare_steps(jnp.minimum(ci + 1, nchunk - 1), nxt))
            return nxt

        lax.fori_loop(0, nchunk, body, prepare(0))

    @pl.when(j == pl.num_programs(2) - 1)
    def _():
        for i, (q, gi) in enumerate(chains):
            sbd = Sbd[i]
            for h in range(R_GROUP):
                S_ref[q, gi * R_GROUP + h] = sbd[h * N:(h + 1) * N, h * N:(h + 1) * N]


def _rwkv(r, k, v, lw, a, g, kkp, kap, rk, lnw, lnb, S0):
    G, R, D = r.shape
    c = math.gcd(R, CHUNK)
    GW = R_GROUP_W
    if R > SUBLANES:
        gs, ng, Rc = 1, 4, min(R, 512)
    else:
        gs, ng, Rc = 2, D // GW, R
    bw = ng * GW
    tok = pl.BlockSpec((gs, Rc, bw), lambda a_, b, j: (a_, j, b))
    par = pl.BlockSpec((1, bw), lambda a_, b, j: (0, b))
    st = pl.BlockSpec((gs, ng * R_GROUP, R_HEAD_DIM, R_HEAD_DIM), lambda a_, b, j: (a_, b, 0, 0))
    row = lambda p: p.reshape(1, D).astype(F32)
    return pl.pallas_call(
        functools.partial(_rwkv_body, c=c, nchunk=Rc // c, ng=ng, gs=gs),
        grid=(G // gs, D // bw, R // Rc),
        in_specs=[tok] * 6 + [par] * 5 + [st],
        out_specs=[tok, st],
        out_shape=[jax.ShapeDtypeStruct((G, R, D), BF16),
                   jax.ShapeDtypeStruct((G, R_HEADS, R_HEAD_DIM, R_HEAD_DIM), F32)],
        scratch_shapes=[pltpu.VMEM((gs * ng, GW, GW), F32)],
        compiler_params=_cparams("arbitrary", "arbitrary", "arbitrary"),
        name="rwkv7",
    )(r, k, v, lw, a, g, row(kkp), row(kap), row(rk), row(lnw), row(lnb), S0)


def _trunk(x, mod, m_C, m_n, m_m, m_conv, g_S, r_S, r_shift, p, lbs, tm):
    G, R, D = x.shape
    md = mod[0]
    h = _modnorm(x, p["norm_mix"][0], md, 1, 0, tm)
    w_in_t = jnp.swapaxes(p["ab_w_in"], 1, 2)
    zm = _mm(h, w_in_t, 0, n_cols=4 * M_WIDTH, tm=tm, tn=1024, name="mm_in_m", w_transposed=True)
    zgate = _mm(h, w_in_t, 0, n_cols=LANES, tm=tm, tn=LANES, name="mm_in_gate", w_transposed=True, col0=4 * M_WIDTH)
    zg = _mm(h, w_in_t, 0, n_cols=4 * G_WIDTH, tm=tm, tn=1024, name="mm_in_g", w_transposed=True,
             col0=4 * M_WIDTH + 2 * M_HEADS)
    hm, C, n, m = _mlstm(zm, zgate, p["m_conv_w"][0], p["ab_gate_b"][0], p["m_norm"][0],
                         m_conv[0], m_C[0], m_n[0], m_m[0])
    conv_new = zm[:, R - (M_CONV - 1):, :2 * M_WIDTH]
    og, S = _hgrn(zg, lbs[0], p["g_norm"][0], g_S[0])
    x = _mm_residual([hm, og], p["ab_w_out"], 0, x, md, 2, tm=tm, tn=512, name="mm_out0")
    h = _modnorm(x, p["norm_ffn"][0], md, 4, 3, tm)
    act = _mm(h, p["ffn_w1"], 0, n_cols=4 * D, tm=tm, tn=1024, act="relu2", out_dtype=BF16, name="ffn_up")
    x = _mm_residual([act], p["ffn_w2_bf16"], 0, x, md, 5, tm=256, tn=1024, name="ffn_down")
    md = mod[1]
    (xr, xw, xk, xv, xa, xg), shift_new = _rwkv_prep(x, p["norm_mix"][1], md, 1, 0, r_shift[0], p["r_mu"][0],
                                                     min(tm, 256))
    r = _mm(xr, p["r_wr"], 0, n_cols=D, tm=tm, tn=1024, name="mm_r")
    k = _mm(xk, p["r_wk"], 0, n_cols=D, tm=tm, tn=1024, name="mm_k")
    v = _mm(xv, p["r_wv"], 0, n_cols=D, tm=tm, tn=1024, name="mm_v")
    lw = _lora(xw, p["r_w1"][0], p["r_w2"][0], "tanh", min(tm, 512), post="log_decay", bias=p["r_w0"][0])
    aa = _lora(xa, p["r_a1"][0], p["r_a2"][0], None, min(tm, 512), post="sigmoid", bias=p["r_a0"][0])
    gg = _lora(xg, p["r_g1"][0], p["r_g2"][0], "sigmoid", min(tm, 512))
    yg, rS = _rwkv(r, k, v, lw, aa, gg, p["r_kk"][0], p["r_ka"][0], p["r_rk"][0].reshape(-1), p["r_lnw"][0],
                   p["r_lnb"][0], r_S[0])
    x = _mm_residual([yg], p["r_wo"], 0, x, md, 2, tm=tm, tn=1024, name="mm_out1")
    h = _modnorm(x, p["norm_ffn"][1], md, 4, 3, tm)
    act = _mm(h, p["ffn_w1"], 1, n_cols=4 * D, tm=tm, tn=1024, act="relu2", out_dtype=BF16, name="ffn_up")
    x = _mm_residual([act], p["ffn_w2_bf16"], 1, x, md, 5, tm=256, tn=1024, name="ffn_down")
    y = _rmsnorm(x, p["final_norm"], min(tm, 512))
    return y, (C[None], n[None], m.reshape(1, G, M_HEADS), conv_new[None], S[None], rS[None],
               shift_new.reshape(1, G, D))


def kernel(x_prompt, x_sample, c_prompt, c_sample, state_mlstm_C, state_mlstm_n, state_mlstm_m, state_mlstm_conv, state_hgrn_S, state_rwkv_S, state_rwkv_shift, mod_w, mod_b, norm_mix, norm_ffn, ffn_w1, ffn_w2, final_norm, ab_w_in, ab_gate_b, m_conv_w, m_norm, g_lb, g_norm, ab_w_out, r_mu, r_w0, r_w1, r_w2, r_a0, r_a1, r_a2, r_g1, r_g2, r_kk, r_ka, r_rk, r_wr, r_wk, r_wv, r_wo, r_lnw, r_lnb):
    p = dict(norm_mix=norm_mix, norm_ffn=norm_ffn, ffn_w1=ffn_w1, ffn_w2=ffn_w2, final_norm=final_norm,
             ab_w_in=ab_w_in, ab_gate_b=ab_gate_b, m_conv_w=m_conv_w, m_norm=m_norm, g_norm=g_norm,
             ab_w_out=ab_w_out, r_mu=r_mu, r_w0=r_w0, r_w1=r_w1, r_w2=r_w2, r_a0=r_a0, r_a1=r_a1, r_a2=r_a2,
             r_g1=r_g1, r_g2=r_g2, r_kk=r_kk, r_ka=r_ka, r_rk=r_rk, r_wr=r_wr, r_wk=r_wk, r_wv=r_wv,
             r_wo=r_wo, r_lnw=r_lnw, r_lnb=r_lnb)
    p["ffn_w2_bf16"] = ffn_w2.astype(BF16)
    B, T, D = x_prompt.shape
    Bs = x_sample.shape[0]
    depth = mod_w.shape[0]
    pad = (-(Bs + B)) % SUBLANES
    c_all = jnp.concatenate([c_sample, c_prompt, jnp.zeros((pad, D), F32)], axis=0)
    mod = _modulation(c_all, mod_w, mod_b)
    mod_s = mod[:, :Bs].reshape(depth, Bs, 1, N_MOD * D)
    mod_p = mod[:, Bs:Bs + B].reshape(depth, B, 1, N_MOD * D)
    lbs = jnp.cumsum(jax.nn.softmax(g_lb.astype(F32), axis=0), axis=0)

    n_even = state_mlstm_C.shape[0]
    n_odd = state_rwkv_S.shape[0]
    z = lambda *s: jnp.zeros(s, F32)
    yp, sp = _trunk(x_prompt, mod_p,
                    z(n_even, B, M_HEADS, M_HEAD_DIM, M_HEAD_DIM), z(n_even, B, M_HEADS, M_HEAD_DIM),
                    z(n_even, B, M_HEADS), z(n_even, B, M_CONV - 1, 2 * M_WIDTH),
                    z(n_even, B, G_HEADS, G_HEAD_DIM, G_HEAD_DIM), z(n_odd, B, R_HEADS, R_HEAD_DIM, R_HEAD_DIM),
                    z(n_odd, B, D), p, lbs, 1024)
    ys, ss = _trunk(x_sample, mod_s, state_mlstm_C, state_mlstm_n, state_mlstm_m, state_mlstm_conv,
                    state_hgrn_S, state_rwkv_S, state_rwkv_shift, p, lbs, 1024)
    return (yp, ys) + tuple(sp) + tuple(ss)
```

```python
import functools
import math

import jax
import jax.numpy as jnp
from jax import lax
from jax.experimental import pallas as pl
from jax.experimental.pallas import tpu as pltpu

F32 = jnp.float32
BF16 = jnp.bfloat16

D_MODEL = 2048
M_HEADS = 4
M_HEAD_DIM = 256
M_WIDTH = M_HEADS * M_HEAD_DIM
M_CONV = 4
G_HEADS = 8
G_HEAD_DIM = 128
G_WIDTH = G_HEADS * G_HEAD_DIM
R_HEAD_DIM = 64
R_HEADS = D_MODEL // R_HEAD_DIM
R_GROUP = 4
R_GROUP_W = R_GROUP * R_HEAD_DIM
N_MOD = 6
RMS_EPS = 1e-6
LN_X_EPS = 64e-5
CHUNK = 64
NEG_BIG = -1e30

V7X_VMEM_LIMIT_BYTES = 56 * 1024 * 1024
SUBLANES = 8
LANES = 128


def _cparams(*sem):
    return pltpu.CompilerParams(dimension_semantics=sem, vmem_limit_bytes=V7X_VMEM_LIMIT_BYTES)


def _sigmoid(x):
    return 1.0 / (1.0 + jnp.exp(-x))


def _silu(x):
    return x * _sigmoid(x)


def _softplus(x):
    return jnp.maximum(x, 0.0) + jnp.log1p(jnp.exp(-jnp.abs(x)))


def _dot(a, b):
    return jnp.dot(a.astype(BF16), b.astype(BF16), preferred_element_type=F32)


def _dot_nt(a, b):
    return lax.dot_general(a.astype(BF16), b.astype(BF16), (((1,), (1,)), ((), ())), preferred_element_type=F32)


def _dot_tn(a, b):
    return lax.dot_general(a.astype(BF16), b.astype(BF16), (((0,), (0,)), ((), ())), preferred_element_type=F32)


def _cumsum_rows(x):
    n = x.shape[0]
    row = lax.broadcasted_iota(jnp.int32, x.shape, 0)
    s = 1
    while s < n:
        x = x + jnp.where(row >= s, pltpu.roll(x, s, axis=0), 0.0)
        s *= 2
    return x


def _row_blocking(G, R, tm):
    if R >= tm:
        assert R % tm == 0
        return 1, tm, R // tm
    assert tm % R == 0 and G % (tm // R) == 0
    return tm // R, R, 1


def _mod_body(c_ref, w_ref, b_ref, o_ref):
    sc = _silu(c_ref[...])
    o_ref[...] = _dot(sc, w_ref[...]) + b_ref[...]


def _modulation(c_all, mod_w, mod_b):
    L, K, N = mod_w.shape
    Mc = c_all.shape[0]
    tn = 1024
    return pl.pallas_call(
        _mod_body,
        grid=(L, N // tn),
        in_specs=[pl.BlockSpec((Mc, K), lambda l, j: (0, 0)),
                  pl.BlockSpec((None, K, tn), lambda l, j: (l, 0, j)),
                  pl.BlockSpec((None, 1, tn), lambda l, j: (l, 0, j))],
        out_specs=pl.BlockSpec((None, Mc, tn), lambda l, j: (l, 0, j)),
        out_shape=jax.ShapeDtypeStruct((L, Mc, N), F32),
        compiler_params=_cparams("arbitrary", "arbitrary"),
        name="modulation",
    )(c_all, mod_w, mod_b.reshape(L, 1, N))


def _rms(x, g):
    ms = jnp.mean(x * x, axis=-1, keepdims=True)
    return x * lax.rsqrt(ms + RMS_EPS) * g


STRIP_ROWS = 2 * SUBLANES


def _strips(gb, rb):
    if gb == 1:
        n = rb // STRIP_ROWS
        tok = lambda i: (slice(None), pl.ds(pl.multiple_of(i * STRIP_ROWS, STRIP_ROWS), STRIP_ROWS), slice(None))
        seq = lambda i: (slice(None), slice(None), slice(None))
    else:
        assert STRIP_ROWS % rb == 0
        per = STRIP_ROWS // rb
        n = gb // per
        tok = seq = lambda i: (pl.ds(i * per, per), slice(None), slice(None))
    return n, tok, seq


def _modnorm_body(x_ref, g_ref, sc_ref, sh_ref, o_ref, *, gb, rb):
    n, tok, seq = _strips(gb, rb)

    def strip(i, carry):
        y = _rms(x_ref[tok(i)], g_ref[...])
        o_ref[tok(i)] = (y * (1.0 + sc_ref[seq(i)]) + sh_ref[seq(i)]).astype(o_ref.dtype)
        return carry

    lax.fori_loop(0, n, strip, 0, unroll=4)


def _modnorm(x, g, mod, i_scale, i_shift, tm):
    G, R, D = x.shape
    gb, rb, nrb = _row_blocking(G, R, tm)
    return pl.pallas_call(
        functools.partial(_modnorm_body, gb=gb, rb=rb),
        grid=(G // gb, nrb),
        in_specs=[pl.BlockSpec((gb, rb, D), lambda a, b: (a, b, 0)),
                  pl.BlockSpec((1, 1, D), lambda a, b: (0, 0, 0)),
                  pl.BlockSpec((gb, 1, D), lambda a, b: (a, 0, i_scale)),
                  pl.BlockSpec((gb, 1, D), lambda a, b: (a, 0, i_shift))],
        out_specs=pl.BlockSpec((gb, rb, D), lambda a, b: (a, b, 0)),
        out_shape=jax.ShapeDtypeStruct((G, R, D), BF16),
        compiler_params=_cparams("arbitrary", "arbitrary"),
        name="modnorm",
    )(x, g.reshape(1, 1, D), mod, mod)


def _rmsnorm_body(x_ref, g_ref, o_ref, *, gb, rb):
    n, tok, _ = _strips(gb, rb)

    def strip(i, carry):
        o_ref[tok(i)] = _rms(x_ref[tok(i)], g_ref[...])
        return carry

    lax.fori_loop(0, n, strip, 0, unroll=4)


def _rmsnorm(x, g, tm):
    G, R, D = x.shape
    gb, rb, nrb = _row_blocking(G, R, tm)
    return pl.pallas_call(
        functools.partial(_rmsnorm_body, gb=gb, rb=rb),
        grid=(G // gb, nrb),
        in_specs=[pl.BlockSpec((gb, rb, D), lambda a, b: (a, b, 0)),
                  pl.BlockSpec((1, 1, D), lambda a, b: (0, 0, 0))],
        out_specs=pl.BlockSpec((gb, rb, D), lambda a, b: (a, b, 0)),
        out_shape=jax.ShapeDtypeStruct((G, R, D), F32),
        compiler_params=_cparams("arbitrary", "arbitrary"),
        name="final_norm",
    )(x, g.reshape(1, 1, D))


def _mm_body(a_ref, w_ref, o_ref, wbf, *, act, w_transposed):
    @pl.when(pl.program_id(1) == 0)
    def _():
        wbf[...] = (w_ref[0].T if w_transposed else w_ref[...]).astype(BF16)

    acc = jnp.dot(a_ref[...].astype(BF16), wbf[...], preferred_element_type=F32)
    if act == "relu2":
        acc = jnp.square(jnp.maximum(acc, 0.0))
    o_ref[...] = acc.astype(o_ref.dtype)


def _mm(a, w, layer, *, n_cols, tm, tn, act=None, out_dtype=F32, name="mm", w_transposed=False, col0=0):
    G, R, K = a.shape
    M = G * R
    assert M % tm == 0 and n_cols % tn == 0 and w.shape[2 if w_transposed else 1] == K
    if w_transposed:
        assert col0 % SUBLANES == 0
        w_spec = pl.BlockSpec((pl.Element(1), pl.Element(tn), pl.Element(K)),
                              lambda j, i: (layer, pl.multiple_of(col0 + j * tn, SUBLANES), 0))
    else:
        assert col0 == 0
        w_spec = pl.BlockSpec((None, K, tn), lambda j, i: (layer, 0, j))
    out = pl.pallas_call(
        functools.partial(_mm_body, act=act, w_transposed=w_transposed),
        grid=(n_cols // tn, M // tm),
        in_specs=[pl.BlockSpec((tm, K), lambda j, i: (i, 0)), w_spec],
        out_specs=pl.BlockSpec((tm, tn), lambda j, i: (i, j)),
        out_shape=jax.ShapeDtypeStruct((M, n_cols), out_dtype),
        scratch_shapes=[pltpu.VMEM((K, tn), BF16)],
        compiler_params=_cparams("arbitrary", "arbitrary"),
        name=name,
    )(a.reshape(M, K), w)
    return out.reshape(G, R, n_cols)


def _mm_res_body(*refs, n_lhs, gb, rb, cast_w):
    a_refs = refs[:n_lhs]
    w_ref, x_ref, gt_ref, o_ref = refs[n_lhs:n_lhs + 4]
    if cast_w:
        wbf = refs[n_lhs + 4]

        @pl.when(pl.program_id(1) == 0)
        def _():
            wbf[...] = w_ref[...].astype(BF16)
    else:
        wbf = w_ref

    acc = None
    k0 = 0
    for a_ref in a_refs:
        kk = a_ref.shape[-1]
        part = jnp.dot(a_ref[...].astype(BF16), wbf[k0:k0 + kk, :], preferred_element_type=F32)
        acc = part if acc is None else acc + part
        k0 += kk
    tn = acc.shape[-1]
    o_ref[...] = x_ref[...] + gt_ref[...] * acc.reshape(gb, rb, tn)


def _mm_residual(a_list, w, layer, x, mod, i_gate, *, tm, tn, name="mm_res"):
    G, R, N = x.shape
    M = G * R
    K = w.shape[1]
    assert sum(a.shape[-1] for a in a_list) == K and w.shape[2] == N
    gb, rb, nrb = _row_blocking(G, R, tm)
    ntn = N // tn
    cast_w = w.dtype != BF16

    def xmap(j, i):
        return (i // nrb, i % nrb, j)

    return pl.pallas_call(
        functools.partial(_mm_res_body, n_lhs=len(a_list), gb=gb, rb=rb, cast_w=cast_w),
        grid=(ntn, M // tm),
        in_specs=[pl.BlockSpec((tm, a.shape[-1]), lambda j, i: (i, 0)) for a in a_list]
        + [pl.BlockSpec((None, K, tn), lambda j, i: (layer, 0, j)),
           pl.BlockSpec((gb, rb, tn), xmap),
           pl.BlockSpec((gb, 1, tn), lambda j, i: (i // nrb, 0, i_gate * ntn + j))],
        out_specs=pl.BlockSpec((gb, rb, tn), xmap),
        out_shape=jax.ShapeDtypeStruct((G, R, N), F32),
        scratch_shapes=[pltpu.VMEM((K, tn), BF16)] if cast_w else [],
        compiler_params=_cparams("arbitrary", "arbitrary"),
        name=name,
    )(*[a.reshape(M, a.shape[-1]) for a in a_list], w, x, mod)


def _mlstm_body(zq_ref, zk_ref, zv_ref, zo_ref, zg_ref, cw_ref, gbias_ref, gain_ref, conv0_ref,
                C0_ref, n0_ref, m0_ref, hm_ref, C_ref, n_ref, m_ref, ext, *, L, gs):
    W = M_WIDTH
    Dh = M_HEAD_DIM
    H = M_HEADS
    keep = M_CONV - 1
    c = pl.program_id(1)

    @pl.when(c == 0)
    def _():
        C_ref[...] = C0_ref[...]
        n_ref[...] = n0_ref[...]
        m_ref[...] = m0_ref[...]
        ext[:, SUBLANES - keep:SUBLANES, :] = conv0_ref[...]

    ext[:, SUBLANES:SUBLANES + L, 0:W] = zq_ref[...]
    ext[:, SUBLANES:SUBLANES + L, W:2 * W] = zk_ref[...]
    row = lax.broadcasted_iota(jnp.int32, (L, L), 0)
    col = lax.broadcasted_iota(jnp.int32, (L, L), 1)
    eye = row == col
    tri = col <= row
    chains = [(g, h) for g in range(gs) for h in range(H)]
    ids = range(len(chains))
    qk, gates = [], []
    for g in range(gs):
        conv = ext[g, SUBLANES:SUBLANES + L, :] * cw_ref[keep:keep + 1, :]
        for j in range(keep):
            conv = conv + ext[g, SUBLANES - keep + j:SUBLANES - keep + j + L, :] * cw_ref[j:j + 1, :]
        qk.append(_silu(conv))
        gates.append(zg_ref[g] + gbias_ref[...])
    ext[:, SUBLANES - keep:SUBLANES, :] = ext[:, SUBLANES + L - keep:SUBLANES + L, :]

    sl = [slice(h * Dh, (h + 1) * Dh) for _, h in chains]
    q = [qk[g][:, h * Dh:(h + 1) * Dh] for g, h in chains]
    k = [qk[g][:, W + h * Dh:W + (h + 1) * Dh] * (Dh ** -0.5) for g, h in chains]
    v = [zv_ref[g, :, sl[i]] for i, (g, h) in enumerate(chains)]
    ig_col = [gates[g][:, h:h + 1] for g, h in chains]
    fpre = [gates[g][:, H + h:H + h + 1] for g, h in chains]
    lf_col = [jnp.minimum(x, 0.0) - jnp.log1p(jnp.exp(-jnp.abs(x))) for x in fpre]
    lf_row = [jnp.sum(jnp.where(eye, x, 0.0), axis=0, keepdims=True) for x in lf_col]
    ig_row = [jnp.sum(jnp.where(eye, x, 0.0), axis=0, keepdims=True) for x in ig_col]
    b_col = [jnp.sum(jnp.where(tri, x, 0.0), axis=1, keepdims=True) for x in lf_row]
    b_row = [jnp.sum(jnp.where(row <= col, x, 0.0), axis=0, keepdims=True) for x in lf_col]
    dmat = [jnp.where(tri, b_col[i] - b_row[i] + ig_row[i], NEG_BIG) for i in ids]
    m_prev = [m_ref[g, :, h:h + 1] for g, h in chains]
    inter = [b_col[i] + m_prev[i] for i in ids]
    m_t = [jnp.maximum(inter[i], jnp.max(dmat[i], axis=1, keepdims=True)) for i in ids]
    w_inter = [jnp.exp(inter[i] - m_t[i]) for i in ids]
    Cm = [C_ref[g, h] for g, h in chains]
    n_row = [n_ref[g, h:h + 1, :] for g, h in chains]
    s = [_dot_nt(q[i], k[i]) * jnp.exp(dmat[i] - m_t[i]) for i in ids]
    qc = [_dot(q[i], Cm[i]) for i in ids]
    num = [_dot(s[i], v[i]) + w_inter[i] * qc[i] for i in ids]
    den = [jnp.sum(s[i], axis=1, keepdims=True) + w_inter[i] * jnp.sum(q[i] * n_row[i], axis=1, keepdims=True)
           for i in ids]
    hh = [num[i] / jnp.maximum(jnp.abs(den[i]), jnp.exp(-m_t[i])) for i in ids]
    m_new = [x[L - 1:L, :] for x in m_t]
    b_last = [x[L - 1:L, :] for x in b_col]
    wk = [jnp.exp(b_last[i] - b_col[i] + ig_col[i] - m_new[i]) * k[i] for i in ids]
    decay = [jnp.exp(b_last[i] + m_prev[i] - m_new[i]) for i in ids]
    upd = [_dot_tn(wk[i], v[i]) for i in ids]
    for i, (g, h) in enumerate(chains):
        C_ref[g, h] = decay[i] * Cm[i] + upd[i]
        n_ref[g, h:h + 1, :] = decay[i] * n_row[i] + jnp.sum(wk[i], axis=0, keepdims=True)
        m_ref[g, :, h:h + 1] = m_new[i]
        dlt = hh[i] - jnp.mean(hh[i], axis=-1, keepdims=True)
        ln = dlt * lax.rsqrt(jnp.mean(dlt * dlt, axis=-1, keepdims=True) + RMS_EPS)
        hm_ref[g, :, sl[i]] = _sigmoid(zo_ref[g, :, sl[i]]) * ln * gain_ref[:, sl[i]]


def _mlstm(zm, zgate, conv_w, gate_b, m_gain, conv0, C0, n0, m0):
    G, R, _ = zm.shape
    L = math.gcd(R, CHUNK)
    W = M_WIDTH
    gs = 2 if R > SUBLANES else 4
    gbias = jnp.zeros((1, LANES), F32).at[0, :2 * M_HEADS].set(gate_b.astype(F32))
    zspec = lambda blk: pl.BlockSpec((gs, L, W), lambda g, c: (g, c, blk))
    st4 = lambda g, c: (g, 0, 0, 0)
    st3 = lambda g, c: (g, 0, 0)
    return pl.pallas_call(
        functools.partial(_mlstm_body, L=L, gs=gs),
        grid=(G // gs, R // L),
        in_specs=[zspec(0), zspec(1), zspec(2), zspec(3),
                  pl.BlockSpec((gs, L, LANES), lambda g, c: (g, c, 0)),
                  pl.BlockSpec((M_CONV, 2 * W), lambda g, c: (0, 0)),
                  pl.BlockSpec((1, LANES), lambda g, c: (0, 0)),
                  pl.BlockSpec((1, W), lambda g, c: (0, 0)),
                  pl.BlockSpec((gs, M_CONV - 1, 2 * W), st3),
                  pl.BlockSpec((gs, M_HEADS, M_HEAD_DIM, M_HEAD_DIM), st4),
                  pl.BlockSpec((gs, M_HEADS, M_HEAD_DIM), st3),
                  pl.BlockSpec((gs, 1, M_HEADS), st3)],
        out_specs=[pl.BlockSpec((gs, L, W), lambda g, c: (g, c, 0)),
                   pl.BlockSpec((gs, M_HEADS, M_HEAD_DIM, M_HEAD_DIM), st4),
                   pl.BlockSpec((gs, M_HEADS, M_HEAD_DIM), st3),
                   pl.BlockSpec((gs, 1, M_HEADS), st3)],
        out_shape=[jax.ShapeDtypeStruct((G, R, W), F32),
                   jax.ShapeDtypeStruct((G, M_HEADS, M_HEAD_DIM, M_HEAD_DIM), F32),
                   jax.ShapeDtypeStruct((G, M_HEADS, M_HEAD_DIM), F32),
                   jax.ShapeDtypeStruct((G, 1, M_HEADS), F32)],
        scratch_shapes=[pltpu.VMEM((gs, SUBLANES + L, 2 * W), F32)],
        compiler_params=_cparams("arbitrary", "arbitrary"),
        name="mlstm",
    )(zm, zm, zm, zm, zgate, conv_w, gbias, m_gain.reshape(1, W), conv0, C0, n0, m0.reshape(G, 1, M_HEADS))


def _hgrn_body(zq_ref, zf_ref, zi_ref, zgg_ref, lb_ref, gain_ref, S0_ref, og_ref, S_ref, ST, *, nblk, gs):
    Dh = G_HEAD_DIM
    B = SUBLANES
    c = pl.program_id(1)
    chains = [(g, h) for g in range(gs) for h in range(G_HEADS)]

    @pl.when(c == 0)
    def _():
        for i, (g, h) in enumerate(chains):
            ST[i] = S0_ref[g, h].T

    rowi = lax.broadcasted_iota(jnp.int32, (B, Dh), 0)

    def blk(bi, carry):
        r0 = pl.multiple_of(bi * B, B)
        rows = pl.ds(r0, B)
        ids = range(len(chains))
        sl = [slice(h * Dh, (h + 1) * Dh) for _, h in chains]
        gi = [g for g, _ in chains]
        f = [lb_ref[:, sl[i]] + (1.0 - lb_ref[:, sl[i]]) * _sigmoid(zf_ref[gi[i], rows, sl[i]]) for i in ids]
        kk = [1.0 - x for x in f]
        bc = [_cumsum_rows(jnp.log(x)) for x in f]
        btot = [x[B - 1:B, :] for x in bc]
        qh = [_silu(zq_ref[gi[i], rows, sl[i]]) * (Dh ** -0.5) for i in ids]
        v = [zi_ref[gi[i], rows, sl[i]] for i in ids]
        st = [ST[i] for i in ids]
        o = [_dot_nt(qh[i] * jnp.exp(bc[i]), st[i]) for i in ids]
        upd = [_dot_tn(v[i], kk[i] * jnp.exp(btot[i] - bc[i])) for i in ids]
        for s in range(B):
            p = [jnp.exp(jnp.where(rowi >= s, bc[i] - bc[i][s:s + 1, :], NEG_BIG)) * qh[i] * kk[i][s:s + 1, :]
                 for i in ids]
            o = [o[i] + jnp.sum(p[i], axis=-1, keepdims=True) * v[i][s:s + 1, :] for i in ids]
        for i in ids:
            ST[i] = st[i] * jnp.exp(btot[i]) + upd[i]
            rms = o[i] * lax.rsqrt(jnp.mean(o[i] * o[i], axis=-1, keepdims=True) + RMS_EPS)
            og_ref[gi[i], rows, sl[i]] = rms * gain_ref[:, sl[i]] * _silu(zgg_ref[gi[i], rows, sl[i]])
        return carry

    lax.fori_loop(0, nblk, blk, 0)

    @pl.when(c == pl.num_programs(1) - 1)
    def _():
        for i, (g, h) in enumerate(chains):
            S_ref[g, h] = ST[i].T


def _hgrn(zg, lb, g_gain, S0):
    G, R, _ = zg.shape
    W = G_WIDTH
    gs = 4
    Rc = min(R, 128)
    zspec = lambda blk: pl.BlockSpec((gs, Rc, W), lambda g, c: (g, c, blk))
    st4 = lambda g, c: (g, 0, 0, 0)
    return pl.pallas_call(
        functools.partial(_hgrn_body, nblk=Rc // SUBLANES, gs=gs),
        grid=(G // gs, R // Rc),
        in_specs=[zspec(0), zspec(1), zspec(2), zspec(3),
                  pl.BlockSpec((1, W), lambda g, c: (0, 0)),
                  pl.BlockSpec((1, W), lambda g, c: (0, 0)),
                  pl.BlockSpec((gs, G_HEADS, G_HEAD_DIM, G_HEAD_DIM), st4)],
        out_specs=[pl.BlockSpec((gs, Rc, W), lambda g, c: (g, c, 0)),
                   pl.BlockSpec((gs, G_HEADS, G_HEAD_DIM, G_HEAD_DIM), st4)],
        out_shape=[jax.ShapeDtypeStruct((G, R, W), F32),
                   jax.ShapeDtypeStruct((G, G_HEADS, G_HEAD_DIM, G_HEAD_DIM), F32)],
        scratch_shapes=[pltpu.VMEM((gs * G_HEADS, G_HEAD_DIM, G_HEAD_DIM), F32)],
        compiler_params=_cparams("arbitrary", "arbitrary"),
        name="hgrn2",
    )(zg, zg, zg, zg, lb.reshape(1, W), g_gain.reshape(1, W), S0)


def _rprep_body(x_ref, g_ref, sc_ref, sh_ref, shift0_ref, mu_ref, *refs, gb, rb):
    outs = refs[:6]
    hlast_ref, hbuf = refs[6:]
    B = SUBLANES
    n, tok, seq = _strips(gb, rb)
    S = hbuf.shape[1] - B
    whole_sequences = gb > 1

    if not whole_sequences:
        @pl.when(pl.program_id(1) == 0)
        def _():
            hbuf[:, B - 1:B, :] = shift0_ref[...]

    def strip(i, carry):
        if whole_sequences:
            hbuf[:, B - 1:B, :] = shift0_ref[seq(i)]
        h = _rms(x_ref[tok(i)], g_ref[...]) * (1.0 + sc_ref[seq(i)]) + sh_ref[seq(i)]
        hbuf[:, B:B + S, :] = h
        xx = hbuf[:, B - 1:B - 1 + S, :] - h
        last = h[:, S - 1:S, :]
        if whole_sequences:
            hlast_ref[seq(i)] = last
        else:
            hbuf[:, B - 1:B, :] = last
        for m in range(6):
            outs[m][tok(i)] = (h + xx * mu_ref[m:m + 1, :]).astype(BF16)
        return carry

    lax.fori_loop(0, n, strip, 0, unroll=4)
    if not whole_sequences:
        hlast_ref[...] = hbuf[:, B - 1:B, :]


def _rwkv_prep(x, g, mod, i_scale, i_shift, shift0, mu, tm):
    G, R, D = x.shape
    gb, rb, nrb = _row_blocking(G, R, tm)
    tok = pl.BlockSpec((gb, rb, D), lambda a, b: (a, b, 0))
    one = pl.BlockSpec((gb, 1, D), lambda a, b: (a, 0, 0))
    strip_shape = (1, SUBLANES + STRIP_ROWS, D) if gb == 1 else (STRIP_ROWS // rb, SUBLANES + rb, D)
    res = pl.pallas_call(
        functools.partial(_rprep_body, gb=gb, rb=rb),
        grid=(G // gb, nrb),
        in_specs=[tok,
                  pl.BlockSpec((1, 1, D), lambda a, b: (0, 0, 0)),
                  pl.BlockSpec((gb, 1, D), lambda a, b: (a, 0, i_scale)),
                  pl.BlockSpec((gb, 1, D), lambda a, b: (a, 0, i_shift)),
                  one,
                  pl.BlockSpec((6, D), lambda a, b: (0, 0))],
        out_specs=[tok] * 6 + [one],
        out_shape=[jax.ShapeDtypeStruct((G, R, D), BF16)] * 6 + [jax.ShapeDtypeStruct((G, 1, D), F32)],
        scratch_shapes=[pltpu.VMEM(strip_shape, F32)],
        compiler_params=_cparams("arbitrary", "arbitrary"),
        name="rwkv_prep",
    )(x, g.reshape(1, 1, D), mod, mod, shift0.reshape(G, 1, D), mu)
    return res[:6], res[6]


def _lora_body(x_ref, w1_ref, w2_ref, b_ref, o_ref, *, act, post):
    t = _dot(x_ref[...], w1_ref[...])
    if act == "tanh":
        t = jnp.tanh(t)
    elif act == "sigmoid":
        t = _sigmoid(t)
    y = _dot(t, w2_ref[...])
    if post == "log_decay":
        y = -jnp.exp(-_softplus(-(b_ref[...] + y)) - 0.5)
    elif post == "sigmoid":
        y = _sigmoid(b_ref[...] + y)
    o_ref[...] = y


def _lora(x, w1, w2, act, tm, post=None, bias=None):
    G, R, K = x.shape
    M = G * R
    r = w1.shape[1]
    rp = -(-r // LANES) * LANES
    w1p = jnp.zeros((K, rp), F32).at[:, :r].set(w1)
    w2p = jnp.zeros((rp, w2.shape[1]), F32).at[:r, :].set(w2)
    N = w2.shape[1]
    b = jnp.zeros((1, N), F32) if bias is None else bias.reshape(1, N).astype(F32)
    out = pl.pallas_call(
        functools.partial(_lora_body, act=act, post=post),
        grid=(M // tm,),
        in_specs=[pl.BlockSpec((tm, K), lambda i: (i, 0)),
                  pl.BlockSpec((K, rp), lambda i: (0, 0)),
                  pl.BlockSpec((rp, N), lambda i: (0, 0)),
                  pl.BlockSpec((1, N), lambda i: (0, 0))],
        out_specs=pl.BlockSpec((tm, N), lambda i: (i, 0)),
        out_shape=jax.ShapeDtypeStruct((M, N), F32),
        compiler_params=_cparams("arbitrary"),
        name="lora_" + str(act),
    )(x.reshape(M, K), w1p, w2p, b)
    return out.reshape(G, R, N)


def _head_mask(rows, cols, rper, cper):
    r = lax.broadcasted_iota(jnp.int32, (rows, cols), 0)
    c = lax.broadcasted_iota(jnp.int32, (rows, cols), 1)
    return (r // rper) == (c // cper)


def _bd(y, mask01):
    if y.shape[0] % (2 * SUBLANES) == 0:
        return jnp.concatenate([y.astype(BF16)] * R_GROUP, axis=0) * mask01
    return jnp.concatenate([y] * R_GROUP, axis=0).astype(BF16) * mask01


def _segsums(xs, ones_bd):
    c = xs[0].shape[0]
    if c % (2 * SUBLANES) == 0:
        lhs = jnp.concatenate([x.astype(BF16) for x in xs], axis=0)
    else:
        lhs = jnp.concatenate(xs, axis=0).astype(BF16)
    res = jnp.dot(lhs, ones_bd, preferred_element_type=F32)
    return [res[i * c:(i + 1) * c] for i in range(len(xs))]


def _rwkv_body(r_ref, k_ref, v_ref, lw_ref, a_ref, g_ref, kkp_ref, kap_ref, rk_ref,
               lnw_ref, lnb_ref, S0_ref, y_ref, S_ref, Sbd, *, c, nchunk, ng, gs):
    N = R_HEAD_DIM
    GW = R_GROUP_W
    j = pl.program_id(2)
    m_state = _head_mask(GW, GW, N, N)
    chains = [(q, gi) for q in range(gs) for gi in range(ng)]
    ids = range(len(chains))
    lanes = [slice(gi * GW, (gi + 1) * GW) for _, gi in chains]
    seqs = [q for q, _ in chains]

    @pl.when(j == 0)
    def _():
        for i, (q, gi) in enumerate(chains):
            s0 = S0_ref[q, gi * R_GROUP:(gi + 1) * R_GROUP].reshape(GW, N)
            Sbd[i] = jnp.where(m_state, jnp.concatenate([s0] * R_GROUP, axis=1), 0.0)

    ones_bd = jnp.where(m_state, 1.0, 0.0).astype(BF16)
    m_vec = jnp.where(_head_mask(R_GROUP * c, GW, c, N), 1.0, 0.0).astype(BF16)
    m_mat = jnp.where(_head_mask(R_GROUP * c, R_GROUP * c, c, c), 1.0, 0.0).astype(BF16)
    t_idx = lax.broadcasted_iota(jnp.int32, (2 * c, R_GROUP * c), 0)
    s_idx = lax.broadcasted_iota(jnp.int32, (2 * c, R_GROUP * c), 1) % c
    causal = jnp.where(t_idx < c, jnp.where(s_idx < t_idx, 1.0, 0.0), jnp.where(s_idx <= t_idx - c, 1.0, 0.0))
    nt = (((1,), (1,)), ((), ()))
    mm = lambda a, b: jnp.dot(a.astype(BF16), b, preferred_element_type=F32)

    eye = jnp.where(lax.broadcasted_iota(jnp.int32, (c, R_GROUP * c), 1) % c
                    == lax.broadcasted_iota(jnp.int32, (c, R_GROUP * c), 0), 1.0, 0.0)
    keys = ("ar", "bdb", "bdk", "bdv", "vb", "gam", "tail")

    def prepare_steps(ci, out):
        rows = pl.ds(pl.multiple_of(ci * c, c), c)
        ld = lambda ref: [ref[seqs[i], rows, lanes[i]] for i in ids]
        r, k, a = ld(r_ref), ld(k_ref), ld(a_ref)
        kk = [k[i] * kkp_ref[:, lanes[i]] for i in ids]
        k2 = [k[i] * (1.0 + (a[i] - 1.0) * kap_ref[:, lanes[i]]) for i in ids]
        sums = _segsums([kk[i] * kk[i] for i in ids] + [r[i] * k2[i] * rk_ref[:, lanes[i]] for i in ids], ones_bd)
        yield
        for i in ids:
            v, lw, g = v_ref[seqs[i], rows, lanes[i]], lw_ref[seqs[i], rows, lanes[i]], g_ref[seqs[i], rows, lanes[i]]
            cw = _cumsum_rows(lw)
            cwl = cw[c - 1:c, :]
            kn = kk[i] / jnp.maximum(jnp.sqrt(sums[i]), 1e-12)
            bb = kn * a[i]
            e_out = jnp.exp(-cw)
            e_end = jnp.exp(cwl - cw)
            out["ar"].append(jnp.concatenate([-kn * jnp.exp(cw - lw), r[i] * jnp.exp(cw)], axis=0).astype(BF16))
            out["gam"].append(jnp.exp(cwl))
            out["tail"].append(jnp.concatenate([sums[len(chains) + i] * v, g], axis=0))
            yield
            out["bdb"].append(_bd(bb * e_out, m_vec))
            out["bdk"].append(_bd(k2[i] * e_out, m_vec))
            out["bdv"].append(_bd(v, m_vec))
            out["vb"].append(jnp.concatenate([v, bb * e_end, k2[i] * e_end], axis=0))
            yield

    def prepare(ci):
        out = {key: [] for key in keys}
        for _ in prepare_steps(ci, out):
            pass
        return out

    def chain(ci, p, side_work=None):
        def tick():
            if side_work is not None:
                next(side_work, None)

        rows = pl.ds(pl.multiple_of(ci * c, c), c)
        ar, bdv = p["ar"], p["bdv"]
        sbd = [Sbd[i] for i in ids]
        pb = [lax.dot_general(ar[i], p["bdb"][i], nt, preferred_element_type=F32) * causal for i in ids]
        pk = [lax.dot_general(ar[i], p["bdk"][i], nt, preferred_element_type=F32) * causal for i in ids]
        tick()
        s0p = [lax.dot_general(ar[i], sbd[i].astype(BF16), nt, preferred_element_type=F32) for i in ids]
        kv = [mm(pk[i], bdv[i]) for i in ids]
        w = [s0p[i][:c] + kv[i][:c] for i in ids]
        tick()
        pw = [pb[i][:c] for i in ids]
        tinv = [eye + pw[i] for i in ids]
        levels = c.bit_length() - 1
        bdp = [_bd(pw[i], m_mat) for i in ids]
        pw = [mm(pw[i], bdp[i]) for i in ids]
        tick()
        for lvl in range(1, levels):
            bdp = [_bd(pw[i], m_mat) for i in ids]
            if lvl < levels - 1:
                res = [mm(jnp.concatenate([tinv[i], pw[i]], axis=0), bdp[i]) for i in ids]
                tinv = [tinv[i] + res[i][:c] for i in ids]
                pw = [res[i][c:] for i in ids]
            else:
                tinv = [tinv[i] + mm(tinv[i], bdp[i]) for i in ids]
            tick()
        u = [mm(tinv[i], _bd(w[i], m_vec)) for i in ids]
        tick()
        y = [s0p[i][c:] + mm(pb[i][c:], _bd(u[i], m_vec)) + kv[i][c:] for i in ids]
        tick()
        upd = [_dot_tn(jnp.concatenate([u[i], p["vb"][i][:c]], axis=0), p["vb"][i][c:]) for i in ids]
        for i in ids:
            Sbd[i] = sbd[i] * p["gam"][i] + jnp.where(m_state, upd[i], 0.0)
        tick()
        ysum = _segsums(y, ones_bd)
        dlt = [y[i] - ysum[i] * (1.0 / N) for i in ids]
        vsum = _segsums([dlt[i] * dlt[i] for i in ids], ones_bd)
        for i in ids:
            yn = dlt[i] * lax.rsqrt(vsum[i] * (1.0 / N) + LN_X_EPS) * lnw_ref[:, lanes[i]] + lnb_ref[:, lanes[i]]
            y_ref[seqs[i], rows, lanes[i]] = ((yn + p["tail"][i][:c]) * p["tail"][i][c:]).astype(y_ref.dtype)
        if side_work is not None:
            for _ in side_work:
                pass

    if nchunk == 1:
        chain(0, prepare(0))
    else:
        def body(ci, p):
            nxt = {key: [] for key in keys}
            chain(ci, p, prepare_steps(jnp.minimum(ci + 1, nchunk - 1), nxt))
            return nxt

        lax.fori_loop(0, nchunk, body, prepare(0))

    @pl.when(j == pl.num_programs(2) - 1)
    def _():
        for i, (q, gi) in enumerate(chains):
            sbd = Sbd[i]
            for h in range(R_GROUP):
                S_ref[q, gi * R_GROUP + h] = sbd[h * N:(h + 1) * N, h * N:(h + 1) * N]


def _rwkv(r, k, v, lw, a, g, kkp, kap, rk, lnw, lnb, S0):
    G, R, D = r.shape
    c = math.gcd(R, CHUNK)
    GW = R_GROUP_W
    if R > SUBLANES:
        gs, ng, Rc = 1, 4, min(R, 512)
    else:
        gs, ng, Rc = 2, D // GW, R
    bw = ng * GW
    tok = pl.BlockSpec((gs, Rc, bw), lambda a_, b, j: (a_, j, b))
    par = pl.BlockSpec((1, bw), lambda a_, b, j: (0, b))
    st = pl.BlockSpec((gs, ng * R_GROUP, R_HEAD_DIM, R_HEAD_DIM), lambda a_, b, j: (a_, b, 0, 0))
    row = lambda p: p.reshape(1, D).astype(F32)
    return pl.pallas_call(
        functools.partial(_rwkv_body, c=c, nchunk=Rc // c, ng=ng, gs=gs),
        grid=(G // gs, D // bw, R // Rc),
        in_specs=[tok] * 6 + [par] * 5 + [st],
        out_specs=[tok, st],
        out_shape=[jax.ShapeDtypeStruct((G, R, D), BF16),
                   jax.ShapeDtypeStruct((G, R_HEADS, R_HEAD_DIM, R_HEAD_DIM), F32)],
        scratch_shapes=[pltpu.VMEM((gs * ng, GW, GW), F32)],
        compiler_params=_cparams("arbitrary", "arbitrary", "arbitrary"),
        name="rwkv7",
    )(r, k, v, lw, a, g, row(kkp), row(kap), row(rk), row(lnw), row(lnb), S0)


def _trunk(x, mod, m_C, m_n, m_m, m_conv, g_S, r_S, r_shift, p, lbs, tm):
    G, R, D = x.shape
    md = mod[0]
    h = _modnorm(x, p["norm_mix"][0], md, 1, 0, tm)
    w_in_t = jnp.swapaxes(p["ab_w_in"], 1, 2)
    zm = _mm(h, w_in_t, 0, n_cols=4 * M_WIDTH, tm=tm, tn=1024, name="mm_in_m", w_transposed=True)
    zgate = _mm(h, w_in_t, 0, n_cols=LANES, tm=tm, tn=LANES, name="mm_in_gate", w_transposed=True, col0=4 * M_WIDTH)
    zg = _mm(h, w_in_t, 0, n_cols=4 * G_WIDTH, tm=tm, tn=1024, name="mm_in_g", w_transposed=True,
             col0=4 * M_WIDTH + 2 * M_HEADS)
    hm, C, n, m = _mlstm(zm, zgate, p["m_conv_w"][0], p["ab_gate_b"][0], p["m_norm"][0],
                         m_conv[0], m_C[0], m_n[0], m_m[0])
    conv_new = zm[:, R - (M_CONV - 1):, :2 * M_WIDTH]
    og, S = _hgrn(zg, lbs[0], p["g_norm"][0], g_S[0])
    x = _mm_residual([hm, og], p["ab_w_out"], 0, x, md, 2, tm=tm, tn=512, name="mm_out0")
    h = _modnorm(x, p["norm_ffn"][0], md, 4, 3, tm)
    act = _mm(h, p["ffn_w1"], 0, n_cols=4 * D, tm=tm, tn=1024, act="relu2", out_dtype=BF16, name="ffn_up")
    x = _mm_residual([act], p["ffn_w2_bf16"], 0, x, md, 5, tm=256, tn=1024, name="ffn_down")
    md = mod[1]
    (xr, xw, xk, xv, xa, xg), shift_new = _rwkv_prep(x, p["norm_mix"][1], md, 1, 0, r_shift[0], p["r_mu"][0],
                                                     min(tm, 256))
    r = _mm(xr, p["r_wr"], 0, n_cols=D, tm=tm, tn=1024, name="mm_r")
    k = _mm(xk, p["r_wk"], 0, n_cols=D, tm=tm, tn=1024, name="mm_k")
    v = _mm(xv, p["r_wv"], 0, n_cols=D, tm=tm, tn=1024, name="mm_v")
    lw = _lora(xw, p["r_w1"][0], p["r_w2"][0], "tanh", min(tm, 512), post="log_decay", bias=p["r_w0"][0])
    aa = _lora(xa, p["r_a1"][0], p["r_a2"][0], None, min(tm, 512), post="sigmoid", bias=p["r_a0"][0])
    gg = _lora(xg, p["r_g1"][0], p["r_g2"][0], "sigmoid", min(tm, 512))
    yg, rS = _rwkv(r, k, v, lw, aa, gg, p["r_kk"][0], p["r_ka"][0], p["r_rk"][0].reshape(-1), p["r_lnw"][0],
                   p["r_lnb"][0], r_S[0])
    x = _mm_residual([yg], p["r_wo"], 0, x, md, 2, tm=tm, tn=1024, name="mm_out1")
    h = _modnorm(x, p["norm_ffn"][1], md, 4, 3, tm)
    act = _mm(h, p["ffn_w1"], 1, n_cols=4 * D, tm=tm, tn=1024, act="relu2", out_dtype=BF16, name="ffn_up")
    x = _mm_residual([act], p["ffn_w2_bf16"], 1, x, md, 5, tm=256, tn=1024, name="ffn_down")
    y = _rmsnorm(x, p["final_norm"], min(tm, 512))
    return y, (C[None], n[None], m.reshape(1, G, M_HEADS), conv_new[None], S[None], rS[None],
               shift_new.reshape(1, G, D))


def kernel(x_prompt, x_sample, c_prompt, c_sample, state_mlstm_C, state_mlstm_n, state_mlstm_m, state_mlstm_conv, state_hgrn_S, state_rwkv_S, state_rwkv_shift, mod_w, mod_b, norm_mix, norm_ffn, ffn_w1, ffn_w2, final_norm, ab_w_in, ab_gate_b, m_conv_w, m_norm, g_lb, g_norm, ab_w_out, r_mu, r_w0, r_w1, r_w2, r_a0, r_a1, r_a2, r_g1, r_g2, r_kk, r_ka, r_rk, r_wr, r_wk, r_wv, r_wo, r_lnw, r_lnb):
    p = dict(norm_mix=norm_mix, norm_ffn=norm_ffn, ffn_w1=ffn_w1, ffn_w2=ffn_w2, final_norm=final_norm,
             ab_w_in=ab_w_in, ab_gate_b=ab_gate_b, m_conv_w=m_conv_w, m_norm=m_norm, g_norm=g_norm,
             ab_w_out=ab_w_out, r_mu=r_mu, r_w0=r_w0, r_w1=r_w1, r_w2=r_w2, r_a0=r_a0, r_a1=r_a1, r_a2=r_a2,
             r_g1=r_g1, r_g2=r_g2, r_kk=r_kk, r_ka=r_ka, r_rk=r_rk, r_wr=r_wr, r_wk=r_wk, r_wv=r_wv,
             r_wo=r_wo, r_lnw=r_lnw, r_lnb=r_lnb)
    p["ffn_w2_bf16"] = ffn_w2.astype(BF16)
    B, T, D = x_prompt.shape
    Bs = x_sample.shape[0]
    depth = mod_w.shape[0]
    pad = (-(Bs + B)) % SUBLANES
    c_all = jnp.concatenate([c_sample, c_prompt, jnp.zeros((pad, D), F32)], axis=0)
    mod = _modulation(c_all, mod_w, mod_b)
    mod_s = mod[:, :Bs].reshape(depth, Bs, 1, N_MOD * D)
    mod_p = mod[:, Bs:Bs + B].reshape(depth, B, 1, N_MOD * D)
    lbs = jnp.cumsum(jax.nn.softmax(g_lb.astype(F32), axis=0), axis=0)

    n_even = state_mlstm_C.shape[0]
    n_odd = state_rwkv_S.shape[0]
    z = lambda *s: jnp.zeros(s, F32)
    yp, sp = _trunk(x_prompt, mod_p,
                    z(n_even, B, M_HEADS, M_HEAD_DIM, M_HEAD_DIM), z(n_even, B, M_HEADS, M_HEAD_DIM),
                    z(n_even, B, M_HEADS), z(n_even, B, M_CONV - 1, 2 * M_WIDTH),
                    z(n_even, B, G_HEADS, G_HEAD_DIM, G_HEAD_DIM), z(n_odd, B, R_HEADS, R_HEAD_DIM, R_HEAD_DIM),
                    z(n_odd, B, D), p, lbs, 1024)
    ys, ss = _trunk(x_sample, mod_s, state_mlstm_C, state_mlstm_n, state_mlstm_m, state_mlstm_conv,
                    state_hgrn_S, state_rwkv_S, state_rwkv_shift, p, lbs, 1024)
    return (yp, ys) + tuple(sp) + tuple(ss)
```

```python
import functools
import math

import jax
import jax.numpy as jnp
from jax import lax
from jax.experimental import pallas as pl
from jax.experimental.pallas import tpu as pltpu

F32 = jnp.float32
BF16 = jnp.bfloat16

D_MODEL = 2048
M_HEADS = 4
M_HEAD_DIM = 256
M_WIDTH = M_HEADS * M_HEAD_DIM
M_CONV = 4
G_HEADS = 8
G_HEAD_DIM = 128
G_WIDTH = G_HEADS * G_HEAD_DIM
R_HEAD_DIM = 64
R_HEADS = D_MODEL // R_HEAD_DIM
R_GROUP = 4
R_GROUP_W = R_GROUP * R_HEAD_DIM
N_MOD = 6
RMS_EPS = 1e-6
LN_X_EPS = 64e-5
CHUNK = 64
NEG_BIG = -1e30

V7X_VMEM_LIMIT_BYTES = 56 * 1024 * 1024
SUBLANES = 8
LANES = 128


def _cparams(*sem):
    return pltpu.CompilerParams(dimension_semantics=sem, vmem_limit_bytes=V7X_VMEM_LIMIT_BYTES)


def _sigmoid(x):
    return 1.0 / (1.0 + jnp.exp(-x))


def _silu(x):
    return x * _sigmoid(x)


def _softplus(x):
    return jnp.maximum(x, 0.0) + jnp.log1p(jnp.exp(-jnp.abs(x)))


def _dot(a, b):
    return jnp.dot(a.astype(BF16), b.astype(BF16), preferred_element_type=F32)


def _dot_nt(a, b):
    return lax.dot_general(a.astype(BF16), b.astype(BF16), (((1,), (1,)), ((), ())), preferred_element_type=F32)


def _dot_tn(a, b):
    return lax.dot_general(a.astype(BF16), b.astype(BF16), (((0,), (0,)), ((), ())), preferred_element_type=F32)


def _cumsum_rows(x):
    n = x.shape[0]
    row = lax.broadcasted_iota(jnp.int32, x.shape, 0)
    s = 1
    while s < n:
        x = x + jnp.where(row >= s, pltpu.roll(x, s, axis=0), 0.0)
        s *= 2
    return x


def _row_blocking(G, R, tm):
    if R >= tm:
        assert R % tm == 0
        return 1, tm, R // tm
    assert tm % R == 0 and G % (tm // R) == 0
    return tm // R, R, 1


def _mod_body(c_ref, w_ref, b_ref, o_ref):
    sc = _silu(c_ref[...])
    o_ref[...] = _dot(sc, w_ref[...]) + b_ref[...]


def _modulation(c_all, mod_w, mod_b):
    L, K, N = mod_w.shape
    Mc = c_all.shape[0]
    tn = 1024
    return pl.pallas_call(
        _mod_body,
        grid=(L, N // tn),
        in_specs=[pl.BlockSpec((Mc, K), lambda l, j: (0, 0)),
                  pl.BlockSpec((None, K, tn), lambda l, j: (l, 0, j)),
                  pl.BlockSpec((None, 1, tn), lambda l, j: (l, 0, j))],
        out_specs=pl.BlockSpec((None, Mc, tn), lambda l, j: (l, 0, j)),
        out_shape=jax.ShapeDtypeStruct((L, Mc, N), F32),
        compiler_params=_cparams("arbitrary", "arbitrary"),
        name="modulation",
    )(c_all, mod_w, mod_b.reshape(L, 1, N))


def _rms(x, g):
    ms = jnp.mean(x * x, axis=-1, keepdims=True)
    return x * lax.rsqrt(ms + RMS_EPS) * g


STRIP_ROWS = 2 * SUBLANES


def _strips(gb, rb):
    if gb == 1:
        n = rb // STRIP_ROWS
        tok = lambda i: (slice(None), pl.ds(pl.multiple_of(i * STRIP_ROWS, STRIP_ROWS), STRIP_ROWS), slice(None))
        seq = lambda i: (slice(None), slice(None), slice(None))
    else:
        assert STRIP_ROWS % rb == 0
        per = STRIP_ROWS // rb
        n = gb // per
        tok = seq = lambda i: (pl.ds(i * per, per), slice(None), slice(None))
    return n, tok, seq


def _modnorm_body(x_ref, g_ref, sc_ref, sh_ref, o_ref, *, gb, rb):
    n, tok, seq = _strips(gb, rb)

    def strip(i, carry):
        y = _rms(x_ref[tok(i)], g_ref[...])
        o_ref[tok(i)] = (y * (1.0 + sc_ref[seq(i)]) + sh_ref[seq(i)]).astype(o_ref.dtype)
        return carry

    lax.fori_loop(0, n, strip, 0, unroll=4)


def _modnorm(x, g, mod, i_scale, i_shift, tm):
    G, R, D = x.shape
    gb, rb, nrb = _row_blocking(G, R, tm)
    return pl.pallas_call(
        functools.partial(_modnorm_body, gb=gb, rb=rb),
        grid=(G // gb, nrb),
        in_specs=[pl.BlockSpec((gb, rb, D), lambda a, b: (a, b, 0)),
                  pl.BlockSpec((1, 1, D), lambda a, b: (0, 0, 0)),
                  pl.BlockSpec((gb, 1, D), lambda a, b: (a, 0, i_scale)),
                  pl.BlockSpec((gb, 1, D), lambda a, b: (a, 0, i_shift))],
        out_specs=pl.BlockSpec((gb, rb, D), lambda a, b: (a, b, 0)),
        out_shape=jax.ShapeDtypeStruct((G, R, D), BF16),
        compiler_params=_cparams("arbitrary", "arbitrary"),
        name="modnorm",
    )(x, g.reshape(1, 1, D), mod, mod)


def _rmsnorm_body(x_ref, g_ref, o_ref, *, gb, rb):
    n, tok, _ = _strips(gb, rb)

    def strip(i, carry):
        o_ref[tok(i)] = _rms(x_ref[tok(i)], g_ref[...])
        return carry

    lax.fori_loop(0, n, strip, 0, unroll=4)


def _rmsnorm(x, g, tm):
    G, R, D = x.shape
    gb, rb, nrb = _row_blocking(G, R, tm)
    return pl.pallas_call(
        functools.partial(_rmsnorm_body, gb=gb, rb=rb),
        grid=(G // gb, nrb),
        in_specs=[pl.BlockSpec((gb, rb, D), lambda a, b: (a, b, 0)),
                  pl.BlockSpec((1, 1, D), lambda a, b: (0, 0, 0))],
        out_specs=pl.BlockSpec((gb, rb, D), lambda a, b: (a, b, 0)),
        out_shape=jax.ShapeDtypeStruct((G, R, D), F32),
        compiler_params=_cparams("arbitrary", "arbitrary"),
        name="final_norm",
    )(x, g.reshape(1, 1, D))


def _mm_body(a_ref, w_ref, o_ref, wbf, *, act, w_transposed):
    @pl.when(pl.program_id(1) == 0)
    def _():
        wbf[...] = (w_ref[0].T if w_transposed else w_ref[...]).astype(BF16)

    acc = jnp.dot(a_ref[...].astype(BF16), wbf[...], preferred_element_type=F32)
    if act == "relu2":
        acc = jnp.square(jnp.maximum(acc, 0.0))
    o_ref[...] = acc.astype(o_ref.dtype)


def _mm(a, w, layer, *, n_cols, tm, tn, act=None, out_dtype=F32, name="mm", w_transposed=False, col0=0):
    G, R, K = a.shape
    M = G * R
    assert M % tm == 0 and n_cols % tn == 0 and w.shape[2 if w_transposed else 1] == K
    if w_transposed:
        assert col0 % SUBLANES == 0
        w_spec = pl.BlockSpec((pl.Element(1), pl.Element(tn), pl.Element(K)),
                              lambda j, i: (layer, pl.multiple_of(col0 + j * tn, SUBLANES), 0))
    else:
        assert col0 == 0
        w_spec = pl.BlockSpec((None, K, tn), lambda j, i: (layer, 0, j))
    out = pl.pallas_call(
        functools.partial(_mm_body, act=act, w_transposed=w_transposed),
        grid=(n_cols // tn, M // tm),
        in_specs=[pl.BlockSpec((tm, K), lambda j, i: (i, 0)), w_spec],
        out_specs=pl.BlockSpec((tm, tn), lambda j, i: (i, j)),
        out_shape=jax.ShapeDtypeStruct((M, n_cols), out_dtype),
        scratch_shapes=[pltpu.VMEM((K, tn), BF16)],
        compiler_params=_cparams("arbitrary", "arbitrary"),
        name=name,
    )(a.reshape(M, K), w)
    return out.reshape(G, R, n_cols)


def _mm_res_body(*refs, n_lhs, gb, rb, cast_w):
    a_refs = refs[:n_lhs]
    w_ref, x_ref, gt_ref, o_ref = refs[n_lhs:n_lhs + 4]
    if cast_w:
        wbf = refs[n_lhs + 4]

        @pl.when(pl.program_id(1) == 0)
        def _():
            wbf[...] = w_ref[...].astype(BF16)
    else:
        wbf = w_ref

    acc = None
    k0 = 0
    for a_ref in a_refs:
        kk = a_ref.shape[-1]
        part = jnp.dot(a_ref[...].astype(BF16), wbf[k0:k0 + kk, :], preferred_element_type=F32)
        acc = part if acc is None else acc + part
        k0 += kk
    tn = acc.shape[-1]
    o_ref[...] = x_ref[...] + gt_ref[...] * acc.reshape(gb, rb, tn)


def _mm_residual(a_list, w, layer, x, mod, i_gate, *, tm, tn, name="mm_res"):
    G, R, N = x.shape
    M = G * R
    K = w.shape[1]
    assert sum(a.shape[-1] for a in a_list) == K and w.shape[2] == N
    gb, rb, nrb = _row_blocking(G, R, tm)
    ntn = N // tn
    cast_w = w.dtype != BF16

    def xmap(j, i):
        return (i // nrb, i % nrb, j)

    return pl.pallas_call(
        functools.partial(_mm_res_body, n_lhs=len(a_list), gb=gb, rb=rb, cast_w=cast_w),
        grid=(ntn, M // tm),
        in_specs=[pl.BlockSpec((tm, a.shape[-1]), lambda j, i: (i, 0)) for a in a_list]
        + [pl.BlockSpec((None, K, tn), lambda j, i: (layer, 0, j)),
           pl.BlockSpec((gb, rb, tn), xmap),
           pl.BlockSpec((gb, 1, tn), lambda j, i: (i // nrb, 0, i_gate * ntn + j))],
        out_specs=pl.BlockSpec((gb, rb, tn), xmap),
        out_shape=jax.ShapeDtypeStruct((G, R, N), F32),
        scratch_shapes=[pltpu.VMEM((K, tn), BF16)] if cast_w else [],
        compiler_params=_cparams("arbitrary", "arbitrary"),
        name=name,
    )(*[a.reshape(M, a.shape[-1]) for a in a_list], w, x, mod)


def _mlstm_body(zq_ref, zk_ref, zv_ref, zo_ref, zg_ref, cw_ref, gbias_ref, gain_ref, conv0_ref,
                C0_ref, n0_ref, m0_ref, hm_ref, C_ref, n_ref, m_ref, ext, *, L, gs):
    W = M_WIDTH
    Dh = M_HEAD_DIM
    H = M_HEADS
    keep = M_CONV - 1
    c = pl.program_id(1)

    @pl.when(c == 0)
    def _():
        C_ref[...] = C0_ref[...]
        n_ref[...] = n0_ref[...]
        m_ref[...] = m0_ref[...]
        ext[:, SUBLANES - keep:SUBLANES, :] = conv0_ref[...]

    ext[:, SUBLANES:SUBLANES + L, 0:W] = zq_ref[...]
    ext[:, SUBLANES:SUBLANES + L, W:2 * W] = zk_ref[...]
    row = lax.broadcasted_iota(jnp.int32, (L, L), 0)
    col = lax.broadcasted_iota(jnp.int32, (L, L), 1)
    eye = row == col
    tri = col <= row
    chains = [(g, h) for g in range(gs) for h in range(H)]
    ids = range(len(chains))
    qk, gates = [], []
    for g in range(gs):
        conv = ext[g, SUBLANES:SUBLANES + L, :] * cw_ref[keep:keep + 1, :]
        for j in range(keep):
            conv = conv + ext[g, SUBLANES - keep + j:SUBLANES - keep + j + L, :] * cw_ref[j:j + 1, :]
        qk.append(_silu(conv))
        gates.append(zg_ref[g] + gbias_ref[...])
    ext[:, SUBLANES - keep:SUBLANES, :] = ext[:, SUBLANES + L - keep:SUBLANES + L, :]

    sl = [slice(h * Dh, (h + 1) * Dh) for _, h in chains]
    q = [qk[g][:, h * Dh:(h + 1) * Dh] for g, h in chains]
    k = [qk[g][:, W + h * Dh:W + (h + 1) * Dh] * (Dh ** -0.5) for g, h in chains]
    v = [zv_ref[g, :, sl[i]] for i, (g, h) in enumerate(chains)]
    ig_col = [gates[g][:, h:h + 1] for g, h in chains]
    fpre = [gates[g][:, H + h:H + h + 1] for g, h in chains]
    lf_col = [jnp.minimum(x, 0.0) - jnp.log1p(jnp.exp(-jnp.abs(x))) for x in fpre]
    lf_row = [jnp.sum(jnp.where(eye, x, 0.0), axis=0, keepdims=True) for x in lf_col]
    ig_row = [jnp.sum(jnp.where(eye, x, 0.0), axis=0, keepdims=True) for x in ig_col]
    b_col = [jnp.sum(jnp.where(tri, x, 0.0), axis=1, keepdims=True) for x in lf_row]
    b_row = [jnp.sum(jnp.where(row <= col, x, 0.0), axis=0, keepdims=True) for x in lf_col]
    dmat = [jnp.where(tri, b_col[i] - b_row[i] + ig_row[i], NEG_BIG) for i in ids]
    m_prev = [m_ref[g, :, h:h + 1] for g, h in chains]
    inter = [b_col[i] + m_prev[i] for i in ids]
    m_t = [jnp.maximum(inter[i], jnp.max(dmat[i], axis=1, keepdims=True)) for i in ids]
    w_inter = [jnp.exp(inter[i] - m_t[i]) for i in ids]
    Cm = [C_ref[g, h] for g, h in chains]
    n_row = [n_ref[g, h:h + 1, :] for g, h in chains]
    s = [_dot_nt(q[i], k[i]) * jnp.exp(dmat[i] - m_t[i]) for i in ids]
    qc = [_dot(q[i], Cm[i]) for i in ids]
    num = [_dot(s[i], v[i]) + w_inter[i] * qc[i] for i in ids]
    den = [jnp.sum(s[i], axis=1, keepdims=True) + w_inter[i] * jnp.sum(q[i] * n_row[i], axis=1, keepdims=True)
           for i in ids]
    hh = [num[i] / jnp.maximum(jnp.abs(den[i]), jnp.exp(-m_t[i])) for i in ids]
    m_new = [x[L - 1:L, :] for x in m_t]
    b_last = [x[L - 1:L, :] for x in b_col]
    wk = [jnp.exp(b_last[i] - b_col[i] + ig_col[i] - m_new[i]) * k[i] for i in ids]
    decay = [jnp.exp(b_last[i] + m_prev[i] - m_new[i]) for i in ids]
    upd = [_dot_tn(wk[i], v[i]) for i in ids]
    for i, (g, h) in enumerate(chains):
        C_ref[g, h] = decay[i] * Cm[i] + upd[i]
        n_ref[g, h:h + 1, :] = decay[i] * n_row[i] + jnp.sum(wk[i], axis=0, keepdims=True)
        m_ref[g, :, h:h + 1] = m_new[i]
        dlt = hh[i] - jnp.mean(hh[i], axis=-1, keepdims=True)
        ln = dlt * lax.rsqrt(jnp.mean(dlt * dlt, axis=-1, keepdims=True) + RMS_EPS)
        hm_ref[g, :, sl[i]] = _sigmoid(zo_ref[g, :, sl[i]]) * ln * gain_ref[:, sl[i]]


def _mlstm(zm, zgate, conv_w, gate_b, m_gain, conv0, C0, n0, m0):
    G, R, _ = zm.shape
    L = math.gcd(R, CHUNK)
    W = M_WIDTH
    gs = 2 if R > SUBLANES else 4
    gbias = jnp.zeros((1, LANES), F32).at[0, :2 * M_HEADS].set(gate_b.astype(F32))
    zspec = lambda blk: pl.BlockSpec((gs, L, W), lambda g, c: (g, c, blk))
    st4 = lambda g, c: (g, 0, 0, 0)
    st3 = lambda g, c: (g, 0, 0)
    return pl.pallas_call(
        functools.partial(_mlstm_body, L=L, gs=gs),
        grid=(G // gs, R // L),
        in_specs=[zspec(0), zspec(1), zspec(2), zspec(3),
                  pl.BlockSpec((gs, L, LANES), lambda g, c: (g, c, 0)),
                  pl.BlockSpec((M_CONV, 2 * W), lambda g, c: (0, 0)),
                  pl.BlockSpec((1, LANES), lambda g, c: (0, 0)),
                  pl.BlockSpec((1, W), lambda g, c: (0, 0)),
                  pl.BlockSpec((gs, M_CONV - 1, 2 * W), st3),
                  pl.BlockSpec((gs, M_HEADS, M_HEAD_DIM, M_HEAD_DIM), st4),
                  pl.BlockSpec((gs, M_HEADS, M_HEAD_DIM), st3),
                  pl.BlockSpec((gs, 1, M_HEADS), st3)],
        out_specs=[pl.BlockSpec((gs, L, W), lambda g, c: (g, c, 0)),
                   pl.BlockSpec((gs, M_HEADS, M_HEAD_DIM, M_HEAD_DIM), st4),
                   pl.BlockSpec((gs, M_HEADS, M_HEAD_DIM), st3),
                   pl.BlockSpec((gs, 1, M_HEADS), st3)],
        out_shape=[jax.ShapeDtypeStruct((G, R, W), F32),
                   jax.ShapeDtypeStruct((G, M_HEADS, M_HEAD_DIM, M_HEAD_DIM), F32),
                   jax.ShapeDtypeStruct((G, M_HEADS, M_HEAD_DIM), F32),
                   jax.ShapeDtypeStruct((G, 1, M_HEADS), F32)],
        scratch_shapes=[pltpu.VMEM((gs, SUBLANES + L, 2 * W), F32)],
        compiler_params=_cparams("arbitrary", "arbitrary"),
        name="mlstm",
    )(zm, zm, zm, zm, zgate, conv_w, gbias, m_gain.reshape(1, W), conv0, C0, n0, m0.reshape(G, 1, M_HEADS))


def _hgrn_body(zq_ref, zf_ref, zi_ref, zgg_ref, lb_ref, gain_ref, S0_ref, og_ref, S_ref, ST, *, nblk, gs):
    Dh = G_HEAD_DIM
    B = SUBLANES
    c = pl.program_id(1)
    chains = [(g, h) for g in range(gs) for h in range(G_HEADS)]

    @pl.when(c == 0)
    def _():
        for i, (g, h) in enumerate(chains):
            ST[i] = S0_ref[g, h].T

    rowi = lax.broadcasted_iota(jnp.int32, (B, Dh), 0)

    def blk(bi, carry):
        r0 = pl.multiple_of(bi * B, B)
        rows = pl.ds(r0, B)
        ids = range(len(chains))
        sl = [slice(h * Dh, (h + 1) * Dh) for _, h in chains]
        gi = [g for g, _ in chains]
        f = [lb_ref[:, sl[i]] + (1.0 - lb_ref[:, sl[i]]) * _sigmoid(zf_ref[gi[i], rows, sl[i]]) for i in ids]
        kk = [1.0 - x for x in f]
        bc = [_cumsum_rows(jnp.log(x)) for x in f]
        btot = [x[B - 1:B, :] for x in bc]
        qh = [_silu(zq_ref[gi[i], rows, sl[i]]) * (Dh ** -0.5) for i in ids]
        v = [zi_ref[gi[i], rows, sl[i]] for i in ids]
        st = [ST[i] for i in ids]
        o = [_dot_nt(qh[i] * jnp.exp(bc[i]), st[i]) for i in ids]
        upd = [_dot_tn(v[i], kk[i] * jnp.exp(btot[i] - bc[i])) for i in ids]
        for s in range(B):
            p = [jnp.exp(jnp.where(rowi >= s, bc[i] - bc[i][s:s + 1, :], NEG_BIG)) * qh[i] * kk[i][s:s + 1, :]
                 for i in ids]
            o = [o[i] + jnp.sum(p[i], axis=-1, keepdims=True) * v[i][s:s + 1, :] for i in ids]
        for i in ids:
            ST[i] = st[i] * jnp.exp(btot[i]) + upd[i]
            rms = o[i] * lax.rsqrt(jnp.mean(o[i] * o[i], axis=-1, keepdims=True) + RMS_EPS)
            og_ref[gi[i], rows, sl[i]] = rms * gain_ref[:, sl[i]] * _silu(zgg_ref[gi[i], rows, sl[i]])
        return carry

    lax.fori_loop(0, nblk, blk, 0)

    @pl.when(c == pl.num_programs(1) - 1)
    def _():
        for i, (g, h) in enumerate(chains):
            S_ref[g, h] = ST[i].T


def _hgrn(zg, lb, g_gain, S0):
    G, R, _ = zg.shape
    W = G_WIDTH
    gs = 4
    Rc = min(R, 128)
    zspec = lambda blk: pl.BlockSpec((gs, Rc, W), lambda g, c: (g, c, blk))
    st4 = lambda g, c: (g, 0, 0, 0)
    return pl.pallas_call(
        functools.partial(_hgrn_body, nblk=Rc // SUBLANES, gs=gs),
        grid=(G // gs, R // Rc),
        in_specs=[zspec(0), zspec(1), zspec(2), zspec(3),
                  pl.BlockSpec((1, W), lambda g, c: (0, 0)),
                  pl.BlockSpec((1, W), lambda g, c: (0, 0)),
                  pl.BlockSpec((gs, G_HEADS, G_HEAD_DIM, G_HEAD_DIM), st4)],
        out_specs=[pl.BlockSpec((gs, Rc, W), lambda g, c: (g, c, 0)),
                   pl.BlockSpec((gs, G_HEADS, G_HEAD_DIM, G_HEAD_DIM), st4)],
        out_shape=[jax.ShapeDtypeStruct((G, R, W), F32),
                   jax.ShapeDtypeStruct((G, G_HEADS, G_HEAD_DIM, G_HEAD_DIM), F32)],
        scratch_shapes=[pltpu.VMEM((gs * G_HEADS, G_HEAD_DIM, G_HEAD_DIM), F32)],
        compiler_params=_cparams("arbitrary", "arbitrary"),
        name="hgrn2",
    )(zg, zg, zg, zg, lb.reshape(1, W), g_gain.reshape(1, W), S0)


def _rprep_body(x_ref, g_ref, sc_ref, sh_ref, shift0_ref, mu_ref, *refs, gb, rb):
    outs = refs[:6]
    hlast_ref, hbuf = refs[6:]
    B = SUBLANES
    n, tok, seq = _strips(gb, rb)
    S = hbuf.shape[1] - B
    whole_sequences = gb > 1

    if not whole_sequences:
        @pl.when(pl.program_id(1) == 0)
        def _():
            hbuf[:, B - 1:B, :] = shift0_ref[...]

    first = lax.broadcasted_iota(jnp.int32, (hbuf.shape[0], S, 1), 1) == 0

    def strip(i, carry):
        h = _rms(x_ref[tok(i)], g_ref[...]) * (1.0 + sc_ref[seq(i)]) + sh_ref[seq(i)]
        before = shift0_ref[seq(i)] if whole_sequences else hbuf[:, B - 1:B, :]
        xx = jnp.where(first, before, pltpu.roll(h, 1, axis=1)) - h
        last = h[:, S - 1:S, :]
        if whole_sequences:
            hlast_ref[seq(i)] = last
        else:
            hbuf[:, B - 1:B, :] = last
        for m in range(6):
            outs[m][tok(i)] = (h + xx * mu_ref[m:m + 1, :]).astype(BF16)
        return carry

    lax.fori_loop(0, n, strip, 0, unroll=4)
    if not whole_sequences:
        hlast_ref[...] = hbuf[:, B - 1:B, :]


def _rwkv_prep(x, g, mod, i_scale, i_shift, shift0, mu, tm):
    G, R, D = x.shape
    gb, rb, nrb = _row_blocking(G, R, tm)
    tok = pl.BlockSpec((gb, rb, D), lambda a, b: (a, b, 0))
    one = pl.BlockSpec((gb, 1, D), lambda a, b: (a, 0, 0))
    strip_shape = (1, SUBLANES + STRIP_ROWS, D) if gb == 1 else (STRIP_ROWS // rb, SUBLANES + rb, D)
    res = pl.pallas_call(
        functools.partial(_rprep_body, gb=gb, rb=rb),
        grid=(G // gb, nrb),
        in_specs=[tok,
                  pl.BlockSpec((1, 1, D), lambda a, b: (0, 0, 0)),
                  pl.BlockSpec((gb, 1, D), lambda a, b: (a, 0, i_scale)),
                  pl.BlockSpec((gb, 1, D), lambda a, b: (a, 0, i_shift)),
                  one,
                  pl.BlockSpec((6, D), lambda a, b: (0, 0))],
        out_specs=[tok] * 6 + [one],
        out_shape=[jax.ShapeDtypeStruct((G, R, D), BF16)] * 6 + [jax.ShapeDtypeStruct((G, 1, D), F32)],
        scratch_shapes=[pltpu.VMEM(strip_shape, F32)],
        compiler_params=_cparams("arbitrary", "arbitrary"),
        name="rwkv_prep",
    )(x, g.reshape(1, 1, D), mod, mod, shift0.reshape(G, 1, D), mu)
    return res[:6], res[6]


def _lora_body(x_ref, w1_ref, w2_ref, b_ref, o_ref, *, act, post):
    t = _dot(x_ref[...], w1_ref[...])
    if act == "tanh":
        t = jnp.tanh(t)
    elif act == "sigmoid":
        t = _sigmoid(t)
    y = _dot(t, w2_ref[...])
    if post == "log_decay":
        y = -jnp.exp(-_softplus(-(b_ref[...] + y)) - 0.5)
    elif post == "sigmoid":
        y = _sigmoid(b_ref[...] + y)
    o_ref[...] = y


def _lora(x, w1, w2, act, tm, post=None, bias=None):
    G, R, K = x.shape
    M = G * R
    r = w1.shape[1]
    rp = -(-r // LANES) * LANES
    w1p = jnp.zeros((K, rp), F32).at[:, :r].set(w1)
    w2p = jnp.zeros((rp, w2.shape[1]), F32).at[:r, :].set(w2)
    N = w2.shape[1]
    b = jnp.zeros((1, N), F32) if bias is None else bias.reshape(1, N).astype(F32)
    out = pl.pallas_call(
        functools.partial(_lora_body, act=act, post=post),
        grid=(M // tm,),
        in_specs=[pl.BlockSpec((tm, K), lambda i: (i, 0)),
                  pl.BlockSpec((K, rp), lambda i: (0, 0)),
                  pl.BlockSpec((rp, N), lambda i: (0, 0)),
                  pl.BlockSpec((1, N), lambda i: (0, 0))],
        out_specs=pl.BlockSpec((tm, N), lambda i: (i, 0)),
        out_shape=jax.ShapeDtypeStruct((M, N), F32),
        compiler_params=_cparams("arbitrary"),
        name="lora_" + str(act),
    )(x.reshape(M, K), w1p, w2p, b)
    return out.reshape(G, R, N)


def _head_mask(rows, cols, rper, cper):
    r = lax.broadcasted_iota(jnp.int32, (rows, cols), 0)
    c = lax.broadcasted_iota(jnp.int32, (rows, cols), 1)
    return (r // rper) == (c // cper)


def _bd(y, mask01):
    if y.shape[0] % (2 * SUBLANES) == 0:
        return jnp.concatenate([y.astype(BF16)] * R_GROUP, axis=0) * mask01
    return jnp.concatenate([y] * R_GROUP, axis=0).astype(BF16) * mask01


def _segsums(xs, ones_bd):
    c = xs[0].shape[0]
    if c % (2 * SUBLANES) == 0:
        lhs = jnp.concatenate([x.astype(BF16) for x in xs], axis=0)
    else:
        lhs = jnp.concatenate(xs, axis=0).astype(BF16)
    res = jnp.dot(lhs, ones_bd, preferred_element_type=F32)
    return [res[i * c:(i + 1) * c] for i in range(len(xs))]


def _rwkv_body(r_ref, k_ref, v_ref, lw_ref, a_ref, g_ref, kkp_ref, kap_ref, rk_ref,
               lnw_ref, lnb_ref, S0_ref, y_ref, S_ref, Sbd, *, c, nchunk, ng, gs):
    N = R_HEAD_DIM
    GW = R_GROUP_W
    j = pl.program_id(2)
    m_state = _head_mask(GW, GW, N, N)
    chains = [(q, gi) for q in range(gs) for gi in range(ng)]
    ids = range(len(chains))
    lanes = [slice(gi * GW, (gi + 1) * GW) for _, gi in chains]
    seqs = [q for q, _ in chains]

    @pl.when(j == 0)
    def _():
        for i, (q, gi) in enumerate(chains):
            s0 = S0_ref[q, gi * R_GROUP:(gi + 1) * R_GROUP].reshape(GW, N)
            Sbd[i] = jnp.where(m_state, jnp.concatenate([s0] * R_GROUP, axis=1), 0.0)

    ones_bd = jnp.where(m_state, 1.0, 0.0).astype(BF16)
    m_vec = jnp.where(_head_mask(R_GROUP * c, GW, c, N), 1.0, 0.0).astype(BF16)
    m_mat = jnp.where(_head_mask(R_GROUP * c, R_GROUP * c, c, c), 1.0, 0.0).astype(BF16)
    t_idx = lax.broadcasted_iota(jnp.int32, (2 * c, R_GROUP * c), 0)
    s_idx = lax.broadcasted_iota(jnp.int32, (2 * c, R_GROUP * c), 1) % c
    causal = jnp.where(t_idx < c, jnp.where(s_idx < t_idx, 1.0, 0.0), jnp.where(s_idx <= t_idx - c, 1.0, 0.0))
    nt = (((1,), (1,)), ((), ()))
    mm = lambda a, b: jnp.dot(a.astype(BF16), b, preferred_element_type=F32)

    eye = jnp.where(lax.broadcasted_iota(jnp.int32, (c, R_GROUP * c), 1) % c
                    == lax.broadcasted_iota(jnp.int32, (c, R_GROUP * c), 0), 1.0, 0.0)
    keys = ("ar", "bdb", "bdk", "bdv", "vb", "gam", "tail")

    def prepare_steps(ci, out):
        rows = pl.ds(pl.multiple_of(ci * c, c), c)
        ld = lambda ref: [ref[seqs[i], rows, lanes[i]] for i in ids]
        r, k, a = ld(r_ref), ld(k_ref), ld(a_ref)
        kk = [k[i] * kkp_ref[:, lanes[i]] for i in ids]
        k2 = [k[i] * (1.0 + (a[i] - 1.0) * kap_ref[:, lanes[i]]) for i in ids]
        sums = _segsums([kk[i] * kk[i] for i in ids] + [r[i] * k2[i] * rk_ref[:, lanes[i]] for i in ids], ones_bd)
        yield
        for i in ids:
            v, lw, g = v_ref[seqs[i], rows, lanes[i]], lw_ref[seqs[i], rows, lanes[i]], g_ref[seqs[i], rows, lanes[i]]
            cw = _cumsum_rows(lw)
            cwl = cw[c - 1:c, :]
            kn = kk[i] / jnp.maximum(jnp.sqrt(sums[i]), 1e-12)
            bb = kn * a[i]
            e_out = jnp.exp(-cw)
            e_end = jnp.exp(cwl - cw)
            out["ar"].append(jnp.concatenate([-kn * jnp.exp(cw - lw), r[i] * jnp.exp(cw)], axis=0).astype(BF16))
            out["gam"].append(jnp.exp(cwl))
            out["tail"].append(jnp.concatenate([sums[len(chains) + i] * v, g], axis=0))
            yield
            out["bdb"].append(_bd(bb * e_out, m_vec))
            out["bdk"].append(_bd(k2[i] * e_out, m_vec))
            out["bdv"].append(_bd(v, m_vec))
            out["vb"].append(jnp.concatenate([v, bb * e_end, k2[i] * e_end], axis=0).astype(BF16))
            yield

    def prepare(ci):
        out = {key: [] for key in keys}
        for _ in prepare_steps(ci, out):
            pass
        return out

    def chain(ci, p, side_work=None):
        def tick():
            if side_work is not None:
                next(side_work, None)

        rows = pl.ds(pl.multiple_of(ci * c, c), c)
        ar, bdv = p["ar"], p["bdv"]
        sbd = [Sbd[i] for i in ids]
        pb = [lax.dot_general(ar[i], p["bdb"][i], nt, preferred_element_type=F32) * causal for i in ids]
        pk = [(lax.dot_general(ar[i], p["bdk"][i], nt, preferred_element_type=F32) * causal).astype(BF16) for i in ids]
        tick()
        s0p = [lax.dot_general(ar[i], sbd[i].astype(BF16), nt, preferred_element_type=F32) for i in ids]
        kv = [mm(pk[i], bdv[i]) for i in ids]
        w = [s0p[i][:c] + kv[i][:c] for i in ids]
        tick()
        pw = [pb[i][:c] for i in ids]
        tinv = [eye + pw[i] for i in ids]
        levels = c.bit_length() - 1
        bdp = [_bd(pw[i], m_mat) for i in ids]
        pw = [mm(pw[i], bdp[i]) for i in ids]
        tick()
        for lvl in range(1, levels):
            bdp = [_bd(pw[i], m_mat) for i in ids]
            if lvl < levels - 1:
                res = [mm(jnp.concatenate([tinv[i], pw[i]], axis=0), bdp[i]) for i in ids]
                tinv = [tinv[i] + res[i][:c] for i in ids]
                pw = [res[i][c:] for i in ids]
            else:
                tinv = [tinv[i] + mm(tinv[i], bdp[i]) for i in ids]
            tick()
        u = [mm(tinv[i], _bd(w[i], m_vec)).astype(BF16) for i in ids]
        tick()
        y = [s0p[i][c:] + mm(pb[i][c:], _bd(u[i], m_vec)) + kv[i][c:] for i in ids]
        tick()
        upd = [_dot_tn(jnp.concatenate([u[i], p["vb"][i][:c]], axis=0), p["vb"][i][c:]) for i in ids]
        for i in ids:
            Sbd[i] = sbd[i] * p["gam"][i] + jnp.where(m_state, upd[i], 0.0)
        tick()
        ysum = _segsums(y, ones_bd)
        dlt = [y[i] - ysum[i] * (1.0 / N) for i in ids]
        vsum = _segsums([dlt[i] * dlt[i] for i in ids], ones_bd)
        for i in ids:
            yn = dlt[i] * lax.rsqrt(vsum[i] * (1.0 / N) + LN_X_EPS) * lnw_ref[:, lanes[i]] + lnb_ref[:, lanes[i]]
            y_ref[seqs[i], rows, lanes[i]] = ((yn + p["tail"][i][:c]) * p["tail"][i][c:]).astype(y_ref.dtype)
        if side_work is not None:
            for _ in side_work:
                pass

    if nchunk == 1:
        chain(0, prepare(0))
    else:
        def body(ci, p):
            nxt = {key: [] for key in keys}
            chain(ci, p, prepare_steps(jnp.minimum(ci + 1, nchunk - 1), nxt))
            return nxt

        lax.fori_loop(0, nchunk, body, prepare(0))

    @pl.when(j == pl.num_programs(2) - 1)
    def _():
        for i, (q, gi) in enumerate(chains):
            sbd = Sbd[i]
            for h in range(R_GROUP):
                S_ref[q, gi * R_GROUP + h] = sbd[h * N:(h + 1) * N, h * N:(h + 1) * N]


def _rwkv(r, k, v, lw, a, g, kkp, kap, rk, lnw, lnb, S0):
    G, R, D = r.shape
    c = math.gcd(R, CHUNK)
    GW = R_GROUP_W
    if R > SUBLANES:
        gs, ng, Rc = 1, 8, min(R, 256)
    else:
        gs, ng, Rc = 2, D // GW, R
    bw = ng * GW
    tok = pl.BlockSpec((gs, Rc, bw), lambda a_, b, j: (a_, j, b))
    par = pl.BlockSpec((1, bw), lambda a_, b, j: (0, b))
    st = pl.BlockSpec((gs, ng * R_GROUP, R_HEAD_DIM, R_HEAD_DIM), lambda a_, b, j: (a_, b, 0, 0))
    row = lambda p: p.reshape(1, D).astype(F32)
    return pl.pallas_call(
        functools.partial(_rwkv_body, c=c, nchunk=Rc // c, ng=ng, gs=gs),
        grid=(G // gs, D // bw, R // Rc),
        in_specs=[tok] * 6 + [par] * 5 + [st],
        out_specs=[tok, st],
        out_shape=[jax.ShapeDtypeStruct((G, R, D), BF16),
                   jax.ShapeDtypeStruct((G, R_HEADS, R_HEAD_DIM, R_HEAD_DIM), F32)],
        scratch_shapes=[pltpu.VMEM((gs * ng, GW, GW), F32)],
        compiler_params=_cparams("arbitrary", "arbitrary", "arbitrary"),
        name="rwkv7",
    )(r, k, v, lw, a, g, row(kkp), row(kap), row(rk), row(lnw), row(lnb), S0)


def _trunk(x, mod, m_C, m_n, m_m, m_conv, g_S, r_S, r_shift, p, lbs, tm):
    G, R, D = x.shape
    md = mod[0]
    h = _modnorm(x, p["norm_mix"][0], md, 1, 0, tm)
    w_in_t = jnp.swapaxes(p["ab_w_in"], 1, 2)
    zm = _mm(h, w_in_t, 0, n_cols=4 * M_WIDTH, tm=tm, tn=1024, name="mm_in_m", w_transposed=True)
    zgate = _mm(h, w_in_t, 0, n_cols=LANES, tm=tm, tn=LANES, name="mm_in_gate", w_transposed=True, col0=4 * M_WIDTH)
    zg = _mm(h, w_in_t, 0, n_cols=4 * G_WIDTH, tm=tm, tn=1024, name="mm_in_g", w_transposed=True,
             col0=4 * M_WIDTH + 2 * M_HEADS)
    hm, C, n, m = _mlstm(zm, zgate, p["m_conv_w"][0], p["ab_gate_b"][0], p["m_norm"][0],
                         m_conv[0], m_C[0], m_n[0], m_m[0])
    conv_new = zm[:, R - (M_CONV - 1):, :2 * M_WIDTH]
    og, S = _hgrn(zg, lbs[0], p["g_norm"][0], g_S[0])
    x = _mm_residual([hm, og], p["ab_w_out"], 0, x, md, 2, tm=tm, tn=512, name="mm_out0")
    h = _modnorm(x, p["norm_ffn"][0], md, 4, 3, tm)
    act = _mm(h, p["ffn_w1"], 0, n_cols=4 * D, tm=tm, tn=1024, act="relu2", out_dtype=BF16, name="ffn_up")
    x = _mm_residual([act], p["ffn_w2_bf16"], 0, x, md, 5, tm=256, tn=1024, name="ffn_down")
    md = mod[1]
    (xr, xw, xk, xv, xa, xg), shift_new = _rwkv_prep(x, p["norm_mix"][1], md, 1, 0, r_shift[0], p["r_mu"][0],
                                                     min(tm, 256))
    r = _mm(xr, p["r_wr"], 0, n_cols=D, tm=tm, tn=1024, name="mm_r")
    k = _mm(xk, p["r_wk"], 0, n_cols=D, tm=tm, tn=1024, name="mm_k")
    v = _mm(xv, p["r_wv"], 0, n_cols=D, tm=tm, tn=1024, name="mm_v")
    lw = _lora(xw, p["r_w1"][0], p["r_w2"][0], "tanh", min(tm, 512), post="log_decay", bias=p["r_w0"][0])
    aa = _lora(xa, p["r_a1"][0], p["r_a2"][0], None, min(tm, 512), post="sigmoid", bias=p["r_a0"][0])
    gg = _lora(xg, p["r_g1"][0], p["r_g2"][0], "sigmoid", min(tm, 512))
    yg, rS = _rwkv(r, k, v, lw, aa, gg, p["r_kk"][0], p["r_ka"][0], p["r_rk"][0].reshape(-1), p["r_lnw"][0],
                   p["r_lnb"][0], r_S[0])
    x = _mm_residual([yg], p["r_wo"], 0, x, md, 2, tm=tm, tn=1024, name="mm_out1")
    h = _modnorm(x, p["norm_ffn"][1], md, 4, 3, tm)
    act = _mm(h, p["ffn_w1"], 1, n_cols=4 * D, tm=tm, tn=1024, act="relu2", out_dtype=BF16, name="ffn_up")
    x = _mm_residual([act], p["ffn_w2_bf16"], 1, x, md, 5, tm=256, tn=1024, name="ffn_down")
    y = _rmsnorm(x, p["final_norm"], min(tm, 512))
    return y, (C[None], n[None], m.reshape(1, G, M_HEADS), conv_new[None], S[None], rS[None],
               shift_new.reshape(1, G, D))


def kernel(x_prompt, x_sample, c_prompt, c_sample, state_mlstm_C, state_mlstm_n, state_mlstm_m, state_mlstm_conv, state_hgrn_S, state_rwkv_S, state_rwkv_shift, mod_w, mod_b, norm_mix, norm_ffn, ffn_w1, ffn_w2, final_norm, ab_w_in, ab_gate_b, m_conv_w, m_norm, g_lb, g_norm, ab_w_out, r_mu, r_w0, r_w1, r_w2, r_a0, r_a1, r_a2, r_g1, r_g2, r_kk, r_ka, r_rk, r_wr, r_wk, r_wv, r_wo, r_lnw, r_lnb):
    p = dict(norm_mix=norm_mix, norm_ffn=norm_ffn, ffn_w1=ffn_w1, ffn_w2=ffn_w2, final_norm=final_norm,
             ab_w_in=ab_w_in, ab_gate_b=ab_gate_b, m_conv_w=m_conv_w, m_norm=m_norm, g_norm=g_norm,
             ab_w_out=ab_w_out, r_mu=r_mu, r_w0=r_w0, r_w1=r_w1, r_w2=r_w2, r_a0=r_a0, r_a1=r_a1, r_a2=r_a2,
             r_g1=r_g1, r_g2=r_g2, r_kk=r_kk, r_ka=r_ka, r_rk=r_rk, r_wr=r_wr, r_wk=r_wk, r_wv=r_wv,
             r_wo=r_wo, r_lnw=r_lnw, r_lnb=r_lnb)
    p["ffn_w2_bf16"] = ffn_w2.astype(BF16)
    B, T, D = x_prompt.shape
    Bs = x_sample.shape[0]
    depth = mod_w.shape[0]
    pad = (-(Bs + B)) % SUBLANES
    c_all = jnp.concatenate([c_sample, c_prompt, jnp.zeros((pad, D), F32)], axis=0)
    mod = _modulation(c_all, mod_w, mod_b)
    mod_s = mod[:, :Bs].reshape(depth, Bs, 1, N_MOD * D)
    mod_p = mod[:, Bs:Bs + B].reshape(depth, B, 1, N_MOD * D)
    lbs = jnp.cumsum(jax.nn.softmax(g_lb.astype(F32), axis=0), axis=0)

    n_even = state_mlstm_C.shape[0]
    n_odd = state_rwkv_S.shape[0]
    z = lambda *s: jnp.zeros(s, F32)
    yp, sp = _trunk(x_prompt, mod_p,
                    z(n_even, B, M_HEADS, M_HEAD_DIM, M_HEAD_DIM), z(n_even, B, M_HEADS, M_HEAD_DIM),
                    z(n_even, B, M_HEADS), z(n_even, B, M_CONV - 1, 2 * M_WIDTH),
                    z(n_even, B, G_HEADS, G_HEAD_DIM, G_HEAD_DIM), z(n_odd, B, R_HEADS, R_HEAD_DIM, R_HEAD_DIM),
                    z(n_odd, B, D), p, lbs, 1024)
    ys, ss = _trunk(x_sample, mod_s, state_mlstm_C, state_mlstm_n, state_mlstm_m, state_mlstm_conv,
                    state_hgrn_S, state_rwkv_S, state_rwkv_shift, p, lbs, 1024)
    return (yp, ys) + tuple(sp) + tuple(ss)
```

```python
import functools
import math

import jax
import jax.numpy as jnp
from jax import lax
from jax.experimental import pallas as pl
from jax.experimental.pallas import tpu as pltpu

F32 = jnp.float32
BF16 = jnp.bfloat16

D_MODEL = 2048
M_HEADS = 4
M_HEAD_DIM = 256
M_WIDTH = M_HEADS * M_HEAD_DIM
M_CONV = 4
G_HEADS = 8
G_HEAD_DIM = 128
G_WIDTH = G_HEADS * G_HEAD_DIM
R_HEAD_DIM = 64
R_HEADS = D_MODEL // R_HEAD_DIM
R_GROUP = 4
R_GROUP_W = R_GROUP * R_HEAD_DIM
N_MOD = 6
RMS_EPS = 1e-6
LN_X_EPS = 64e-5
CHUNK = 64
NEG_BIG = -1e30

V7X_VMEM_LIMIT_BYTES = 56 * 1024 * 1024
SUBLANES = 8
LANES = 128


def _cparams(*sem):
    return pltpu.CompilerParams(dimension_semantics=sem, vmem_limit_bytes=V7X_VMEM_LIMIT_BYTES)


def _sigmoid(x):
    return 1.0 / (1.0 + jnp.exp(-x))


def _silu(x):
    return x * _sigmoid(x)


def _softplus(x):
    return jnp.maximum(x, 0.0) + jnp.log1p(jnp.exp(-jnp.abs(x)))


def _dot(a, b):
    return jnp.dot(a.astype(BF16), b.astype(BF16), preferred_element_type=F32)


def _dot_nt(a, b):
    return lax.dot_general(a.astype(BF16), b.astype(BF16), (((1,), (1,)), ((), ())), preferred_element_type=F32)


def _dot_tn(a, b):
    return lax.dot_general(a.astype(BF16), b.astype(BF16), (((0,), (0,)), ((), ())), preferred_element_type=F32)


def _cumsum_rows(x):
    n = x.shape[0]
    row = lax.broadcasted_iota(jnp.int32, x.shape, 0)
    s = 1
    while s < n:
        x = x + jnp.where(row >= s, pltpu.roll(x, s, axis=0), 0.0)
        s *= 2
    return x


def _row_blocking(G, R, tm):
    if R >= tm:
        assert R % tm == 0
        return 1, tm, R // tm
    assert tm % R == 0 and G % (tm // R) == 0
    return tm // R, R, 1


def _mod_body(c_ref, w_ref, b_ref, o_ref):
    sc = _silu(c_ref[...])
    o_ref[...] = _dot(sc, w_ref[...]) + b_ref[...]


def _modulation(c_all, mod_w, mod_b):
    L, K, N = mod_w.shape
    Mc = c_all.shape[0]
    tn = 1024
    return pl.pallas_call(
        _mod_body,
        grid=(L, N // tn),
        in_specs=[pl.BlockSpec((Mc, K), lambda l, j: (0, 0)),
                  pl.BlockSpec((None, K, tn), lambda l, j: (l, 0, j)),
                  pl.BlockSpec((None, 1, tn), lambda l, j: (l, 0, j))],
        out_specs=pl.BlockSpec((None, Mc, tn), lambda l, j: (l, 0, j)),
        out_shape=jax.ShapeDtypeStruct((L, Mc, N), F32),
        compiler_params=_cparams("arbitrary", "arbitrary"),
        name="modulation",
    )(c_all, mod_w, mod_b.reshape(L, 1, N))


def _rms(x, g):
    ms = jnp.mean(x * x, axis=-1, keepdims=True)
    return x * lax.rsqrt(ms + RMS_EPS) * g


STRIP_ROWS = 2 * SUBLANES


def _strips(gb, rb):
    if gb == 1:
        n = rb // STRIP_ROWS
        tok = lambda i: (slice(None), pl.ds(pl.multiple_of(i * STRIP_ROWS, STRIP_ROWS), STRIP_ROWS), slice(None))
        seq = lambda i: (slice(None), slice(None), slice(None))
    else:
        assert STRIP_ROWS % rb == 0
        per = STRIP_ROWS // rb
        n = gb // per
        tok = seq = lambda i: (pl.ds(i * per, per), slice(None), slice(None))
    return n, tok, seq


def _modnorm_body(x_ref, g_ref, sc_ref, sh_ref, o_ref, *, gb, rb):
    n, tok, seq = _strips(gb, rb)

    def strip(i, carry):
        y = _rms(x_ref[tok(i)], g_ref[...])
        o_ref[tok(i)] = (y * (1.0 + sc_ref[seq(i)]) + sh_ref[seq(i)]).astype(o_ref.dtype)
        return carry

    lax.fori_loop(0, n, strip, 0, unroll=4)


def _modnorm(x, g, mod, i_scale, i_shift, tm):
    G, R, D = x.shape
    gb, rb, nrb = _row_blocking(G, R, tm)
    return pl.pallas_call(
        functools.partial(_modnorm_body, gb=gb, rb=rb),
        grid=(G // gb, nrb),
        in_specs=[pl.BlockSpec((gb, rb, D), lambda a, b: (a, b, 0)),
                  pl.BlockSpec((1, 1, D), lambda a, b: (0, 0, 0)),
                  pl.BlockSpec((gb, 1, D), lambda a, b: (a, 0, i_scale)),
                  pl.BlockSpec((gb, 1, D), lambda a, b: (a, 0, i_shift))],
        out_specs=pl.BlockSpec((gb, rb, D), lambda a, b: (a, b, 0)),
        out_shape=jax.ShapeDtypeStruct((G, R, D), BF16),
        compiler_params=_cparams("arbitrary", "arbitrary"),
        name="modnorm",
    )(x, g.reshape(1, 1, D), mod, mod)


def _rmsnorm_body(x_ref, g_ref, o_ref, *, gb, rb):
    n, tok, _ = _strips(gb, rb)

    def strip(i, carry):
        o_ref[tok(i)] = _rms(x_ref[tok(i)], g_ref[...])
        return carry

    lax.fori_loop(0, n, strip, 0, unroll=4)


def _rmsnorm(x, g, tm):
    G, R, D = x.shape
    gb, rb, nrb = _row_blocking(G, R, tm)
    return pl.pallas_call(
        functools.partial(_rmsnorm_body, gb=gb, rb=rb),
        grid=(G // gb, nrb),
        in_specs=[pl.BlockSpec((gb, rb, D), lambda a, b: (a, b, 0)),
                  pl.BlockSpec((1, 1, D), lambda a, b: (0, 0, 0))],
        out_specs=pl.BlockSpec((gb, rb, D), lambda a, b: (a, b, 0)),
        out_shape=jax.ShapeDtypeStruct((G, R, D), F32),
        compiler_params=_cparams("arbitrary", "arbitrary"),
        name="final_norm",
    )(x, g.reshape(1, 1, D))


def _mm_body(a_ref, w_ref, *rest, act, w_transposed, side_cast):
    if side_cast:
        src_ref, o_ref, dst_ref, wbf = rest
        dst_ref[...] = src_ref[...].astype(BF16)
    else:
        o_ref, wbf = rest

    @pl.when(pl.program_id(1) == 0)
    def _():
        wbf[...] = (w_ref[0].T if w_transposed else w_ref[...]).astype(BF16)

    acc = jnp.dot(a_ref[...].astype(BF16), wbf[...], preferred_element_type=F32)
    if act == "relu2":
        acc = jnp.square(jnp.maximum(acc, 0.0))
    o_ref[...] = acc.astype(o_ref.dtype)


def _mm(a, w, layer, *, n_cols, tm, tn, act=None, out_dtype=F32, name="mm", w_transposed=False, col0=0,
        cast_src=None):
    G, R, K = a.shape
    M = G * R
    assert M % tm == 0 and n_cols % tn == 0 and w.shape[2 if w_transposed else 1] == K
    nm = M // tm
    if w_transposed:
        assert col0 % SUBLANES == 0
        w_spec = pl.BlockSpec((pl.Element(1), pl.Element(tn), pl.Element(K)),
                              lambda j, i: (layer, pl.multiple_of(col0 + j * tn, SUBLANES), 0))
    else:
        assert col0 == 0
        w_spec = pl.BlockSpec((None, K, tn), lambda j, i: (layer, 0, j))
    in_specs = [pl.BlockSpec((tm, K), lambda j, i: (i, 0)), w_spec]
    out_specs = [pl.BlockSpec((tm, tn), lambda j, i: (i, j))]
    out_shape = [jax.ShapeDtypeStruct((M, n_cols), out_dtype)]
    args = [a.reshape(M, K), w]
    if cast_src is not None:
        _, P, Q = cast_src.shape
        steps = (n_cols // tn) * nm
        assert P % steps == 0
        in_specs.append(pl.BlockSpec((None, P // steps, Q), lambda j, i: (layer, j * nm + i, 0)))
        out_specs.append(pl.BlockSpec((P // steps, Q), lambda j, i: (j * nm + i, 0)))
        out_shape.append(jax.ShapeDtypeStruct((P, Q), BF16))
        args.append(cast_src)
    res = pl.pallas_call(
        functools.partial(_mm_body, act=act, w_transposed=w_transposed, side_cast=cast_src is not None),
        grid=(n_cols // tn, nm),
        in_specs=in_specs,
        out_specs=out_specs,
        out_shape=out_shape,
        scratch_shapes=[pltpu.VMEM((K, tn), BF16)],
        compiler_params=_cparams("arbitrary", "arbitrary"),
        name=name,
    )(*args)
    out = res[0].reshape(G, R, n_cols)
    return (out, res[1]) if cast_src is not None else out


def _mm_res_body(*refs, n_lhs, gb, rb, cast_w):
    a_refs = refs[:n_lhs]
    w_ref, x_ref, gt_ref, o_ref = refs[n_lhs:n_lhs + 4]
    if cast_w:
        wbf = refs[n_lhs + 4]

        @pl.when(pl.program_id(1) == 0)
        def _():
            wbf[...] = w_ref[...].astype(BF16)
    else:
        wbf = w_ref

    acc = None
    k0 = 0
    for a_ref in a_refs:
        kk = a_ref.shape[-1]
        part = jnp.dot(a_ref[...].astype(BF16), wbf[k0:k0 + kk, :], preferred_element_type=F32)
        acc = part if acc is None else acc + part
        k0 += kk
    tn = acc.shape[-1]
    o_ref[...] = x_ref[...] + gt_ref[...] * acc.reshape(gb, rb, tn)


def _mm_residual(a_list, w, layer, x, mod, i_gate, *, tm, tn, name="mm_res"):
    G, R, N = x.shape
    M = G * R
    K = w.shape[1]
    assert sum(a.shape[-1] for a in a_list) == K and w.shape[2] == N
    gb, rb, nrb = _row_blocking(G, R, tm)
    ntn = N // tn
    cast_w = w.dtype != BF16

    def xmap(j, i):
        return (i // nrb, i % nrb, j)

    return pl.pallas_call(
        functools.partial(_mm_res_body, n_lhs=len(a_list), gb=gb, rb=rb, cast_w=cast_w),
        grid=(ntn, M // tm),
        in_specs=[pl.BlockSpec((tm, a.shape[-1]), lambda j, i: (i, 0)) for a in a_list]
        + [pl.BlockSpec((None, K, tn), lambda j, i: (layer, 0, j)),
           pl.BlockSpec((gb, rb, tn), xmap),
           pl.BlockSpec((gb, 1, tn), lambda j, i: (i // nrb, 0, i_gate * ntn + j))],
        out_specs=pl.BlockSpec((gb, rb, tn), xmap),
        out_shape=jax.ShapeDtypeStruct((G, R, N), F32),
        scratch_shapes=[pltpu.VMEM((K, tn), BF16)] if cast_w else [],
        compiler_params=_cparams("arbitrary", "arbitrary"),
        name=name,
    )(*[a.reshape(M, a.shape[-1]) for a in a_list], w, x, mod)


def _mlstm_body(zq_ref, zk_ref, zv_ref, zo_ref, zg_ref, cw_ref, gbias_ref, gain_ref, conv0_ref,
                C0_ref, n0_ref, m0_ref, hm_ref, C_ref, n_ref, m_ref, ext, *, L, gs):
    W = M_WIDTH
    Dh = M_HEAD_DIM
    H = M_HEADS
    keep = M_CONV - 1
    c = pl.program_id(1)

    @pl.when(c == 0)
    def _():
        C_ref[...] = C0_ref[...]
        n_ref[...] = n0_ref[...]
        m_ref[...] = m0_ref[...]
        ext[:, SUBLANES - keep:SUBLANES, :] = conv0_ref[...]

    ext[:, SUBLANES:SUBLANES + L, 0:W] = zq_ref[...]
    ext[:, SUBLANES:SUBLANES + L, W:2 * W] = zk_ref[...]
    row = lax.broadcasted_iota(jnp.int32, (L, L), 0)
    col = lax.broadcasted_iota(jnp.int32, (L, L), 1)
    eye = row == col
    tri = col <= row
    chains = [(g, h) for g in range(gs) for h in range(H)]
    ids = range(len(chains))
    qk, gates = [], []
    for g in range(gs):
        conv = ext[g, SUBLANES:SUBLANES + L, :] * cw_ref[keep:keep + 1, :]
        for j in range(keep):
            conv = conv + ext[g, SUBLANES - keep + j:SUBLANES - keep + j + L, :] * cw_ref[j:j + 1, :]
        qk.append(_silu(conv))
        gates.append(zg_ref[g] + gbias_ref[...])
    ext[:, SUBLANES - keep:SUBLANES, :] = ext[:, SUBLANES + L - keep:SUBLANES + L, :]

    sl = [slice(h * Dh, (h + 1) * Dh) for _, h in chains]
    q = [qk[g][:, h * Dh:(h + 1) * Dh] for g, h in chains]
    k = [qk[g][:, W + h * Dh:W + (h + 1) * Dh] * (Dh ** -0.5) for g, h in chains]
    v = [zv_ref[g, :, sl[i]] for i, (g, h) in enumerate(chains)]
    ig_col = [gates[g][:, h:h + 1] for g, h in chains]
    fpre = [gates[g][:, H + h:H + h + 1] for g, h in chains]
    lf_col = [jnp.minimum(x, 0.0) - jnp.log1p(jnp.exp(-jnp.abs(x))) for x in fpre]
    lf_row = [jnp.sum(jnp.where(eye, x, 0.0), axis=0, keepdims=True) for x in lf_col]
    ig_row = [jnp.sum(jnp.where(eye, x, 0.0), axis=0, keepdims=True) for x in ig_col]
    b_col = [jnp.sum(jnp.where(tri, x, 0.0), axis=1, keepdims=True) for x in lf_row]
    b_row = [jnp.sum(jnp.where(row <= col, x, 0.0), axis=0, keepdims=True) for x in lf_col]
    dmat = [jnp.where(tri, b_col[i] - b_row[i] + ig_row[i], NEG_BIG) for i in ids]
    m_prev = [m_ref[g, :, h:h + 1] for g, h in chains]
    inter = [b_col[i] + m_prev[i] for i in ids]
    m_t = [jnp.maximum(inter[i], jnp.max(dmat[i], axis=1, keepdims=True)) for i in ids]
    w_inter = [jnp.exp(inter[i] - m_t[i]) for i in ids]
    Cm = [C_ref[g, h] for g, h in chains]
    n_row = [n_ref[g, h:h + 1, :] for g, h in chains]
    s = [_dot_nt(q[i], k[i]) * jnp.exp(dmat[i] - m_t[i]) for i in ids]
    qc = [_dot(q[i], Cm[i]) for i in ids]
    num = [_dot(s[i], v[i]) + w_inter[i] * qc[i] for i in ids]
    den = [jnp.sum(s[i], axis=1, keepdims=True) + w_inter[i] * jnp.sum(q[i] * n_row[i], axis=1, keepdims=True)
           for i in ids]
    hh = [num[i] / jnp.maximum(jnp.abs(den[i]), jnp.exp(-m_t[i])) for i in ids]
    m_new = [x[L - 1:L, :] for x in m_t]
    b_last = [x[L - 1:L, :] for x in b_col]
    wk = [jnp.exp(b_last[i] - b_col[i] + ig_col[i] - m_new[i]) * k[i] for i in ids]
    decay = [jnp.exp(b_last[i] + m_prev[i] - m_new[i]) for i in ids]
    upd = [_dot_tn(wk[i], v[i]) for i in ids]
    for i, (g, h) in enumerate(chains):
        C_ref[g, h] = decay[i] * Cm[i] + upd[i]
        n_ref[g, h:h + 1, :] = decay[i] * n_row[i] + jnp.sum(wk[i], axis=0, keepdims=True)
        m_ref[g, :, h:h + 1] = m_new[i]
        dlt = hh[i] - jnp.mean(hh[i], axis=-1, keepdims=True)
        ln = dlt * lax.rsqrt(jnp.mean(dlt * dlt, axis=-1, keepdims=True) + RMS_EPS)
        hm_ref[g, :, sl[i]] = _sigmoid(zo_ref[g, :, sl[i]]) * ln * gain_ref[:, sl[i]]


def _mlstm(zm, zgate, conv_w, gate_b, m_gain, conv0, C0, n0, m0):
    G, R, _ = zm.shape
    L = math.gcd(R, CHUNK)
    W = M_WIDTH
    gs = 2 if R > SUBLANES else 4
    gbias = jnp.zeros((1, LANES), F32).at[0, :2 * M_HEADS].set(gate_b.astype(F32))
    zspec = lambda blk: pl.BlockSpec((gs, L, W), lambda g, c: (g, c, blk))
    st4 = lambda g, c: (g, 0, 0, 0)
    st3 = lambda g, c: (g, 0, 0)
    return pl.pallas_call(
        functools.partial(_mlstm_body, L=L, gs=gs),
        grid=(G // gs, R // L),
        in_specs=[zspec(0), zspec(1), zspec(2), zspec(3),
                  pl.BlockSpec((gs, L, LANES), lambda g, c: (g, c, 0)),
                  pl.BlockSpec((M_CONV, 2 * W), lambda g, c: (0, 0)),
                  pl.BlockSpec((1, LANES), lambda g, c: (0, 0)),
                  pl.BlockSpec((1, W), lambda g, c: (0, 0)),
                  pl.BlockSpec((gs, M_CONV - 1, 2 * W), st3),
                  pl.BlockSpec((gs, M_HEADS, M_HEAD_DIM, M_HEAD_DIM), st4),
                  pl.BlockSpec((gs, M_HEADS, M_HEAD_DIM), st3),
                  pl.BlockSpec((gs, 1, M_HEADS), st3)],
        out_specs=[pl.BlockSpec((gs, L, W), lambda g, c: (g, c, 0)),
                   pl.BlockSpec((gs, M_HEADS, M_HEAD_DIM, M_HEAD_DIM), st4),
                   pl.BlockSpec((gs, M_HEADS, M_HEAD_DIM), st3),
                   pl.BlockSpec((gs, 1, M_HEADS), st3)],
        out_shape=[jax.ShapeDtypeStruct((G, R, W), F32),
                   jax.ShapeDtypeStruct((G, M_HEADS, M_HEAD_DIM, M_HEAD_DIM), F32),
                   jax.ShapeDtypeStruct((G, M_HEADS, M_HEAD_DIM), F32),
                   jax.ShapeDtypeStruct((G, 1, M_HEADS), F32)],
        scratch_shapes=[pltpu.VMEM((gs, SUBLANES + L, 2 * W), F32)],
        compiler_params=_cparams("arbitrary", "arbitrary"),
        name="mlstm",
    )(zm, zm, zm, zm, zgate, conv_w, gbias, m_gain.reshape(1, W), conv0, C0, n0, m0.reshape(G, 1, M_HEADS))


def _hgrn_body(zq_ref, zf_ref, zi_ref, zgg_ref, lb_ref, gain_ref, S0_ref, og_ref, S_ref, ST, *, nblk, gs, st_t):
    Dh = G_HEAD_DIM
    B = SUBLANES
    c = pl.program_id(1)
    chains = [(g, h) for g in range(gs) for h in range(G_HEADS)]

    @pl.when(c == 0)
    def _():
        for i, (g, h) in enumerate(chains):
            ST[i] = S0_ref[g, h].T if st_t else S0_ref[g, h]

    rowi = lax.broadcasted_iota(jnp.int32, (B, Dh), 0)

    def blk(bi, carry):
        r0 = pl.multiple_of(bi * B, B)
        rows = pl.ds(r0, B)
        ids = range(len(chains))
        sl = [slice(h * Dh, (h + 1) * Dh) for _, h in chains]
        gi = [g for g, _ in chains]
        f = [lb_ref[:, sl[i]] + (1.0 - lb_ref[:, sl[i]]) * _sigmoid(zf_ref[gi[i], rows, sl[i]]) for i in ids]
        kk = [1.0 - x for x in f]
        bc = [_cumsum_rows(jnp.log(x)) for x in f]
        btot = [x[B - 1:B, :] for x in bc]
        qh = [_silu(zq_ref[gi[i], rows, sl[i]]) * (Dh ** -0.5) for i in ids]
        v = [zi_ref[gi[i], rows, sl[i]] for i in ids]
        st = [ST[i] for i in ids]
        qd = [qh[i] * jnp.exp(bc[i]) for i in ids]
        kd = [kk[i] * jnp.exp(btot[i] - bc[i]) for i in ids]
        if st_t:
            o = [_dot_nt(qd[i], st[i]) for i in ids]
            upd = [_dot_tn(v[i], kd[i]) for i in ids]
            dec = [jnp.exp(btot[i]) for i in ids]
        else:
            o = [_dot(qd[i], st[i]) for i in ids]
            upd = [_dot_tn(kd[i], v[i]) for i in ids]
            dec = [jnp.transpose(jnp.broadcast_to(jnp.exp(btot[i]), (B, Dh)))[:, 0:1] for i in ids]
        for s in range(B):
            p = [jnp.exp(jnp.where(rowi >= s, bc[i] - bc[i][s:s + 1, :], NEG_BIG)) * qh[i] * kk[i][s:s + 1, :]
                 for i in ids]
            o = [o[i] + jnp.sum(p[i], axis=-1, keepdims=True) * v[i][s:s + 1, :] for i in ids]
        for i in ids:
            ST[i] = st[i] * dec[i] + upd[i]
            rms = o[i] * lax.rsqrt(jnp.mean(o[i] * o[i], axis=-1, keepdims=True) + RMS_EPS)
            og_ref[gi[i], rows, sl[i]] = rms * gain_ref[:, sl[i]] * _silu(zgg_ref[gi[i], rows, sl[i]])
        return carry

    lax.fori_loop(0, nblk, blk, 0)

    @pl.when(c == pl.num_programs(1) - 1)
    def _():
        for i, (g, h) in enumerate(chains):
            S_ref[g, h] = ST[i].T if st_t else ST[i]


def _hgrn(zg, lb, g_gain, S0):
    G, R, _ = zg.shape
    W = G_WIDTH
    gs = 4
    Rc = min(R, 128)
    zspec = lambda blk: pl.BlockSpec((gs, Rc, W), lambda g, c: (g, c, blk))
    st4 = lambda g, c: (g, 0, 0, 0)
    return pl.pallas_call(
        functools.partial(_hgrn_body, nblk=Rc // SUBLANES, gs=gs, st_t=R > SUBLANES),
        grid=(G // gs, R // Rc),
        in_specs=[zspec(0), zspec(1), zspec(2), zspec(3),
                  pl.BlockSpec((1, W), lambda g, c: (0, 0)),
                  pl.BlockSpec((1, W), lambda g, c: (0, 0)),
                  pl.BlockSpec((gs, G_HEADS, G_HEAD_DIM, G_HEAD_DIM), st4)],
        out_specs=[pl.BlockSpec((gs, Rc, W), lambda g, c: (g, c, 0)),
                   pl.BlockSpec((gs, G_HEADS, G_HEAD_DIM, G_HEAD_DIM), st4)],
        out_shape=[jax.ShapeDtypeStruct((G, R, W), F32),
                   jax.ShapeDtypeStruct((G, G_HEADS, G_HEAD_DIM, G_HEAD_DIM), F32)],
        scratch_shapes=[pltpu.VMEM((gs * G_HEADS, G_HEAD_DIM, G_HEAD_DIM), F32)],
        compiler_params=_cparams("arbitrary", "arbitrary"),
        name="hgrn2",
    )(zg, zg, zg, zg, lb.reshape(1, W), g_gain.reshape(1, W), S0)


def _rprep_body(x_ref, g_ref, sc_ref, sh_ref, shift0_ref, mu_ref, *refs, gb, rb):
    outs = refs[:6]
    hlast_ref, hbuf = refs[6:]
    B = SUBLANES
    n, tok, seq = _strips(gb, rb)
    S = hbuf.shape[1] - B
    whole_sequences = gb > 1

    if not whole_sequences:
        @pl.when(pl.program_id(1) == 0)
        def _():
            hbuf[:, B - 1:B, :] = shift0_ref[...]

    first = lax.broadcasted_iota(jnp.int32, (hbuf.shape[0], S, 1), 1) == 0

    def strip(i, carry):
        h = _rms(x_ref[tok(i)], g_ref[...]) * (1.0 + sc_ref[seq(i)]) + sh_ref[seq(i)]
        before = shift0_ref[seq(i)] if whole_sequences else hbuf[:, B - 1:B, :]
        xx = jnp.where(first, before, pltpu.roll(h, 1, axis=1)) - h
        last = h[:, S - 1:S, :]
        if whole_sequences:
            hlast_ref[seq(i)] = last
        else:
            hbuf[:, B - 1:B, :] = last
        for m in range(6):
            outs[m][tok(i)] = (h + xx * mu_ref[m:m + 1, :]).astype(BF16)
        return carry

    lax.fori_loop(0, n, strip, 0, unroll=4)
    if not whole_sequences:
        hlast_ref[...] = hbuf[:, B - 1:B, :]


def _rwkv_prep(x, g, mod, i_scale, i_shift, shift0, mu, tm):
    G, R, D = x.shape
    gb, rb, nrb = _row_blocking(G, R, tm)
    tok = pl.BlockSpec((gb, rb, D), lambda a, b: (a, b, 0))
    one = pl.BlockSpec((gb, 1, D), lambda a, b: (a, 0, 0))
    strip_shape = (1, SUBLANES + STRIP_ROWS, D) if gb == 1 else (STRIP_ROWS // rb, SUBLANES + rb, D)
    res = pl.pallas_call(
        functools.partial(_rprep_body, gb=gb, rb=rb),
        grid=(G // gb, nrb),
        in_specs=[tok,
                  pl.BlockSpec((1, 1, D), lambda a, b: (0, 0, 0)),
                  pl.BlockSpec((gb, 1, D), lambda a, b: (a, 0, i_scale)),
                  pl.BlockSpec((gb, 1, D), lambda a, b: (a, 0, i_shift)),
                  one,
                  pl.BlockSpec((6, D), lambda a, b: (0, 0))],
        out_specs=[tok] * 6 + [one],
        out_shape=[jax.ShapeDtypeStruct((G, R, D), BF16)] * 6 + [jax.ShapeDtypeStruct((G, 1, D), F32)],
        scratch_shapes=[pltpu.VMEM(strip_shape, F32)],
        compiler_params=_cparams("arbitrary", "arbitrary"),
        name="rwkv_prep",
    )(x, g.reshape(1, 1, D), mod, mod, shift0.reshape(G, 1, D), mu)
    return res[:6], res[6]


def _lora_body(x_ref, w1_ref, w2_ref, b_ref, o_ref, *, act, post):
    t = _dot(x_ref[...], w1_ref[...])
    if act == "tanh":
        t = jnp.tanh(t)
    elif act == "sigmoid":
        t = _sigmoid(t)
    y = _dot(t, w2_ref[...])
    if post == "log_decay":
        y = -jnp.exp(-_softplus(-(b_ref[...] + y)) - 0.5)
    elif post == "sigmoid":
        y = _sigmoid(b_ref[...] + y)
    o_ref[...] = y


def _lora(x, w1, w2, act, tm, post=None, bias=None):
    G, R, K = x.shape
    M = G * R
    r = w1.shape[1]
    rp = -(-r // LANES) * LANES
    w1p = jnp.zeros((K, rp), F32).at[:, :r].set(w1)
    w2p = jnp.zeros((rp, w2.shape[1]), F32).at[:r, :].set(w2)
    N = w2.shape[1]
    b = jnp.zeros((1, N), F32) if bias is None else bias.reshape(1, N).astype(F32)
    out = pl.pallas_call(
        functools.partial(_lora_body, act=act, post=post),
        grid=(M // tm,),
        in_specs=[pl.BlockSpec((tm, K), lambda i: (i, 0)),
                  pl.BlockSpec((K, rp), lambda i: (0, 0)),
                  pl.BlockSpec((rp, N), lambda i: (0, 0)),
                  pl.BlockSpec((1, N), lambda i: (0, 0))],
        out_specs=pl.BlockSpec((tm, N), lambda i: (i, 0)),
        out_shape=jax.ShapeDtypeStruct((M, N), F32),
        compiler_params=_cparams("arbitrary"),
        name="lora_" + str(act),
    )(x.reshape(M, K), w1p, w2p, b)
    return out.reshape(G, R, N)


def _head_mask(rows, cols, rper, cper):
    r = lax.broadcasted_iota(jnp.int32, (rows, cols), 0)
    c = lax.broadcasted_iota(jnp.int32, (rows, cols), 1)
    return (r // rper) == (c // cper)


def _bd(y, mask01):
    if y.shape[0] % (2 * SUBLANES) == 0:
        return jnp.concatenate([y.astype(BF16)] * R_GROUP, axis=0) * mask01
    return jnp.concatenate([y] * R_GROUP, axis=0).astype(BF16) * mask01


def _segsums(xs, ones_bd):
    c = xs[0].shape[0]
    if c % (2 * SUBLANES) == 0:
        lhs = jnp.concatenate([x.astype(BF16) for x in xs], axis=0)
    else:
        lhs = jnp.concatenate(xs, axis=0).astype(BF16)
    res = jnp.dot(lhs, ones_bd, preferred_element_type=F32)
    return [res[i * c:(i + 1) * c] for i in range(len(xs))]


def _rwkv_body(r_ref, k_ref, v_ref, lw_ref, a_ref, g_ref, kkp_ref, kap_ref, rk_ref,
               lnw_ref, lnb_ref, S0_ref, y_ref, S_ref, Sbd, *, c, nchunk, ng, gs):
    N = R_HEAD_DIM
    GW = R_GROUP_W
    j = pl.program_id(2)
    m_state = _head_mask(GW, GW, N, N)
    chains = [(q, gi) for q in range(gs) for gi in range(ng)]
    ids = range(len(chains))
    lanes = [slice(gi * GW, (gi + 1) * GW) for _, gi in chains]
    seqs = [q for q, _ in chains]

    @pl.when(j == 0)
    def _():
        for i, (q, gi) in enumerate(chains):
            s0 = S0_ref[q, gi * R_GROUP:(gi + 1) * R_GROUP].reshape(GW, N)
            Sbd[i] = jnp.where(m_state, jnp.concatenate([s0] * R_GROUP, axis=1), 0.0)

    ones_bd = jnp.where(m_state, 1.0, 0.0).astype(BF16)
    m_vec = jnp.where(_head_mask(R_GROUP * c, GW, c, N), 1.0, 0.0).astype(BF16)
    m_mat = jnp.where(_head_mask(R_GROUP * c, R_GROUP * c, c, c), 1.0, 0.0).astype(BF16)
    t_idx = lax.broadcasted_iota(jnp.int32, (2 * c, R_GROUP * c), 0)
    s_idx = lax.broadcasted_iota(jnp.int32, (2 * c, R_GROUP * c), 1) % c
    causal = jnp.where(t_idx < c, jnp.where(s_idx < t_idx, 1.0, 0.0), jnp.where(s_idx <= t_idx - c, 1.0, 0.0))
    nt = (((1,), (1,)), ((), ()))
    mm = lambda a, b: jnp.dot(a.astype(BF16), b, preferred_element_type=F32)

    tril = jnp.where(lax.broadcasted_iota(jnp.int32, (c, c), 1) <= lax.broadcasted_iota(jnp.int32, (c, c), 0),
                     1.0, 0.0).astype(BF16)
    eye = jnp.where(lax.broadcasted_iota(jnp.int32, (c, R_GROUP * c), 1) % c
                    == lax.broadcasted_iota(jnp.int32, (c, R_GROUP * c), 0), 1.0, 0.0)
    keys = ("ar", "bdb", "bdk", "bdv", "vb", "gam", "tail")

    def prepare_steps(ci, out):
        rows = pl.ds(pl.multiple_of(ci * c, c), c)
        ld = lambda ref: [ref[seqs[i], rows, lanes[i]] for i in ids]
        r, k, a = ld(r_ref), ld(k_ref), ld(a_ref)
        kk = [k[i] * kkp_ref[:, lanes[i]] for i in ids]
        k2 = [k[i] * (1.0 + (a[i] - 1.0) * kap_ref[:, lanes[i]]) for i in ids]
        sums = _segsums([kk[i] * kk[i] for i in ids] + [r[i] * k2[i] * rk_ref[:, lanes[i]] for i in ids], ones_bd)
        yield
        for i in ids:
            v, lw, g = v_ref[seqs[i], rows, lanes[i]], lw_ref[seqs[i], rows, lanes[i]], g_ref[seqs[i], rows, lanes[i]]
            hi = lw.astype(BF16)
            lo = (lw - hi.astype(F32)).astype(BF16)
            cw2 = jnp.dot(tril, jnp.concatenate([hi, lo], axis=1), preferred_element_type=F32)
            cw = cw2[:, :GW] + cw2[:, GW:]
            gam = jnp.exp(cw[c - 1:c, :])
            kn = kk[i] * lax.rsqrt(jnp.maximum(sums[i], 1e-24))
            bb = kn * a[i]
            e_out = jnp.exp(-cw)
            e_end = gam * e_out
            out["ar"].append(jnp.concatenate([-kn * jnp.exp(cw - lw), r[i] * jnp.exp(cw)], axis=0).astype(BF16))
            out["gam"].append(gam)
            out["tail"].append(jnp.concatenate([sums[len(chains) + i] * v, g], axis=0))
            yield
            out["bdb"].append(_bd(bb * e_out, m_vec))
            out["bdk"].append(_bd(k2[i] * e_out, m_vec))
            out["bdv"].append(_bd(v, m_vec))
            out["vb"].append(jnp.concatenate([v, bb * e_end, k2[i] * e_end], axis=0).astype(BF16))
            yield

    def prepare(ci):
        out = {key: [] for key in keys}
        for _ in prepare_steps(ci, out):
            pass
        return out

    def chain(ci, p, side_work=None):
        def tick():
            if side_work is not None:
                next(side_work, None)

        rows = pl.ds(pl.multiple_of(ci * c, c), c)
        ar, bdv = p["ar"], p["bdv"]
        sbd = [Sbd[i] for i in ids]
        pb = [lax.dot_general(ar[i], p["bdb"][i], nt, preferred_element_type=F32) * causal for i in ids]
        pk = [(lax.dot_general(ar[i], p["bdk"][i], nt, preferred_element_type=F32) * causal).astype(BF16) for i in ids]
        tick()
        s0p = [lax.dot_general(ar[i], sbd[i].astype(BF16), nt, preferred_element_type=F32) for i in ids]
        kv = [mm(pk[i], bdv[i]) for i in ids]
        w = [s0p[i][:c] + kv[i][:c] for i in ids]
        tick()
        pw = [pb[i][:c] for i in ids]
        tinv = [eye + pw[i] for i in ids]
        levels = c.bit_length() - 1
        bdp = [_bd(pw[i], m_mat) for i in ids]
        pw = [mm(pw[i], bdp[i]) for i in ids]
        tick()
        for lvl in range(1, levels):
            bdp = [_bd(pw[i], m_mat) for i in ids]
            if lvl < levels - 1:
                res = [mm(jnp.concatenate([tinv[i], pw[i]], axis=0), bdp[i]) for i in ids]
                tinv = [tinv[i] + res[i][:c] for i in ids]
                pw = [res[i][c:] for i in ids]
            else:
                tinv = [tinv[i] + mm(tinv[i], bdp[i]) for i in ids]
            tick()
        u = [mm(tinv[i], _bd(w[i], m_vec)).astype(BF16) for i in ids]
        tick()
        y = [s0p[i][c:] + mm(pb[i][c:], _bd(u[i], m_vec)) + kv[i][c:] for i in ids]
        tick()
        upd = [_dot_tn(jnp.concatenate([u[i], p["vb"][i][:c]], axis=0), p["vb"][i][c:]) for i in ids]
        for i in ids:
            Sbd[i] = sbd[i] * p["gam"][i] + jnp.where(m_state, upd[i], 0.0)
        tick()
        ysum = _segsums(y, ones_bd)
        dlt = [y[i] - ysum[i] * (1.0 / N) for i in ids]
        vsum = _segsums([dlt[i] * dlt[i] for i in ids], ones_bd)
        for i in ids:
            yn = dlt[i] * lax.rsqrt(vsum[i] * (1.0 / N) + LN_X_EPS) * lnw_ref[:, lanes[i]] + lnb_ref[:, lanes[i]]
            y_ref[seqs[i], rows, lanes[i]] = ((yn + p["tail"][i][:c]) * p["tail"][i][c:]).astype(y_ref.dtype)
        if side_work is not None:
            for _ in side_work:
                pass

    if nchunk == 1:
        chain(0, prepare(0))
    else:
        def body(ci, p):
            nxt = {key: [] for key in keys}
            chain(ci, p, prepare_steps(jnp.minimum(ci + 1, nchunk - 1), nxt))
            return nxt

        lax.fori_loop(0, nchunk, body, prepare(0))

    @pl.when(j == pl.num_programs(2) - 1)
    def _():
        for i, (q, gi) in enumerate(chains):
            sbd = Sbd[i]
            for h in range(R_GROUP):
                S_ref[q, gi * R_GROUP + h] = sbd[h * N:(h + 1) * N, h * N:(h + 1) * N]


def _rwkv(r, k, v, lw, a, g, kkp, kap, rk, lnw, lnb, S0):
    G, R, D = r.shape
    c = math.gcd(R, CHUNK)
    GW = R_GROUP_W
    if R > SUBLANES:
        gs, ng, Rc = 1, 8, min(R, 256)
    else:
        gs, ng, Rc = 2, D // GW, R
    bw = ng * GW
    tok = pl.BlockSpec((gs, Rc, bw), lambda a_, b, j: (a_, j, b))
    par = pl.BlockSpec((1, bw), lambda a_, b, j: (0, b))
    st = pl.BlockSpec((gs, ng * R_GROUP, R_HEAD_DIM, R_HEAD_DIM), lambda a_, b, j: (a_, b, 0, 0))
    row = lambda p: p.reshape(1, D).astype(F32)
    return pl.pallas_call(
        functools.partial(_rwkv_body, c=c, nchunk=Rc // c, ng=ng, gs=gs),
        grid=(G // gs, D // bw, R // Rc),
        in_specs=[tok] * 6 + [par] * 5 + [st],
        out_specs=[tok, st],
        out_shape=[jax.ShapeDtypeStruct((G, R, D), BF16),
                   jax.ShapeDtypeStruct((G, R_HEADS, R_HEAD_DIM, R_HEAD_DIM), F32)],
        scratch_shapes=[pltpu.VMEM((gs * ng, GW, GW), F32)],
        compiler_params=_cparams("arbitrary", "arbitrary", "arbitrary"),
        name="rwkv7",
    )(r, k, v, lw, a, g, row(kkp), row(kap), row(rk), row(lnw), row(lnb), S0)


def _trunk(x, mod, m_C, m_n, m_m, m_conv, g_S, r_S, r_shift, p, lbs, tm, w2_bf16=None):
    make_w2 = w2_bf16 is None
    w2_bf16 = [] if make_w2 else w2_bf16
    G, R, D = x.shape
    md = mod[0]
    h = _modnorm(x, p["norm_mix"][0], md, 1, 0, tm)
    w_in_t = jnp.swapaxes(p["ab_w_in"], 1, 2)
    zm = _mm(h, w_in_t, 0, n_cols=4 * M_WIDTH, tm=tm, tn=1024, name="mm_in_m", w_transposed=True)
    zgate = _mm(h, w_in_t, 0, n_cols=LANES, tm=tm, tn=LANES, name="mm_in_gate", w_transposed=True, col0=4 * M_WIDTH)
    zg = _mm(h, w_in_t, 0, n_cols=4 * G_WIDTH, tm=tm, tn=1024, name="mm_in_g", w_transposed=True,
             col0=4 * M_WIDTH + 2 * M_HEADS)
    hm, C, n, m = _mlstm(zm, zgate, p["m_conv_w"][0], p["ab_gate_b"][0], p["m_norm"][0],
                         m_conv[0], m_C[0], m_n[0], m_m[0])
    conv_new = zm[:, R - (M_CONV - 1):, :2 * M_WIDTH]
    og, S = _hgrn(zg, lbs[0], p["g_norm"][0], g_S[0])
    x = _mm_residual([hm, og], p["ab_w_out"], 0, x, md, 2, tm=tm, tn=512, name="mm_out0")
    h = _modnorm(x, p["norm_ffn"][0], md, 4, 3, tm)
    act = _mm(h, p["ffn_w1"], 0, n_cols=4 * D, tm=tm, tn=1024, act="relu2", out_dtype=BF16, name="ffn_up",
              cast_src=p["ffn_w2"] if make_w2 else None)
    if make_w2:
        act, w2 = act
        w2_bf16.append(w2[None])
    x = _mm_residual([act], w2_bf16[0], 0, x, md, 5, tm=256, tn=1024, name="ffn_down")
    md = mod[1]
    (xr, xw, xk, xv, xa, xg), shift_new = _rwkv_prep(x, p["norm_mix"][1], md, 1, 0, r_shift[0], p["r_mu"][0],
                                                     min(tm, 256))
    r = _mm(xr, p["r_wr"], 0, n_cols=D, tm=tm, tn=1024, name="mm_r")
    k = _mm(xk, p["r_wk"], 0, n_cols=D, tm=tm, tn=1024, name="mm_k")
    v = _mm(xv, p["r_wv"], 0, n_cols=D, tm=tm, tn=1024, name="mm_v")
    lw = _lora(xw, p["r_w1"][0], p["r_w2"][0], "tanh", min(tm, 512), post="log_decay", bias=p["r_w0"][0])
    aa = _lora(xa, p["r_a1"][0], p["r_a2"][0], None, min(tm, 512), post="sigmoid", bias=p["r_a0"][0])
    gg = _lora(xg, p["r_g1"][0], p["r_g2"][0], "sigmoid", min(tm, 512))
    yg, rS = _rwkv(r, k, v, lw, aa, gg, p["r_kk"][0], p["r_ka"][0], p["r_rk"][0].reshape(-1), p["r_lnw"][0],
                   p["r_lnb"][0], r_S[0])
    x = _mm_residual([yg], p["r_wo"], 0, x, md, 2, tm=tm, tn=1024, name="mm_out1")
    h = _modnorm(x, p["norm_ffn"][1], md, 4, 3, tm)
    act = _mm(h, p["ffn_w1"], 1, n_cols=4 * D, tm=tm, tn=1024, act="relu2", out_dtype=BF16, name="ffn_up",
              cast_src=p["ffn_w2"] if make_w2 else None)
    if make_w2:
        act, w2 = act
        w2_bf16.append(w2[None])
    x = _mm_residual([act], w2_bf16[1], 0, x, md, 5, tm=256, tn=1024, name="ffn_down")
    y = _rmsnorm(x, p["final_norm"], min(tm, 512))
    return y, (C[None], n[None], m.reshape(1, G, M_HEADS), conv_new[None], S[None], rS[None],
               shift_new.reshape(1, G, D)), w2_bf16


def kernel(x_prompt, x_sample, c_prompt, c_sample, state_mlstm_C, state_mlstm_n, state_mlstm_m, state_mlstm_conv, state_hgrn_S, state_rwkv_S, state_rwkv_shift, mod_w, mod_b, norm_mix, norm_ffn, ffn_w1, ffn_w2, final_norm, ab_w_in, ab_gate_b, m_conv_w, m_norm, g_lb, g_norm, ab_w_out, r_mu, r_w0, r_w1, r_w2, r_a0, r_a1, r_a2, r_g1, r_g2, r_kk, r_ka, r_rk, r_wr, r_wk, r_wv, r_wo, r_lnw, r_lnb):
    p = dict(norm_mix=norm_mix, norm_ffn=norm_ffn, ffn_w1=ffn_w1, ffn_w2=ffn_w2, final_norm=final_norm,
             ab_w_in=ab_w_in, ab_gate_b=ab_gate_b, m_conv_w=m_conv_w, m_norm=m_norm, g_norm=g_norm,
             ab_w_out=ab_w_out, r_mu=r_mu, r_w0=r_w0, r_w1=r_w1, r_w2=r_w2, r_a0=r_a0, r_a1=r_a1, r_a2=r_a2,
             r_g1=r_g1, r_g2=r_g2, r_kk=r_kk, r_ka=r_ka, r_rk=r_rk, r_wr=r_wr, r_wk=r_wk, r_wv=r_wv,
             r_wo=r_wo, r_lnw=r_lnw, r_lnb=r_lnb)
    B, T, D = x_prompt.shape
    Bs = x_sample.shape[0]
    depth = mod_w.shape[0]
    pad = (-(Bs + B)) % SUBLANES
    c_all = jnp.concatenate([c_sample, c_prompt, jnp.zeros((pad, D), F32)], axis=0)
    mod = _modulation(c_all, mod_w, mod_b)
    mod_s = mod[:, :Bs].reshape(depth, Bs, 1, N_MOD * D)
    mod_p = mod[:, Bs:Bs + B].reshape(depth, B, 1, N_MOD * D)
    lbs = jnp.cumsum(jax.nn.softmax(g_lb.astype(F32), axis=0), axis=0)

    n_even = state_mlstm_C.shape[0]
    n_odd = state_rwkv_S.shape[0]
    z = lambda *s: jnp.zeros(s, F32)
    yp, sp, w2_bf16 = _trunk(x_prompt, mod_p,
                    z(n_even, B, M_HEADS, M_HEAD_DIM, M_HEAD_DIM), z(n_even, B, M_HEADS, M_HEAD_DIM),
                    z(n_even, B, M_HEADS), z(n_even, B, M_CONV - 1, 2 * M_WIDTH),
                    z(n_even, B, G_HEADS, G_HEAD_DIM, G_HEAD_DIM), z(n_odd, B, R_HEADS, R_HEAD_DIM, R_HEAD_DIM),
                    z(n_odd, B, D), p, lbs, 1024)
    ys, ss, _ = _trunk(x_sample, mod_s, state_mlstm_C, state_mlstm_n, state_mlstm_m, state_mlstm_conv,
                       state_hgrn_S, state_rwkv_S, state_rwkv_shift, p, lbs, 1024, w2_bf16)
    return (yp, ys) + tuple(sp) + tuple(ss)
```

```python
import functools
import math

import jax
import jax.numpy as jnp
from jax import lax
from jax.experimental import pallas as pl
from jax.experimental.pallas import tpu as pltpu

F32 = jnp.float32
BF16 = jnp.bfloat16

D_MODEL = 2048
M_HEADS = 4
M_HEAD_DIM = 256
M_WIDTH = M_HEADS * M_HEAD_DIM
M_CONV = 4
G_HEADS = 8
G_HEAD_DIM = 128
G_WIDTH = G_HEADS * G_HEAD_DIM
R_HEAD_DIM = 64
R_HEADS = D_MODEL // R_HEAD_DIM
R_GROUP = 4
R_GROUP_W = R_GROUP * R_HEAD_DIM
N_MOD = 6
RMS_EPS = 1e-6
LN_X_EPS = 64e-5
CHUNK = 64
NEG_BIG = -1e30

V7X_VMEM_LIMIT_BYTES = 56 * 1024 * 1024
SUBLANES = 8
LANES = 128


def _cparams(*sem):
    return pltpu.CompilerParams(dimension_semantics=sem, vmem_limit_bytes=V7X_VMEM_LIMIT_BYTES)


def _sigmoid(x):
    return 1.0 / (1.0 + jnp.exp(-x))


def _silu(x):
    return x * _sigmoid(x)


def _softplus(x):
    return jnp.maximum(x, 0.0) + jnp.log1p(jnp.exp(-jnp.abs(x)))


def _dot(a, b):
    return jnp.dot(a.astype(BF16), b.astype(BF16), preferred_element_type=F32)


def _dot_nt(a, b):
    return lax.dot_general(a.astype(BF16), b.astype(BF16), (((1,), (1,)), ((), ())), preferred_element_type=F32)


def _dot_tn(a, b):
    return lax.dot_general(a.astype(BF16), b.astype(BF16), (((0,), (0,)), ((), ())), preferred_element_type=F32)


def _cumsum_rows(x):
    n = x.shape[0]
    row = lax.broadcasted_iota(jnp.int32, x.shape, 0)
    s = 1
    while s < n:
        x = x + jnp.where(row >= s, pltpu.roll(x, s, axis=0), 0.0)
        s *= 2
    return x


def _row_blocking(G, R, tm):
    if R >= tm:
        assert R % tm == 0
        return 1, tm, R // tm
    assert tm % R == 0 and G % (tm // R) == 0
    return tm // R, R, 1


def _mod_body(c_ref, w_ref, b_ref, o_ref):
    sc = _silu(c_ref[...])
    o_ref[...] = _dot(sc, w_ref[...]) + b_ref[...]


def _modulation(c_all, mod_w, mod_b):
    L, K, N = mod_w.shape
    Mc = c_all.shape[0]
    tn = 1024
    return pl.pallas_call(
        _mod_body,
        grid=(L, N // tn),
        in_specs=[pl.BlockSpec((Mc, K), lambda l, j: (0, 0)),
                  pl.BlockSpec((None, K, tn), lambda l, j: (l, 0, j)),
                  pl.BlockSpec((None, 1, tn), lambda l, j: (l, 0, j))],
        out_specs=pl.BlockSpec((None, Mc, tn), lambda l, j: (l, 0, j)),
        out_shape=jax.ShapeDtypeStruct((L, Mc, N), F32),
        compiler_params=_cparams("arbitrary", "arbitrary"),
        name="modulation",
    )(c_all, mod_w, mod_b.reshape(L, 1, N))


def _rms(x, g):
    ms = jnp.mean(x * x, axis=-1, keepdims=True)
    return x * lax.rsqrt(ms + RMS_EPS) * g


STRIP_ROWS = 2 * SUBLANES


def _strips(gb, rb):
    if gb == 1:
        n = rb // STRIP_ROWS
        tok = lambda i: (slice(None), pl.ds(pl.multiple_of(i * STRIP_ROWS, STRIP_ROWS), STRIP_ROWS), slice(None))
        seq = lambda i: (slice(None), slice(None), slice(None))
    else:
        assert STRIP_ROWS % rb == 0
        per = STRIP_ROWS // rb
        n = gb // per
        tok = seq = lambda i: (pl.ds(i * per, per), slice(None), slice(None))
    return n, tok, seq


def _modnorm_body(x_ref, g_ref, sc_ref, sh_ref, o_ref, *, gb, rb):
    n, tok, seq = _strips(gb, rb)

    def strip(i, carry):
        y = _rms(x_ref[tok(i)], g_ref[...])
        o_ref[tok(i)] = (y * (1.0 + sc_ref[seq(i)]) + sh_ref[seq(i)]).astype(o_ref.dtype)
        return carry

    lax.fori_loop(0, n, strip, 0, unroll=4)


def _modnorm(x, g, mod, i_scale, i_shift, tm):
    G, R, D = x.shape
    gb, rb, nrb = _row_blocking(G, R, tm)
    return pl.pallas_call(
        functools.partial(_modnorm_body, gb=gb, rb=rb),
        grid=(G // gb, nrb),
        in_specs=[pl.BlockSpec((gb, rb, D), lambda a, b: (a, b, 0)),
                  pl.BlockSpec((1, 1, D), lambda a, b: (0, 0, 0)),
                  pl.BlockSpec((gb, 1, D), lambda a, b: (a, 0, i_scale)),
                  pl.BlockSpec((gb, 1, D), lambda a, b: (a, 0, i_shift))],
        out_specs=pl.BlockSpec((gb, rb, D), lambda a, b: (a, b, 0)),
        out_shape=jax.ShapeDtypeStruct((G, R, D), BF16),
        compiler_params=_cparams("arbitrary", "arbitrary"),
        name="modnorm",
    )(x, g.reshape(1, 1, D), mod, mod)


def _rmsnorm_body(x_ref, g_ref, o_ref, *, gb, rb):
    n, tok, _ = _strips(gb, rb)

    def strip(i, carry):
        o_ref[tok(i)] = _rms(x_ref[tok(i)], g_ref[...])
        return carry

    lax.fori_loop(0, n, strip, 0, unroll=4)


def _rmsnorm(x, g, tm):
    G, R, D = x.shape
    gb, rb, nrb = _row_blocking(G, R, tm)
    return pl.pallas_call(
        functools.partial(_rmsnorm_body, gb=gb, rb=rb),
        grid=(G // gb, nrb),
        in_specs=[pl.BlockSpec((gb, rb, D), lambda a, b: (a, b, 0)),
                  pl.BlockSpec((1, 1, D), lambda a, b: (0, 0, 0))],
        out_specs=pl.BlockSpec((gb, rb, D), lambda a, b: (a, b, 0)),
        out_shape=jax.ShapeDtypeStruct((G, R, D), F32),
        compiler_params=_cparams("arbitrary", "arbitrary"),
        name="final_norm",
    )(x, g.reshape(1, 1, D))


def _mm_body(a_ref, w_ref, *rest, act, w_transposed, side_cast):
    if side_cast:
        src_ref, o_ref, dst_ref, wbf = rest
        dst_ref[...] = src_ref[...].astype(BF16)
    else:
        o_ref, wbf = rest

    @pl.when(pl.program_id(1) == 0)
    def _():
        wbf[...] = (w_ref[0].T if w_transposed else w_ref[...]).astype(BF16)

    acc = jnp.dot(a_ref[...].astype(BF16), wbf[...], preferred_element_type=F32)
    if act == "relu2":
        acc = jnp.square(jnp.maximum(acc, 0.0))
    o_ref[...] = acc.astype(o_ref.dtype)


def _mm(a, w, layer, *, n_cols, tm, tn, act=None, out_dtype=F32, name="mm", w_transposed=False, col0=0,
        cast_src=None):
    G, R, K = a.shape
    M = G * R
    assert M % tm == 0 and n_cols % tn == 0 and w.shape[2 if w_transposed else 1] == K
    nm = M // tm
    if w_transposed:
        assert col0 % SUBLANES == 0
        w_spec = pl.BlockSpec((pl.Element(1), pl.Element(tn), pl.Element(K)),
                              lambda j, i: (layer, pl.multiple_of(col0 + j * tn, SUBLANES), 0))
    else:
        assert col0 == 0
        w_spec = pl.BlockSpec((None, K, tn), lambda j, i: (layer, 0, j))
    in_specs = [pl.BlockSpec((tm, K), lambda j, i: (i, 0)), w_spec]
    out_specs = [pl.BlockSpec((tm, tn), lambda j, i: (i, j))]
    out_shape = [jax.ShapeDtypeStruct((M, n_cols), out_dtype)]
    args = [a.reshape(M, K), w]
    if cast_src is not None:
        _, P, Q = cast_src.shape
        steps = (n_cols // tn) * nm
        assert P % steps == 0
        in_specs.append(pl.BlockSpec((None, P // steps, Q), lambda j, i: (layer, j * nm + i, 0)))
        out_specs.append(pl.BlockSpec((P // steps, Q), lambda j, i: (j * nm + i, 0)))
        out_shape.append(jax.ShapeDtypeStruct((P, Q), BF16))
        args.append(cast_src)
    res = pl.pallas_call(
        functools.partial(_mm_body, act=act, w_transposed=w_transposed, side_cast=cast_src is not None),
        grid=(n_cols // tn, nm),
        in_specs=in_specs,
        out_specs=out_specs,
        out_shape=out_shape,
        scratch_shapes=[pltpu.VMEM((K, tn), BF16)],
        compiler_params=_cparams("arbitrary", "arbitrary"),
        name=name,
    )(*args)
    out = res[0].reshape(G, R, n_cols)
    return (out, res[1]) if cast_src is not None else out


def _mm_res_body(*refs, n_lhs, gb, rb, cast_w):
    a_refs = refs[:n_lhs]
    w_ref, x_ref, gt_ref, o_ref = refs[n_lhs:n_lhs + 4]
    if cast_w:
        wbf = refs[n_lhs + 4]

        @pl.when(pl.program_id(1) == 0)
        def _():
            wbf[...] = w_ref[...].astype(BF16)
    else:
        wbf = w_ref

    acc = None
    k0 = 0
    for a_ref in a_refs:
        kk = a_ref.shape[-1]
        part = jnp.dot(a_ref[...].astype(BF16), wbf[k0:k0 + kk, :], preferred_element_type=F32)
        acc = part if acc is None else acc + part
        k0 += kk
    tn = acc.shape[-1]
    o_ref[...] = x_ref[...] + gt_ref[...] * acc.reshape(gb, rb, tn)


def _mm_residual(a_list, w, layer, x, mod, i_gate, *, tm, tn, name="mm_res"):
    G, R, N = x.shape
    M = G * R
    K = w.shape[1]
    assert sum(a.shape[-1] for a in a_list) == K and w.shape[2] == N
    gb, rb, nrb = _row_blocking(G, R, tm)
    ntn = N // tn
    cast_w = w.dtype != BF16

    def xmap(j, i):
        return (i // nrb, i % nrb, j)

    return pl.pallas_call(
        functools.partial(_mm_res_body, n_lhs=len(a_list), gb=gb, rb=rb, cast_w=cast_w),
        grid=(ntn, M // tm),
        in_specs=[pl.BlockSpec((tm, a.shape[-1]), lambda j, i: (i, 0)) for a in a_list]
        + [pl.BlockSpec((None, K, tn), lambda j, i: (layer, 0, j)),
           pl.BlockSpec((gb, rb, tn), xmap),
           pl.BlockSpec((gb, 1, tn), lambda j, i: (i // nrb, 0, i_gate * ntn + j))],
        out_specs=pl.BlockSpec((gb, rb, tn), xmap),
        out_shape=jax.ShapeDtypeStruct((G, R, N), F32),
        scratch_shapes=[pltpu.VMEM((K, tn), BF16)] if cast_w else [],
        compiler_params=_cparams("arbitrary", "arbitrary"),
        name=name,
    )(*[a.reshape(M, a.shape[-1]) for a in a_list], w, x, mod)


def _mlstm_body(zq_ref, zk_ref, zv_ref, zo_ref, zg_ref, cw_ref, gbias_ref, gain_ref, conv0_ref,
                C0_ref, n0_ref, m0_ref, hm_ref, C_ref, n_ref, m_ref, ext, *, L, gs):
    W = M_WIDTH
    Dh = M_HEAD_DIM
    H = M_HEADS
    keep = M_CONV - 1
    c = pl.program_id(1)

    @pl.when(c == 0)
    def _():
        C_ref[...] = C0_ref[...]
        n_ref[...] = n0_ref[...]
        m_ref[...] = m0_ref[...]
        ext[:, SUBLANES - keep:SUBLANES, :] = conv0_ref[...]

    ext[:, SUBLANES:SUBLANES + L, 0:W] = zq_ref[...]
    ext[:, SUBLANES:SUBLANES + L, W:2 * W] = zk_ref[...]
    row = lax.broadcasted_iota(jnp.int32, (L, L), 0)
    col = lax.broadcasted_iota(jnp.int32, (L, L), 1)
    eye = row == col
    tri = col <= row
    chains = [(g, h) for g in range(gs) for h in range(H)]
    ids = range(len(chains))
    qk, gates = [], []
    for g in range(gs):
        conv = ext[g, SUBLANES:SUBLANES + L, :] * cw_ref[keep:keep + 1, :]
        for j in range(keep):
            conv = conv + ext[g, SUBLANES - keep + j:SUBLANES - keep + j + L, :] * cw_ref[j:j + 1, :]
        qk.append(_silu(conv))
        gates.append(zg_ref[g] + gbias_ref[...])
    ext[:, SUBLANES - keep:SUBLANES, :] = ext[:, SUBLANES + L - keep:SUBLANES + L, :]

    sl = [slice(h * Dh, (h + 1) * Dh) for _, h in chains]
    q = [qk[g][:, h * Dh:(h + 1) * Dh] for g, h in chains]
    k = [qk[g][:, W + h * Dh:W + (h + 1) * Dh] * (Dh ** -0.5) for g, h in chains]
    v = [zv_ref[g, :, sl[i]] for i, (g, h) in enumerate(chains)]
    ig_col = [gates[g][:, h:h + 1] for g, h in chains]
    fpre = [gates[g][:, H + h:H + h + 1] for g, h in chains]
    lf_col = [jnp.minimum(x, 0.0) - jnp.log1p(jnp.exp(-jnp.abs(x))) for x in fpre]
    lf_row = [jnp.sum(jnp.where(eye, x, 0.0), axis=0, keepdims=True) for x in lf_col]
    ig_row = [jnp.sum(jnp.where(eye, x, 0.0), axis=0, keepdims=True) for x in ig_col]
    b_col = [jnp.sum(jnp.where(tri, x, 0.0), axis=1, keepdims=True) for x in lf_row]
    b_row = [jnp.sum(jnp.where(row <= col, x, 0.0), axis=0, keepdims=True) for x in lf_col]
    dmat = [jnp.where(tri, b_col[i] - b_row[i] + ig_row[i], NEG_BIG) for i in ids]
    m_prev = [m_ref[g, :, h:h + 1] for g, h in chains]
    inter = [b_col[i] + m_prev[i] for i in ids]
    m_t = [jnp.maximum(inter[i], jnp.max(dmat[i], axis=1, keepdims=True)) for i in ids]
    w_inter = [jnp.exp(inter[i] - m_t[i]) for i in ids]
    Cm = [C_ref[g, h] for g, h in chains]
    n_row = [n_ref[g, h:h + 1, :] for g, h in chains]
    s = [_dot_nt(q[i], k[i]) * jnp.exp(dmat[i] - m_t[i]) for i in ids]
    qc = [_dot(q[i], Cm[i]) for i in ids]
    num = [_dot(s[i], v[i]) + w_inter[i] * qc[i] for i in ids]
    den = [jnp.sum(s[i], axis=1, keepdims=True) + w_inter[i] * jnp.sum(q[i] * n_row[i], axis=1, keepdims=True)
           for i in ids]
    hh = [num[i] / jnp.maximum(jnp.abs(den[i]), jnp.exp(-m_t[i])) for i in ids]
    m_new = [x[L - 1:L, :] for x in m_t]
    b_last = [x[L - 1:L, :] for x in b_col]
    wk = [jnp.exp(b_last[i] - b_col[i] + ig_col[i] - m_new[i]) * k[i] for i in ids]
    decay = [jnp.exp(b_last[i] + m_prev[i] - m_new[i]) for i in ids]
    upd = [_dot_tn(wk[i], v[i]) for i in ids]
    for i, (g, h) in enumerate(chains):
        C_ref[g, h] = decay[i] * Cm[i] + upd[i]
        n_ref[g, h:h + 1, :] = decay[i] * n_row[i] + jnp.sum(wk[i], axis=0, keepdims=True)
        m_ref[g, :, h:h + 1] = m_new[i]
        dlt = hh[i] - jnp.mean(hh[i], axis=-1, keepdims=True)
        ln = dlt * lax.rsqrt(jnp.mean(dlt * dlt, axis=-1, keepdims=True) + RMS_EPS)
        hm_ref[g, :, sl[i]] = _sigmoid(zo_ref[g, :, sl[i]]) * ln * gain_ref[:, sl[i]]


def _mlstm(zm, zgate, conv_w, gate_b, m_gain, conv0, C0, n0, m0):
    G, R, _ = zm.shape
    L = math.gcd(R, CHUNK)
    W = M_WIDTH
    gs = 2 if R > SUBLANES else 4
    gbias = jnp.zeros((1, LANES), F32).at[0, :2 * M_HEADS].set(gate_b.astype(F32))
    zspec = lambda blk: pl.BlockSpec((gs, L, W), lambda g, c: (g, c, blk))
    st4 = lambda g, c: (g, 0, 0, 0)
    st3 = lambda g, c: (g, 0, 0)
    return pl.pallas_call(
        functools.partial(_mlstm_body, L=L, gs=gs),
        grid=(G // gs, R // L),
        in_specs=[zspec(0), zspec(1), zspec(2), zspec(3),
                  pl.BlockSpec((gs, L, LANES), lambda g, c: (g, c, 0)),
                  pl.BlockSpec((M_CONV, 2 * W), lambda g, c: (0, 0)),
                  pl.BlockSpec((1, LANES), lambda g, c: (0, 0)),
                  pl.BlockSpec((1, W), lambda g, c: (0, 0)),
                  pl.BlockSpec((gs, M_CONV - 1, 2 * W), st3),
                  pl.BlockSpec((gs, M_HEADS, M_HEAD_DIM, M_HEAD_DIM), st4),
                  pl.BlockSpec((gs, M_HEADS, M_HEAD_DIM), st3),
                  pl.BlockSpec((gs, 1, M_HEADS), st3)],
        out_specs=[pl.BlockSpec((gs, L, W), lambda g, c: (g, c, 0)),
                   pl.BlockSpec((gs, M_HEADS, M_HEAD_DIM, M_HEAD_DIM), st4),
                   pl.BlockSpec((gs, M_HEADS, M_HEAD_DIM), st3),
                   pl.BlockSpec((gs, 1, M_HEADS), st3)],
        out_shape=[jax.ShapeDtypeStruct((G, R, W), F32),
                   jax.ShapeDtypeStruct((G, M_HEADS, M_HEAD_DIM, M_HEAD_DIM), F32),
                   jax.ShapeDtypeStruct((G, M_HEADS, M_HEAD_DIM), F32),
                   jax.ShapeDtypeStruct((G, 1, M_HEADS), F32)],
        scratch_shapes=[pltpu.VMEM((gs, SUBLANES + L, 2 * W), F32)],
        compiler_params=_cparams("arbitrary", "arbitrary"),
        name="mlstm",
    )(zm, zm, zm, zm, zgate, conv_w, gbias, m_gain.reshape(1, W), conv0, C0, n0, m0.reshape(G, 1, M_HEADS))


def _hgrn_body(zq_ref, zf_ref, zi_ref, zgg_ref, lb_ref, gain_ref, S0_ref, og_ref, S_ref, ST, *, nblk, gs, st_t, nh):
    Dh = G_HEAD_DIM
    B = SUBLANES
    c = pl.program_id(1)
    chains = [(g, h) for g in range(gs) for h in range(G_HEADS)]

    @pl.when(c == 0)
    def _():
        for i, (g, h) in enumerate(chains):
            ST[i] = S0_ref[g, h].T if st_t else S0_ref[g, h]

    rowi = lax.broadcasted_iota(jnp.int32, (B, Dh), 0)
    ids = range(len(chains))
    sl = [slice(h * Dh, (h + 1) * Dh) for _, h in chains]
    gi = [g for g, _ in chains]

    def diag(bc, qh, kk, v, o):
        for s in range(B):
            p = [jnp.exp(jnp.where(rowi >= s, bc[i] - bc[i][s:s + 1, :], NEG_BIG)) * qh[i] * kk[i][s:s + 1, :]
                 for i in ids]
            o = [o[i] + jnp.sum(p[i], axis=-1, keepdims=True) * v[i][s:s + 1, :] for i in ids]
        return o

    def blk(bi, carry):
        r0 = pl.multiple_of(bi * (nh * B), nh * B)
        st = [ST[i] for i in ids]
        f, kk, qh, v, bc = [], [], [], [], []
        for hf in range(nh):
            rows = pl.ds(r0 + hf * B, B)
            f_h = [lb_ref[:, sl[i]] + (1.0 - lb_ref[:, sl[i]]) * _sigmoid(zf_ref[gi[i], rows, sl[i]]) for i in ids]
            kk.append([1.0 - x for x in f_h])
            loc = [_cumsum_rows(jnp.log(x)) for x in f_h]
            bc.append(loc if hf == 0 else [loc[i] + bc[hf - 1][i][B - 1:B, :] for i in ids])
            qh.append([_silu(zq_ref[gi[i], rows, sl[i]]) * (Dh ** -0.5) for i in ids])
            v.append([zi_ref[gi[i], rows, sl[i]] for i in ids])
        btot = [bc[nh - 1][i][B - 1:B, :] for i in ids]
        cat = lambda parts, i: parts[0][i] if nh == 1 else jnp.concatenate([pt[i] for pt in parts], axis=0)
        qd = [cat([[qh[hf][i] * jnp.exp(bc[hf][i]) for i in ids] for hf in range(nh)], i) for i in ids]
        kd = [cat([[kk[hf][i] * jnp.exp(btot[i] - bc[hf][i]) for i in ids] for hf in range(nh)], i) for i in ids]
        vv = [cat(v, i) for i in ids]
        if st_t:
            os_ = [_dot_nt(qd[i], st[i]) for i in ids]
            upd = [_dot_tn(vv[i], kd[i]) for i in ids]
            dec = [jnp.exp(btot[i]) for i in ids]
        else:
            os_ = [_dot(qd[i], st[i]) for i in ids]
            upd = [_dot_tn(kd[i], vv[i]) for i in ids]
            dec = [jnp.transpose(jnp.broadcast_to(jnp.exp(btot[i]), (B, Dh)))[:, 0:1] for i in ids]
        o = []
        for hf in range(nh):
            oh = diag(bc[hf], qh[hf], kk[hf], v[hf], [os_[i][hf * B:(hf + 1) * B, :] for i in ids])
            for prev in range(hf):
                edge = [bc[prev][i][B - 1:B, :] for i in ids]
                ax = [_dot_nt(qh[hf][i] * jnp.exp(bc[hf][i] - edge[i]), kk[prev][i] * jnp.exp(edge[i] - bc[prev][i]))
                      for i in ids]
                oh = [oh[i] + _dot(ax[i], v[prev][i]) for i in ids]
            o.append(oh)
        for i in ids:
            ST[i] = st[i] * dec[i] + upd[i]
        for hf in range(nh):
            rows = pl.ds(r0 + hf * B, B)
            for i in ids:
                oi = o[hf][i]
                rms = oi * lax.rsqrt(jnp.mean(oi * oi, axis=-1, keepdims=True) + RMS_EPS)
                og_ref[gi[i], rows, sl[i]] = rms * gain_ref[:, sl[i]] * _silu(zgg_ref[gi[i], rows, sl[i]])
        return carry

    lax.fori_loop(0, nblk, blk, 0)

    @pl.when(c == pl.num_programs(1) - 1)
    def _():
        for i, (g, h) in enumerate(chains):
            S_ref[g, h] = ST[i].T if st_t else ST[i]


def _hgrn(zg, lb, g_gain, S0):
    G, R, _ = zg.shape
    W = G_WIDTH
    gs = 4
    Rc = min(R, 128)
    nh = 4 if R > SUBLANES else 1
    zspec = lambda blk: pl.BlockSpec((gs, Rc, W), lambda g, c: (g, c, blk))
    st4 = lambda g, c: (g, 0, 0, 0)
    return pl.pallas_call(
        functools.partial(_hgrn_body, nblk=Rc // (nh * SUBLANES), gs=gs, st_t=R > SUBLANES, nh=nh),
        grid=(G // gs, R // Rc),
        in_specs=[zspec(0), zspec(1), zspec(2), zspec(3),
                  pl.BlockSpec((1, W), lambda g, c: (0, 0)),
                  pl.BlockSpec((1, W), lambda g, c: (0, 0)),
                  pl.BlockSpec((gs, G_HEADS, G_HEAD_DIM, G_HEAD_DIM), st4)],
        out_specs=[pl.BlockSpec((gs, Rc, W), lambda g, c: (g, c, 0)),
                   pl.BlockSpec((gs, G_HEADS, G_HEAD_DIM, G_HEAD_DIM), st4)],
        out_shape=[jax.ShapeDtypeStruct((G, R, W), F32),
                   jax.ShapeDtypeStruct((G, G_HEADS, G_HEAD_DIM, G_HEAD_DIM), F32)],
        scratch_shapes=[pltpu.VMEM((gs * G_HEADS, G_HEAD_DIM, G_HEAD_DIM), F32)],
        compiler_params=_cparams("arbitrary", "arbitrary"),
        name="hgrn2",
    )(zg, zg, zg, zg, lb.reshape(1, W), g_gain.reshape(1, W), S0)


def _rprep_body(x_ref, g_ref, sc_ref, sh_ref, shift0_ref, mu_ref, *refs, gb, rb):
    outs = refs[:6]
    hlast_ref, hbuf = refs[6:]
    B = SUBLANES
    n, tok, seq = _strips(gb, rb)
    S = hbuf.shape[1] - B
    whole_sequences = gb > 1

    if not whole_sequences:
        @pl.when(pl.program_id(1) == 0)
        def _():
            hbuf[:, B - 1:B, :] = shift0_ref[...]

    first = lax.broadcasted_iota(jnp.int32, (hbuf.shape[0], S, 1), 1) == 0

    def strip(i, carry):
        h = _rms(x_ref[tok(i)], g_ref[...]) * (1.0 + sc_ref[seq(i)]) + sh_ref[seq(i)]
        before = shift0_ref[seq(i)] if whole_sequences else hbuf[:, B - 1:B, :]
        xx = jnp.where(first, before, pltpu.roll(h, 1, axis=1)) - h
        last = h[:, S - 1:S, :]
        if whole_sequences:
            hlast_ref[seq(i)] = last
        else:
            hbuf[:, B - 1:B, :] = last
        for m in range(6):
            outs[m][tok(i)] = (h + xx * mu_ref[m:m + 1, :]).astype(BF16)
        return carry

    lax.fori_loop(0, n, strip, 0, unroll=4)
    if not whole_sequences:
        hlast_ref[...] = hbuf[:, B - 1:B, :]


def _rwkv_prep(x, g, mod, i_scale, i_shift, shift0, mu, tm):
    G, R, D = x.shape
    gb, rb, nrb = _row_blocking(G, R, tm)
    tok = pl.BlockSpec((gb, rb, D), lambda a, b: (a, b, 0))
    one = pl.BlockSpec((gb, 1, D), lambda a, b: (a, 0, 0))
    strip_shape = (1, SUBLANES + STRIP_ROWS, D) if gb == 1 else (STRIP_ROWS // rb, SUBLANES + rb, D)
    res = pl.pallas_call(
        functools.partial(_rprep_body, gb=gb, rb=rb),
        grid=(G // gb, nrb),
        in_specs=[tok,
                  pl.BlockSpec((1, 1, D), lambda a, b: (0, 0, 0)),
                  pl.BlockSpec((gb, 1, D), lambda a, b: (a, 0, i_scale)),
                  pl.BlockSpec((gb, 1, D), lambda a, b: (a, 0, i_shift)),
                  one,
                  pl.BlockSpec((6, D), lambda a, b: (0, 0))],
        out_specs=[tok] * 6 + [one],
        out_shape=[jax.ShapeDtypeStruct((G, R, D), BF16)] * 6 + [jax.ShapeDtypeStruct((G, 1, D), F32)],
        scratch_shapes=[pltpu.VMEM(strip_shape, F32)],
        compiler_params=_cparams("arbitrary", "arbitrary"),
        name="rwkv_prep",
    )(x, g.reshape(1, 1, D), mod, mod, shift0.reshape(G, 1, D), mu)
    return res[:6], res[6]


def _lora_body(x_ref, w1_ref, w2_ref, b_ref, o_ref, *, act, post):
    t = _dot(x_ref[...], w1_ref[...])
    if act == "tanh":
        t = jnp.tanh(t)
    elif act == "sigmoid":
        t = _sigmoid(t)
    y = _dot(t, w2_ref[...])
    if post == "log_decay":
        y = -jnp.exp(-_softplus(-(b_ref[...] + y)) - 0.5)
    elif post == "sigmoid":
        y = _sigmoid(b_ref[...] + y)
    o_ref[...] = y


def _lora(x, w1, w2, act, tm, post=None, bias=None):
    G, R, K = x.shape
    M = G * R
    r = w1.shape[1]
    rp = -(-r // LANES) * LANES
    w1p = jnp.zeros((K, rp), F32).at[:, :r].set(w1)
    w2p = jnp.zeros((rp, w2.shape[1]), F32).at[:r, :].set(w2)
    N = w2.shape[1]
    b = jnp.zeros((1, N), F32) if bias is None else bias.reshape(1, N).astype(F32)
    out = pl.pallas_call(
        functools.partial(_lora_body, act=act, post=post),
        grid=(M // tm,),
        in_specs=[pl.BlockSpec((tm, K), lambda i: (i, 0)),
                  pl.BlockSpec((K, rp), lambda i: (0, 0)),
                  pl.BlockSpec((rp, N), lambda i: (0, 0)),
                  pl.BlockSpec((1, N), lambda i: (0, 0))],
        out_specs=pl.BlockSpec((tm, N), lambda i: (i, 0)),
        out_shape=jax.ShapeDtypeStruct((M, N), F32),
        compiler_params=_cparams("arbitrary"),
        name="lora_" + str(act),
    )(x.reshape(M, K), w1p, w2p, b)
    return out.reshape(G, R, N)


def _head_mask(rows, cols, rper, cper):
    r = lax.broadcasted_iota(jnp.int32, (rows, cols), 0)
    c = lax.broadcasted_iota(jnp.int32, (rows, cols), 1)
    return (r // rper) == (c // cper)


def _bd(y, mask01):
    if y.shape[0] % (2 * SUBLANES) == 0:
        return jnp.concatenate([y.astype(BF16)] * R_GROUP, axis=0) * mask01
    return jnp.concatenate([y] * R_GROUP, axis=0).astype(BF16) * mask01


def _segsums(xs, ones_bd):
    c = xs[0].shape[0]
    if c % (2 * SUBLANES) == 0:
        lhs = jnp.concatenate([x.astype(BF16) for x in xs], axis=0)
    else:
        lhs = jnp.concatenate(xs, axis=0).astype(BF16)
    res = jnp.dot(lhs, ones_bd, preferred_element_type=F32)
    return [res[i * c:(i + 1) * c] for i in range(len(xs))]


def _rwkv_body(r_ref, k_ref, v_ref, lw_ref, a_ref, g_ref, kkp_ref, kap_ref, rk_ref,
               lnw_ref, lnb_ref, S0_ref, y_ref, S_ref, Sbd, *, c, nchunk, ng, gs):
    N = R_HEAD_DIM
    GW = R_GROUP_W
    j = pl.program_id(2)
    m_state = _head_mask(GW, GW, N, N)
    chains = [(q, gi) for q in range(gs) for gi in range(ng)]
    ids = range(len(chains))
    lanes = [slice(gi * GW, (gi + 1) * GW) for _, gi in chains]
    seqs = [q for q, _ in chains]

    @pl.when(j == 0)
    def _():
        for i, (q, gi) in enumerate(chains):
            s0 = S0_ref[q, gi * R_GROUP:(gi + 1) * R_GROUP].reshape(GW, N)
            Sbd[i] = jnp.where(m_state, jnp.concatenate([s0] * R_GROUP, axis=1), 0.0)

    ones_bd = jnp.where(m_state, 1.0, 0.0).astype(BF16)
    m_vec = jnp.where(_head_mask(R_GROUP * c, GW, c, N), 1.0, 0.0).astype(BF16)
    m_mat = jnp.where(_head_mask(R_GROUP * c, R_GROUP * c, c, c), 1.0, 0.0).astype(BF16)
    t_idx = lax.broadcasted_iota(jnp.int32, (2 * c, R_GROUP * c), 0)
    s_idx = lax.broadcasted_iota(jnp.int32, (2 * c, R_GROUP * c), 1) % c
    causal = jnp.where(t_idx < c, jnp.where(s_idx < t_idx, 1.0, 0.0), jnp.where(s_idx <= t_idx - c, 1.0, 0.0))
    nt = (((1,), (1,)), ((), ()))
    mm = lambda a, b: jnp.dot(a.astype(BF16), b, preferred_element_type=F32)

    tril = jnp.where(lax.broadcasted_iota(jnp.int32, (c, c), 1) <= lax.broadcasted_iota(jnp.int32, (c, c), 0),
                     1.0, 0.0).astype(BF16)
    eye = jnp.where(lax.broadcasted_iota(jnp.int32, (c, R_GROUP * c), 1) % c
                    == lax.broadcasted_iota(jnp.int32, (c, R_GROUP * c), 0), 1.0, 0.0)
    keys = ("ar", "bdb", "bdk", "bdv", "vb", "gam", "tail")

    def prepare_steps(ci, out):
        rows = pl.ds(pl.multiple_of(ci * c, c), c)
        ld = lambda ref: [ref[seqs[i], rows, lanes[i]] for i in ids]
        r, k, a = ld(r_ref), ld(k_ref), ld(a_ref)
        kk = [k[i] * kkp_ref[:, lanes[i]] for i in ids]
        k2 = [k[i] * (1.0 + (a[i] - 1.0) * kap_ref[:, lanes[i]]) for i in ids]
        sums = _segsums([kk[i] * kk[i] for i in ids] + [r[i] * k2[i] * rk_ref[:, lanes[i]] for i in ids], ones_bd)
        yield
        for i in ids:
            v, lw, g = v_ref[seqs[i], rows, lanes[i]], lw_ref[seqs[i], rows, lanes[i]], g_ref[seqs[i], rows, lanes[i]]
            hi = lw.astype(BF16)
            lo = (lw - hi.astype(F32)).astype(BF16)
            cw2 = jnp.dot(tril, jnp.concatenate([hi, lo], axis=1), preferred_element_type=F32)
            cw = cw2[:, :GW] + cw2[:, GW:]
            gam = jnp.exp(cw[c - 1:c, :])
            kn = kk[i] * lax.rsqrt(jnp.maximum(sums[i], 1e-24))
            bb = kn * a[i]
            e_out = jnp.exp(-cw)
            e_end = gam * e_out
            out["ar"].append(jnp.concatenate([-kn * jnp.exp(cw - lw), r[i] * jnp.exp(cw)], axis=0).astype(BF16))
            out["gam"].append(gam)
            out["tail"].append(jnp.concatenate([sums[len(chains) + i] * v, g], axis=0))
            yield
            out["bdb"].append(_bd(bb * e_out, m_vec))
            out["bdk"].append(_bd(k2[i] * e_out, m_vec))
            out["bdv"].append(_bd(v, m_vec))
            out["vb"].append(jnp.concatenate([v, bb * e_end, k2[i] * e_end], axis=0).astype(BF16))
            yield

    def prepare(ci):
        out = {key: [] for key in keys}
        for _ in prepare_steps(ci, out):
            pass
        return out

    def chain(ci, p, side_work=None):
        def tick():
            if side_work is not None:
                next(side_work, None)

        rows = pl.ds(pl.multiple_of(ci * c, c), c)
        ar, bdv = p["ar"], p["bdv"]
        sbd = [Sbd[i] for i in ids]
        pb = [lax.dot_general(ar[i], p["bdb"][i], nt, preferred_element_type=F32) * causal for i in ids]
        pk = [(lax.dot_general(ar[i], p["bdk"][i], nt, preferred_element_type=F32) * causal).astype(BF16) for i in ids]
        tick()
        s0p = [lax.dot_general(ar[i], sbd[i].astype(BF16), nt, preferred_element_type=F32) for i in ids]
        kv = [mm(pk[i], bdv[i]) for i in ids]
        w = [s0p[i][:c] + kv[i][:c] for i in ids]
        tick()
        pw = [pb[i][:c] for i in ids]
        tinv = [eye + pw[i] for i in ids]
        levels = c.bit_length() - 1
        bdp = [_bd(pw[i], m_mat) for i in ids]
        pw = [mm(pw[i], bdp[i]) for i in ids]
        tick()
        for lvl in range(1, levels):
            bdp = [_bd(pw[i], m_mat) for i in ids]
            if lvl < levels - 1:
                res = [mm(jnp.concatenate([tinv[i], pw[i]], axis=0), bdp[i]) for i in ids]
                tinv = [tinv[i] + res[i][:c] for i in ids]
                pw = [res[i][c:] for i in ids]
            else:
                tinv = [tinv[i] + mm(tinv[i], bdp[i]) for i in ids]
            tick()
        u = [mm(tinv[i], _bd(w[i], m_vec)).astype(BF16) for i in ids]
        tick()
        y = [s0p[i][c:] + mm(pb[i][c:], _bd(u[i], m_vec)) + kv[i][c:] for i in ids]
        tick()
        upd = [_dot_tn(jnp.concatenate([u[i], p["vb"][i][:c]], axis=0), p["vb"][i][c:]) for i in ids]
        for i in ids:
            Sbd[i] = sbd[i] * p["gam"][i] + jnp.where(m_state, upd[i], 0.0)
        tick()
        ysum = _segsums(y, ones_bd)
        dlt = [y[i] - ysum[i] * (1.0 / N) for i in ids]
        vsum = _segsums([dlt[i] * dlt[i] for i in ids], ones_bd)
        for i in ids:
            yn = dlt[i] * lax.rsqrt(vsum[i] * (1.0 / N) + LN_X_EPS) * lnw_ref[:, lanes[i]] + lnb_ref[:, lanes[i]]
            y_ref[seqs[i], rows, lanes[i]] = ((yn + p["tail"][i][:c]) * p["tail"][i][c:]).astype(y_ref.dtype)
        if side_work is not None:
            for _ in side_work:
                pass

    if nchunk == 1:
        chain(0, prepare(0))
    else:
        def body(ci, p):
            nxt = {key: [] for key in keys}
            chain(ci, p, prepare_steps(jnp.minimum(ci + 1, nchunk - 1), nxt))
            return nxt

        lax.fori_loop(0, nchunk, body, prepare(0))

    @pl.when(j == pl.num_programs(2) - 1)
    def _():
        for i, (q, gi) in enumerate(chains):
            sbd = Sbd[i]
            for h in range(R_GROUP):
                S_ref[q, gi * R_GROUP + h] = sbd[h * N:(h + 1) * N, h * N:(h + 1) * N]


def _rwkv(r, k, v, lw, a, g, kkp, kap, rk, lnw, lnb, S0):
    G, R, D = r.shape
    c = math.gcd(R, CHUNK)
    GW = R_GROUP_W
    if R > SUBLANES:
        gs, ng, Rc = 1, 8, min(R, 256)
    else:
        gs, ng, Rc = 2, D // GW, R
    bw = ng * GW
    tok = pl.BlockSpec((gs, Rc, bw), lambda a_, b, j: (a_, j, b))
    par = pl.BlockSpec((1, bw), lambda a_, b, j: (0, b))
    st = pl.BlockSpec((gs, ng * R_GROUP, R_HEAD_DIM, R_HEAD_DIM), lambda a_, b, j: (a_, b, 0, 0))
    row = lambda p: p.reshape(1, D).astype(F32)
    return pl.pallas_call(
        functools.partial(_rwkv_body, c=c, nchunk=Rc // c, ng=ng, gs=gs),
        grid=(G // gs, D // bw, R // Rc),
        in_specs=[tok] * 6 + [par] * 5 + [st],
        out_specs=[tok, st],
        out_shape=[jax.ShapeDtypeStruct((G, R, D), BF16),
                   jax.ShapeDtypeStruct((G, R_HEADS, R_HEAD_DIM, R_HEAD_DIM), F32)],
        scratch_shapes=[pltpu.VMEM((gs * ng, GW, GW), F32)],
        compiler_params=_cparams("arbitrary", "arbitrary", "arbitrary"),
        name="rwkv7",
    )(r, k, v, lw, a, g, row(kkp), row(kap), row(rk), row(lnw), row(lnb), S0)


def _trunk(x, mod, m_C, m_n, m_m, m_conv, g_S, r_S, r_shift, p, lbs, tm, w2_bf16=None):
    make_w2 = w2_bf16 is None
    w2_bf16 = [] if make_w2 else w2_bf16
    G, R, D = x.shape
    md = mod[0]
    h = _modnorm(x, p["norm_mix"][0], md, 1, 0, tm)
    w_in_t = jnp.swapaxes(p["ab_w_in"], 1, 2)
    zm = _mm(h, w_in_t, 0, n_cols=4 * M_WIDTH, tm=tm, tn=1024, name="mm_in_m", w_transposed=True)
    zgate = _mm(h, w_in_t, 0, n_cols=LANES, tm=tm, tn=LANES, name="mm_in_gate", w_transposed=True, col0=4 * M_WIDTH)
    zg = _mm(h, w_in_t, 0, n_cols=4 * G_WIDTH, tm=tm, tn=1024, name="mm_in_g", w_transposed=True,
             col0=4 * M_WIDTH + 2 * M_HEADS)
    hm, C, n, m = _mlstm(zm, zgate, p["m_conv_w"][0], p["ab_gate_b"][0], p["m_norm"][0],
                         m_conv[0], m_C[0], m_n[0], m_m[0])
    conv_new = zm[:, R - (M_CONV - 1):, :2 * M_WIDTH]
    og, S = _hgrn(zg, lbs[0], p["g_norm"][0], g_S[0])
    x = _mm_residual([hm, og], p["ab_w_out"], 0, x, md, 2, tm=tm, tn=512, name="mm_out0")
    h = _modnorm(x, p["norm_ffn"][0], md, 4, 3, tm)
    act = _mm(h, p["ffn_w1"], 0, n_cols=4 * D, tm=tm, tn=1024, act="relu2", out_dtype=BF16, name="ffn_up",
              cast_src=p["ffn_w2"] if make_w2 else None)
    if make_w2:
        act, w2 = act
        w2_bf16.append(w2[None])
    x = _mm_residual([act], w2_bf16[0], 0, x, md, 5, tm=256, tn=1024, name="ffn_down")
    md = mod[1]
    (xr, xw, xk, xv, xa, xg), shift_new = _rwkv_prep(x, p["norm_mix"][1], md, 1, 0, r_shift[0], p["r_mu"][0],
                                                     min(tm, 256))
    r = _mm(xr, p["r_wr"], 0, n_cols=D, tm=tm, tn=1024, name="mm_r")
    k = _mm(xk, p["r_wk"], 0, n_cols=D, tm=tm, tn=1024, name="mm_k")
    v = _mm(xv, p["r_wv"], 0, n_cols=D, tm=tm, tn=1024, name="mm_v")
    lw = _lora(xw, p["r_w1"][0], p["r_w2"][0], "tanh", min(tm, 512), post="log_decay", bias=p["r_w0"][0])
    aa = _lora(xa, p["r_a1"][0], p["r_a2"][0], None, min(tm, 512), post="sigmoid", bias=p["r_a0"][0])
    gg = _lora(xg, p["r_g1"][0], p["r_g2"][0], "sigmoid", min(tm, 512))
    yg, rS = _rwkv(r, k, v, lw, aa, gg, p["r_kk"][0], p["r_ka"][0], p["r_rk"][0].reshape(-1), p["r_lnw"][0],
                   p["r_lnb"][0], r_S[0])
    x = _mm_residual([yg], p["r_wo"], 0, x, md, 2, tm=tm, tn=1024, name="mm_out1")
    h = _modnorm(x, p["norm_ffn"][1], md, 4, 3, tm)
    act = _mm(h, p["ffn_w1"], 1, n_cols=4 * D, tm=tm, tn=1024, act="relu2", out_dtype=BF16, name="ffn_up",
              cast_src=p["ffn_w2"] if make_w2 else None)
    if make_w2:
        act, w2 = act
        w2_bf16.append(w2[None])
    x = _mm_residual([act], w2_bf16[1], 0, x, md, 5, tm=256, tn=1024, name="ffn_down")
    y = _rmsnorm(x, p["final_norm"], min(tm, 512))
    return y, (C[None], n[None], m.reshape(1, G, M_HEADS), conv_new[None], S[None], rS[None],
               shift_new.reshape(1, G, D)), w2_bf16


def kernel(x_prompt, x_sample, c_prompt, c_sample, state_mlstm_C, state_mlstm_n, state_mlstm_m, state_mlstm_conv, state_hgrn_S, state_rwkv_S, state_rwkv_shift, mod_w, mod_b, norm_mix, norm_ffn, ffn_w1, ffn_w2, final_norm, ab_w_in, ab_gate_b, m_conv_w, m_norm, g_lb, g_norm, ab_w_out, r_mu, r_w0, r_w1, r_w2, r_a0, r_a1, r_a2, r_g1, r_g2, r_kk, r_ka, r_rk, r_wr, r_wk, r_wv, r_wo, r_lnw, r_lnb):
    p = dict(norm_mix=norm_mix, norm_ffn=norm_ffn, ffn_w1=ffn_w1, ffn_w2=ffn_w2, final_norm=final_norm,
             ab_w_in=ab_w_in, ab_gate_b=ab_gate_b, m_conv_w=m_conv_w, m_norm=m_norm, g_norm=g_norm,
             ab_w_out=ab_w_out, r_mu=r_mu, r_w0=r_w0, r_w1=r_w1, r_w2=r_w2, r_a0=r_a0, r_a1=r_a1, r_a2=r_a2,
             r_g1=r_g1, r_g2=r_g2, r_kk=r_kk, r_ka=r_ka, r_rk=r_rk, r_wr=r_wr, r_wk=r_wk, r_wv=r_wv,
             r_wo=r_wo, r_lnw=r_lnw, r_lnb=r_lnb)
    B, T, D = x_prompt.shape
    Bs = x_sample.shape[0]
    depth = mod_w.shape[0]
    pad = (-(Bs + B)) % SUBLANES
    c_all = jnp.concatenate([c_sample, c_prompt, jnp.zeros((pad, D), F32)], axis=0)
    mod = _modulation(c_all, mod_w, mod_b)
    mod_s = mod[:, :Bs].reshape(depth, Bs, 1, N_MOD * D)
    mod_p = mod[:, Bs:Bs + B].reshape(depth, B, 1, N_MOD * D)
    lbs = jnp.cumsum(jax.nn.softmax(g_lb.astype(F32), axis=0), axis=0)

    n_even = state_mlstm_C.shape[0]
    n_odd = state_rwkv_S.shape[0]
    z = lambda *s: jnp.zeros(s, F32)
    yp, sp, w2_bf16 = _trunk(x_prompt, mod_p,
                    z(n_even, B, M_HEADS, M_HEAD_DIM, M_HEAD_DIM), z(n_even, B, M_HEADS, M_HEAD_DIM),
                    z(n_even, B, M_HEADS), z(n_even, B, M_CONV - 1, 2 * M_WIDTH),
                    z(n_even, B, G_HEADS, G_HEAD_DIM, G_HEAD_DIM), z(n_odd, B, R_HEADS, R_HEAD_DIM, R_HEAD_DIM),
                    z(n_odd, B, D), p, lbs, 1024)
    ys, ss, _ = _trunk(x_sample, mod_s, state_mlstm_C, state_mlstm_n, state_mlstm_m, state_mlstm_conv,
                       state_hgrn_S, state_rwkv_S, state_rwkv_shift, p, lbs, 1024, w2_bf16)
    return (yp, ys) + tuple(sp) + tuple(ss)
```

```python
import functools
import math

import jax
import jax.numpy as jnp
from jax import lax
from jax.experimental import pallas as pl
from jax.experimental.pallas import tpu as pltpu

F32 = jnp.float32
BF16 = jnp.bfloat16

D_MODEL = 2048
M_HEADS = 4
M_HEAD_DIM = 256
M_WIDTH = M_HEADS * M_HEAD_DIM
M_CONV = 4
G_HEADS = 8
G_HEAD_DIM = 128
G_WIDTH = G_HEADS * G_HEAD_DIM
R_HEAD_DIM = 64
R_HEADS = D_MODEL // R_HEAD_DIM
R_GROUP = 4
R_GROUP_W = R_GROUP * R_HEAD_DIM
N_MOD = 6
RMS_EPS = 1e-6
LN_X_EPS = 64e-5
CHUNK = 64
NEG_BIG = -1e30

V7X_VMEM_LIMIT_BYTES = 56 * 1024 * 1024
SUBLANES = 8
LANES = 128


def _cparams(*sem):
    return pltpu.CompilerParams(dimension_semantics=sem, vmem_limit_bytes=V7X_VMEM_LIMIT_BYTES)


def _sigmoid(x):
    return 1.0 / (1.0 + jnp.exp(-x))


def _silu(x):
    return x * _sigmoid(x)


def _softplus(x):
    return jnp.maximum(x, 0.0) + jnp.log1p(jnp.exp(-jnp.abs(x)))


def _dot(a, b):
    return jnp.dot(a.astype(BF16), b.astype(BF16), preferred_element_type=F32)


def _dot_nt(a, b):
    return lax.dot_general(a.astype(BF16), b.astype(BF16), (((1,), (1,)), ((), ())), preferred_element_type=F32)


def _dot_tn(a, b):
    return lax.dot_general(a.astype(BF16), b.astype(BF16), (((0,), (0,)), ((), ())), preferred_element_type=F32)


def _cumsum_rows(x):
    n = x.shape[0]
    row = lax.broadcasted_iota(jnp.int32, x.shape, 0)
    s = 1
    while s < n:
        x = x + jnp.where(row >= s, pltpu.roll(x, s, axis=0), 0.0)
        s *= 2
    return x


def _row_blocking(G, R, tm):
    if R >= tm:
        assert R % tm == 0
        return 1, tm, R // tm
    assert tm % R == 0 and G % (tm // R) == 0
    return tm // R, R, 1


def _mod_body(c_ref, w_ref, b_ref, o_ref):
    sc = _silu(c_ref[...])
    o_ref[...] = _dot(sc, w_ref[...]) + b_ref[...]


def _modulation(c_all, mod_w, mod_b):
    L, K, N = mod_w.shape
    Mc = c_all.shape[0]
    tn = 1024
    return pl.pallas_call(
        _mod_body,
        grid=(L, N // tn),
        in_specs=[pl.BlockSpec((Mc, K), lambda l, j: (0, 0)),
                  pl.BlockSpec((None, K, tn), lambda l, j: (l, 0, j)),
                  pl.BlockSpec((None, 1, tn), lambda l, j: (l, 0, j))],
        out_specs=pl.BlockSpec((None, Mc, tn), lambda l, j: (l, 0, j)),
        out_shape=jax.ShapeDtypeStruct((L, Mc, N), F32),
        compiler_params=_cparams("arbitrary", "arbitrary"),
        name="modulation",
    )(c_all, mod_w, mod_b.reshape(L, 1, N))


def _rms(x, g):
    ms = jnp.mean(x * x, axis=-1, keepdims=True)
    return x * lax.rsqrt(ms + RMS_EPS) * g


STRIP_ROWS = 2 * SUBLANES


def _strips(gb, rb):
    if gb == 1:
        n = rb // STRIP_ROWS
        tok = lambda i: (slice(None), pl.ds(pl.multiple_of(i * STRIP_ROWS, STRIP_ROWS), STRIP_ROWS), slice(None))
        seq = lambda i: (slice(None), slice(None), slice(None))
    else:
        assert STRIP_ROWS % rb == 0
        per = STRIP_ROWS // rb
        n = gb // per
        tok = seq = lambda i: (pl.ds(i * per, per), slice(None), slice(None))
    return n, tok, seq


def _modnorm_body(x_ref, g_ref, sc_ref, sh_ref, o_ref, *, gb, rb):
    n, tok, seq = _strips(gb, rb)

    def strip(i, carry):
        y = _rms(x_ref[tok(i)], g_ref[...])
        o_ref[tok(i)] = (y * (1.0 + sc_ref[seq(i)]) + sh_ref[seq(i)]).astype(o_ref.dtype)
        return carry

    lax.fori_loop(0, n, strip, 0, unroll=4)


def _modnorm(x, g, mod, i_scale, i_shift, tm):
    G, R, D = x.shape
    gb, rb, nrb = _row_blocking(G, R, tm)
    return pl.pallas_call(
        functools.partial(_modnorm_body, gb=gb, rb=rb),
        grid=(G // gb, nrb),
        in_specs=[pl.BlockSpec((gb, rb, D), lambda a, b: (a, b, 0)),
                  pl.BlockSpec((1, 1, D), lambda a, b: (0, 0, 0)),
                  pl.BlockSpec((gb, 1, D), lambda a, b: (a, 0, i_scale)),
                  pl.BlockSpec((gb, 1, D), lambda a, b: (a, 0, i_shift))],
        out_specs=pl.BlockSpec((gb, rb, D), lambda a, b: (a, b, 0)),
        out_shape=jax.ShapeDtypeStruct((G, R, D), BF16),
        compiler_params=_cparams("arbitrary", "arbitrary"),
        name="modnorm",
    )(x, g.reshape(1, 1, D), mod, mod)


def _rmsnorm_body(x_ref, g_ref, o_ref, *, gb, rb):
    n, tok, _ = _strips(gb, rb)

    def strip(i, carry):
        o_ref[tok(i)] = _rms(x_ref[tok(i)], g_ref[...])
        return carry

    lax.fori_loop(0, n, strip, 0, unroll=4)


def _rmsnorm(x, g, tm):
    G, R, D = x.shape
    gb, rb, nrb = _row_blocking(G, R, tm)
    return pl.pallas_call(
        functools.partial(_rmsnorm_body, gb=gb, rb=rb),
        grid=(G // gb, nrb),
        in_specs=[pl.BlockSpec((gb, rb, D), lambda a, b: (a, b, 0)),
                  pl.BlockSpec((1, 1, D), lambda a, b: (0, 0, 0))],
        out_specs=pl.BlockSpec((gb, rb, D), lambda a, b: (a, b, 0)),
        out_shape=jax.ShapeDtypeStruct((G, R, D), F32),
        compiler_params=_cparams("arbitrary", "arbitrary"),
        name="final_norm",
    )(x, g.reshape(1, 1, D))


def _mm_body(a_ref, w_ref, *rest, act, w_transposed, side_cast):
    if side_cast:
        src_ref, o_ref, dst_ref, wbf = rest
        dst_ref[...] = src_ref[...].astype(BF16)
    else:
        o_ref, wbf = rest

    @pl.when(pl.program_id(1) == 0)
    def _():
        wbf[...] = (w_ref[0].T if w_transposed else w_ref[...]).astype(BF16)

    acc = jnp.dot(a_ref[...].astype(BF16), wbf[...], preferred_element_type=F32)
    if act == "relu2":
        acc = jnp.square(jnp.maximum(acc, 0.0))
    o_ref[...] = acc.astype(o_ref.dtype)


def _mm(a, w, layer, *, n_cols, tm, tn, act=None, out_dtype=F32, name="mm", w_transposed=False, col0=0,
        cast_src=None):
    G, R, K = a.shape
    M = G * R
    assert M % tm == 0 and n_cols % tn == 0 and w.shape[2 if w_transposed else 1] == K
    nm = M // tm
    if w_transposed:
        assert col0 % SUBLANES == 0
        w_spec = pl.BlockSpec((pl.Element(1), pl.Element(tn), pl.Element(K)),
                              lambda j, i: (layer, pl.multiple_of(col0 + j * tn, SUBLANES), 0))
    else:
        assert col0 == 0
        w_spec = pl.BlockSpec((None, K, tn), lambda j, i: (layer, 0, j))
    in_specs = [pl.BlockSpec((tm, K), lambda j, i: (i, 0)), w_spec]
    out_specs = [pl.BlockSpec((tm, tn), lambda j, i: (i, j))]
    out_shape = [jax.ShapeDtypeStruct((M, n_cols), out_dtype)]
    args = [a.reshape(M, K), w]
    if cast_src is not None:
        _, P, Q = cast_src.shape
        steps = (n_cols // tn) * nm
        assert P % steps == 0
        in_specs.append(pl.BlockSpec((None, P // steps, Q), lambda j, i: (layer, j * nm + i, 0)))
        out_specs.append(pl.BlockSpec((P // steps, Q), lambda j, i: (j * nm + i, 0)))
        out_shape.append(jax.ShapeDtypeStruct((P, Q), BF16))
        args.append(cast_src)
    res = pl.pallas_call(
        functools.partial(_mm_body, act=act, w_transposed=w_transposed, side_cast=cast_src is not None),
        grid=(n_cols // tn, nm),
        in_specs=in_specs,
        out_specs=out_specs,
        out_shape=out_shape,
        scratch_shapes=[pltpu.VMEM((K, tn), BF16)],
        compiler_params=_cparams("arbitrary", "arbitrary"),
        name=name,
    )(*args)
    out = res[0].reshape(G, R, n_cols)
    return (out, res[1]) if cast_src is not None else out


def _mm_res_body(*refs, n_lhs, gb, rb, cast_w):
    a_refs = refs[:n_lhs]
    w_ref, x_ref, gt_ref, o_ref = refs[n_lhs:n_lhs + 4]
    if cast_w:
        wbf = refs[n_lhs + 4]

        @pl.when(pl.program_id(1) == 0)
        def _():
            wbf[...] = w_ref[...].astype(BF16)
    else:
        wbf = w_ref

    acc = None
    k0 = 0
    for a_ref in a_refs:
        kk = a_ref.shape[-1]
        part = jnp.dot(a_ref[...].astype(BF16), wbf[k0:k0 + kk, :], preferred_element_type=F32)
        acc = part if acc is None else acc + part
        k0 += kk
    tn = acc.shape[-1]
    o_ref[...] = x_ref[...] + gt_ref[...] * acc.reshape(gb, rb, tn)


def _mm_residual(a_list, w, layer, x, mod, i_gate, *, tm, tn, name="mm_res"):
    G, R, N = x.shape
    M = G * R
    K = w.shape[1]
    assert sum(a.shape[-1] for a in a_list) == K and w.shape[2] == N
    gb, rb, nrb = _row_blocking(G, R, tm)
    ntn = N // tn
    cast_w = w.dtype != BF16

    def xmap(j, i):
        return (i // nrb, i % nrb, j)

    return pl.pallas_call(
        functools.partial(_mm_res_body, n_lhs=len(a_list), gb=gb, rb=rb, cast_w=cast_w),
        grid=(ntn, M // tm),
        in_specs=[pl.BlockSpec((tm, a.shape[-1]), lambda j, i: (i, 0)) for a in a_list]
        + [pl.BlockSpec((None, K, tn), lambda j, i: (layer, 0, j)),
           pl.BlockSpec((gb, rb, tn), xmap),
           pl.BlockSpec((gb, 1, tn), lambda j, i: (i // nrb, 0, i_gate * ntn + j))],
        out_specs=pl.BlockSpec((gb, rb, tn), xmap),
        out_shape=jax.ShapeDtypeStruct((G, R, N), F32),
        scratch_shapes=[pltpu.VMEM((K, tn), BF16)] if cast_w else [],
        compiler_params=_cparams("arbitrary", "arbitrary"),
        name=name,
    )(*[a.reshape(M, a.shape[-1]) for a in a_list], w, x, mod)


def _mlstm_body(zq_ref, zk_ref, zv_ref, zo_ref, zg_ref, cw_ref, gbias_ref, gain_ref, conv0_ref,
                C0_ref, n0_ref, m0_ref, hm_ref, C_ref, n_ref, m_ref, ext, *, L, gs):
    W = M_WIDTH
    Dh = M_HEAD_DIM
    H = M_HEADS
    keep = M_CONV - 1
    c = pl.program_id(1)

    @pl.when(c == 0)
    def _():
        C_ref[...] = C0_ref[...]
        n_ref[...] = n0_ref[...]
        m_ref[...] = m0_ref[...]
        ext[:, SUBLANES - keep:SUBLANES, :] = conv0_ref[...]

    ext[:, SUBLANES:SUBLANES + L, 0:W] = zq_ref[...]
    ext[:, SUBLANES:SUBLANES + L, W:2 * W] = zk_ref[...]
    row = lax.broadcasted_iota(jnp.int32, (L, L), 0)
    col = lax.broadcasted_iota(jnp.int32, (L, L), 1)
    eye = row == col
    tri = col <= row
    chains = [(g, h) for g in range(gs) for h in range(H)]
    ids = range(len(chains))
    qk, gates = [], []
    for g in range(gs):
        conv = ext[g, SUBLANES:SUBLANES + L, :] * cw_ref[keep:keep + 1, :]
        for j in range(keep):
            conv = conv + ext[g, SUBLANES - keep + j:SUBLANES - keep + j + L, :] * cw_ref[j:j + 1, :]
        qk.append(_silu(conv))
        gates.append(zg_ref[g] + gbias_ref[...])
    ext[:, SUBLANES - keep:SUBLANES, :] = ext[:, SUBLANES + L - keep:SUBLANES + L, :]

    sl = [slice(h * Dh, (h + 1) * Dh) for _, h in chains]
    q = [qk[g][:, h * Dh:(h + 1) * Dh] for g, h in chains]
    k = [qk[g][:, W + h * Dh:W + (h + 1) * Dh] * (Dh ** -0.5) for g, h in chains]
    v = [zv_ref[g, :, sl[i]] for i, (g, h) in enumerate(chains)]
    ig_col = [gates[g][:, h:h + 1] for g, h in chains]
    fpre = [gates[g][:, H + h:H + h + 1] for g, h in chains]
    lf_col = [jnp.minimum(x, 0.0) - jnp.log1p(jnp.exp(-jnp.abs(x))) for x in fpre]
    lf_row = [jnp.sum(jnp.where(eye, x, 0.0), axis=0, keepdims=True) for x in lf_col]
    ig_row = [jnp.sum(jnp.where(eye, x, 0.0), axis=0, keepdims=True) for x in ig_col]
    b_col = [jnp.sum(jnp.where(tri, x, 0.0), axis=1, keepdims=True) for x in lf_row]
    b_row = [jnp.sum(jnp.where(row <= col, x, 0.0), axis=0, keepdims=True) for x in lf_col]
    dmat = [jnp.where(tri, b_col[i] - b_row[i] + ig_row[i], NEG_BIG) for i in ids]
    m_prev = [m_ref[g, :, h:h + 1] for g, h in chains]
    inter = [b_col[i] + m_prev[i] for i in ids]
    m_t = [jnp.maximum(inter[i], jnp.max(dmat[i], axis=1, keepdims=True)) for i in ids]
    w_inter = [jnp.exp(inter[i] - m_t[i]) for i in ids]
    Cm = [C_ref[g, h] for g, h in chains]
    n_row = [n_ref[g, h:h + 1, :] for g, h in chains]
    s = [_dot_nt(q[i], k[i]) * jnp.exp(dmat[i] - m_t[i]) for i in ids]
    qc = [_dot(q[i], Cm[i]) for i in ids]
    num = [_dot(s[i], v[i]) + w_inter[i] * qc[i] for i in ids]
    den = [jnp.sum(s[i], axis=1, keepdims=True) + w_inter[i] * jnp.sum(q[i] * n_row[i], axis=1, keepdims=True)
           for i in ids]
    hh = [num[i] / jnp.maximum(jnp.abs(den[i]), jnp.exp(-m_t[i])) for i in ids]
    m_new = [x[L - 1:L, :] for x in m_t]
    b_last = [x[L - 1:L, :] for x in b_col]
    wk = [jnp.exp(b_last[i] - b_col[i] + ig_col[i] - m_new[i]) * k[i] for i in ids]
    decay = [jnp.exp(b_last[i] + m_prev[i] - m_new[i]) for i in ids]
    upd = [_dot_tn(wk[i], v[i]) for i in ids]
    for i, (g, h) in enumerate(chains):
        C_ref[g, h] = decay[i] * Cm[i] + upd[i]
        n_ref[g, h:h + 1, :] = decay[i] * n_row[i] + jnp.sum(wk[i], axis=0, keepdims=True)
        m_ref[g, :, h:h + 1] = m_new[i]
        dlt = hh[i] - jnp.mean(hh[i], axis=-1, keepdims=True)
        ln = dlt * lax.rsqrt(jnp.mean(dlt * dlt, axis=-1, keepdims=True) + RMS_EPS)
        hm_ref[g, :, sl[i]] = (_sigmoid(zo_ref[g, :, sl[i]]) * ln * gain_ref[:, sl[i]]).astype(hm_ref.dtype)


def _mlstm(zm, zgate, conv_w, gate_b, m_gain, conv0, C0, n0, m0):
    G, R, _ = zm.shape
    L = math.gcd(R, CHUNK)
    W = M_WIDTH
    gs = 2 if R > SUBLANES else 4
    gbias = jnp.zeros((1, LANES), F32).at[0, :2 * M_HEADS].set(gate_b.astype(F32))
    zspec = lambda blk: pl.BlockSpec((gs, L, W), lambda g, c: (g, c, blk))
    st4 = lambda g, c: (g, 0, 0, 0)
    st3 = lambda g, c: (g, 0, 0)
    return pl.pallas_call(
        functools.partial(_mlstm_body, L=L, gs=gs),
        grid=(G // gs, R // L),
        in_specs=[zspec(0), zspec(1), zspec(2), zspec(3),
                  pl.BlockSpec((gs, L, LANES), lambda g, c: (g, c, 0)),
                  pl.BlockSpec((M_CONV, 2 * W), lambda g, c: (0, 0)),
                  pl.BlockSpec((1, LANES), lambda g, c: (0, 0)),
                  pl.BlockSpec((1, W), lambda g, c: (0, 0)),
                  pl.BlockSpec((gs, M_CONV - 1, 2 * W), st3),
                  pl.BlockSpec((gs, M_HEADS, M_HEAD_DIM, M_HEAD_DIM), st4),
                  pl.BlockSpec((gs, M_HEADS, M_HEAD_DIM), st3),
                  pl.BlockSpec((gs, 1, M_HEADS), st3)],
        out_specs=[pl.BlockSpec((gs, L, W), lambda g, c: (g, c, 0)),
                   pl.BlockSpec((gs, M_HEADS, M_HEAD_DIM, M_HEAD_DIM), st4),
                   pl.BlockSpec((gs, M_HEADS, M_HEAD_DIM), st3),
                   pl.BlockSpec((gs, 1, M_HEADS), st3)],
        out_shape=[jax.ShapeDtypeStruct((G, R, W), BF16),
                   jax.ShapeDtypeStruct((G, M_HEADS, M_HEAD_DIM, M_HEAD_DIM), F32),
                   jax.ShapeDtypeStruct((G, M_HEADS, M_HEAD_DIM), F32),
                   jax.ShapeDtypeStruct((G, 1, M_HEADS), F32)],
        scratch_shapes=[pltpu.VMEM((gs, SUBLANES + L, 2 * W), F32)],
        compiler_params=_cparams("arbitrary", "arbitrary"),
        name="mlstm",
    )(zm, zm, zm, zm, zgate, conv_w, gbias, m_gain.reshape(1, W), conv0, C0, n0, m0.reshape(G, 1, M_HEADS))


def _hgrn_body(zq_ref, zf_ref, zi_ref, zgg_ref, lb_ref, gain_ref, S0_ref, og_ref, S_ref, ST, *, nblk, gs, st_t, nh):
    Dh = G_HEAD_DIM
    B = SUBLANES
    c = pl.program_id(1)
    chains = [(g, h) for g in range(gs) for h in range(G_HEADS)]

    @pl.when(c == 0)
    def _():
        for i, (g, h) in enumerate(chains):
            ST[i] = S0_ref[g, h].T if st_t else S0_ref[g, h]

    rowi = lax.broadcasted_iota(jnp.int32, (B, Dh), 0)
    ids = range(len(chains))
    sl = [slice(h * Dh, (h + 1) * Dh) for _, h in chains]
    gi = [g for g, _ in chains]

    def diag(bc, qh, kk, v, o):
        for s in range(B):
            p = [jnp.exp(jnp.where(rowi >= s, bc[i] - bc[i][s:s + 1, :], NEG_BIG)) * qh[i] * kk[i][s:s + 1, :]
                 for i in ids]
            o = [o[i] + jnp.sum(p[i], axis=-1, keepdims=True) * v[i][s:s + 1, :] for i in ids]
        return o

    def blk(bi, carry):
        r0 = pl.multiple_of(bi * (nh * B), nh * B)
        st = [ST[i] for i in ids]
        f, kk, qh, v, bc = [], [], [], [], []
        for hf in range(nh):
            rows = pl.ds(r0 + hf * B, B)
            f_h = [lb_ref[:, sl[i]] + (1.0 - lb_ref[:, sl[i]]) * _sigmoid(zf_ref[gi[i], rows, sl[i]]) for i in ids]
            kk.append([1.0 - x for x in f_h])
            loc = [_cumsum_rows(jnp.log(x)) for x in f_h]
            bc.append(loc if hf == 0 else [loc[i] + bc[hf - 1][i][B - 1:B, :] for i in ids])
            qh.append([_silu(zq_ref[gi[i], rows, sl[i]]) * (Dh ** -0.5) for i in ids])
            v.append([zi_ref[gi[i], rows, sl[i]] for i in ids])
        btot = [bc[nh - 1][i][B - 1:B, :] for i in ids]
        cat = lambda parts, i: parts[0][i] if nh == 1 else jnp.concatenate([pt[i] for pt in parts], axis=0)
        qd = [cat([[qh[hf][i] * jnp.exp(bc[hf][i]) for i in ids] for hf in range(nh)], i) for i in ids]
        kd = [cat([[kk[hf][i] * jnp.exp(btot[i] - bc[hf][i]) for i in ids] for hf in range(nh)], i) for i in ids]
        vv = [cat(v, i) for i in ids]
        if st_t:
            os_ = [_dot_nt(qd[i], st[i]) for i in ids]
            upd = [_dot_tn(vv[i], kd[i]) for i in ids]
            dec = [jnp.exp(btot[i]) for i in ids]
        else:
            os_ = [_dot(qd[i], st[i]) for i in ids]
            upd = [_dot_tn(kd[i], vv[i]) for i in ids]
            dec = [jnp.transpose(jnp.broadcast_to(jnp.exp(btot[i]), (B, Dh)))[:, 0:1] for i in ids]
        o = []
        for hf in range(nh):
            oh = diag(bc[hf], qh[hf], kk[hf], v[hf], [os_[i][hf * B:(hf + 1) * B, :] for i in ids])
            for prev in range(hf):
                edge = [bc[prev][i][B - 1:B, :] for i in ids]
                ax = [_dot_nt(qh[hf][i] * jnp.exp(bc[hf][i] - edge[i]), kk[prev][i] * jnp.exp(edge[i] - bc[prev][i]))
                      for i in ids]
                oh = [oh[i] + _dot(ax[i], v[prev][i]) for i in ids]
            o.append(oh)
        for i in ids:
            ST[i] = st[i] * dec[i] + upd[i]
        per = min(nh, 2)
        for h0 in range(0, nh, per):
            rows = pl.ds(r0 + h0 * B, per * B)
            for i in ids:
                oi = o[h0][i] if per == 1 else jnp.concatenate([o[h0][i], o[h0 + 1][i]], axis=0)
                rms = oi * lax.rsqrt(jnp.mean(oi * oi, axis=-1, keepdims=True) + RMS_EPS)
                og_ref[gi[i], rows, sl[i]] = (rms * gain_ref[:, sl[i]]
                                              * _silu(zgg_ref[gi[i], rows, sl[i]])).astype(og_ref.dtype)
        return carry

    if nblk == 1:
        blk(0, 0)
    else:
        lax.fori_loop(0, nblk, blk, 0)

    @pl.when(c == pl.num_programs(1) - 1)
    def _():
        for i, (g, h) in enumerate(chains):
            S_ref[g, h] = ST[i].T if st_t else ST[i]


def _hgrn(zg, lb, g_gain, S0):
    G, R, _ = zg.shape
    W = G_WIDTH
    gs = 4
    Rc = min(R, 128)
    nh = 4 if R > SUBLANES else 1
    zspec = lambda blk: pl.BlockSpec((gs, Rc, W), lambda g, c: (g, c, blk))
    st4 = lambda g, c: (g, 0, 0, 0)
    return pl.pallas_call(
        functools.partial(_hgrn_body, nblk=Rc // (nh * SUBLANES), gs=gs, st_t=R > SUBLANES, nh=nh),
        grid=(G // gs, R // Rc),
        in_specs=[zspec(0), zspec(1), zspec(2), zspec(3),
                  pl.BlockSpec((1, W), lambda g, c: (0, 0)),
                  pl.BlockSpec((1, W), lambda g, c: (0, 0)),
                  pl.BlockSpec((gs, G_HEADS, G_HEAD_DIM, G_HEAD_DIM), st4)],
        out_specs=[pl.BlockSpec((gs, Rc, W), lambda g, c: (g, c, 0)),
                   pl.BlockSpec((gs, G_HEADS, G_HEAD_DIM, G_HEAD_DIM), st4)],
        out_shape=[jax.ShapeDtypeStruct((G, R, W), BF16),
                   jax.ShapeDtypeStruct((G, G_HEADS, G_HEAD_DIM, G_HEAD_DIM), F32)],
        scratch_shapes=[pltpu.VMEM((gs * G_HEADS, G_HEAD_DIM, G_HEAD_DIM), F32)],
        compiler_params=_cparams("arbitrary", "arbitrary"),
        name="hgrn2",
    )(zg, zg, zg, zg, lb.reshape(1, W), g_gain.reshape(1, W), S0)


def _rprep_body(x_ref, g_ref, sc_ref, sh_ref, shift0_ref, mu_ref, *refs, gb, rb):
    outs = refs[:6]
    hlast_ref, hbuf = refs[6:]
    B = SUBLANES
    n, tok, seq = _strips(gb, rb)
    S = hbuf.shape[1] - B
    whole_sequences = gb > 1

    if not whole_sequences:
        @pl.when(pl.program_id(1) == 0)
        def _():
            hbuf[:, B - 1:B, :] = shift0_ref[...]

    first = lax.broadcasted_iota(jnp.int32, (hbuf.shape[0], S, 1), 1) == 0

    def strip(i, carry):
        h = _rms(x_ref[tok(i)], g_ref[...]) * (1.0 + sc_ref[seq(i)]) + sh_ref[seq(i)]
        before = shift0_ref[seq(i)] if whole_sequences else hbuf[:, B - 1:B, :]
        xx = jnp.where(first, before, pltpu.roll(h, 1, axis=1)) - h
        last = h[:, S - 1:S, :]
        if whole_sequences:
            hlast_ref[seq(i)] = last
        else:
            hbuf[:, B - 1:B, :] = last
        for m in range(6):
            outs[m][tok(i)] = (h + xx * mu_ref[m:m + 1, :]).astype(BF16)
        return carry

    lax.fori_loop(0, n, strip, 0, unroll=4)
    if not whole_sequences:
        hlast_ref[...] = hbuf[:, B - 1:B, :]


def _rwkv_prep(x, g, mod, i_scale, i_shift, shift0, mu, tm):
    G, R, D = x.shape
    gb, rb, nrb = _row_blocking(G, R, tm)
    tok = pl.BlockSpec((gb, rb, D), lambda a, b: (a, b, 0))
    one = pl.BlockSpec((gb, 1, D), lambda a, b: (a, 0, 0))
    strip_shape = (1, SUBLANES + STRIP_ROWS, D) if gb == 1 else (STRIP_ROWS // rb, SUBLANES + rb, D)
    res = pl.pallas_call(
        functools.partial(_rprep_body, gb=gb, rb=rb),
        grid=(G // gb, nrb),
        in_specs=[tok,
                  pl.BlockSpec((1, 1, D), lambda a, b: (0, 0, 0)),
                  pl.BlockSpec((gb, 1, D), lambda a, b: (a, 0, i_scale)),
                  pl.BlockSpec((gb, 1, D), lambda a, b: (a, 0, i_shift)),
                  one,
                  pl.BlockSpec((6, D), lambda a, b: (0, 0))],
        out_specs=[tok] * 6 + [one],
        out_shape=[jax.ShapeDtypeStruct((G, R, D), BF16)] * 6 + [jax.ShapeDtypeStruct((G, 1, D), F32)],
        scratch_shapes=[pltpu.VMEM(strip_shape, F32)],
        compiler_params=_cparams("arbitrary", "arbitrary"),
        name="rwkv_prep",
    )(x, g.reshape(1, 1, D), mod, mod, shift0.reshape(G, 1, D), mu)
    return res[:6], res[6]


def _lora_body(x_ref, w1_ref, w2_ref, b_ref, o_ref, *, act, post):
    t = _dot(x_ref[...], w1_ref[...])
    if act == "tanh":
        t = jnp.tanh(t)
    elif act == "sigmoid":
        t = _sigmoid(t)
    y = _dot(t, w2_ref[...])
    if post == "log_decay":
        y = -math.exp(-0.5) * _sigmoid(b_ref[...] + y)
    elif post == "sigmoid":
        y = _sigmoid(b_ref[...] + y)
    o_ref[...] = y


def _lora(x, w1, w2, act, tm, post=None, bias=None):
    G, R, K = x.shape
    M = G * R
    r = w1.shape[1]
    rp = -(-r // LANES) * LANES
    w1p = jnp.zeros((K, rp), F32).at[:, :r].set(w1)
    w2p = jnp.zeros((rp, w2.shape[1]), F32).at[:r, :].set(w2)
    N = w2.shape[1]
    b = jnp.zeros((1, N), F32) if bias is None else bias.reshape(1, N).astype(F32)
    out = pl.pallas_call(
        functools.partial(_lora_body, act=act, post=post),
        grid=(M // tm,),
        in_specs=[pl.BlockSpec((tm, K), lambda i: (i, 0)),
                  pl.BlockSpec((K, rp), lambda i: (0, 0)),
                  pl.BlockSpec((rp, N), lambda i: (0, 0)),
                  pl.BlockSpec((1, N), lambda i: (0, 0))],
        out_specs=pl.BlockSpec((tm, N), lambda i: (i, 0)),
        out_shape=jax.ShapeDtypeStruct((M, N), F32),
        compiler_params=_cparams("arbitrary"),
        name="lora_" + str(act),
    )(x.reshape(M, K), w1p, w2p, b)
    return out.reshape(G, R, N)


def _head_mask(rows, cols, rper, cper):
    r = lax.broadcasted_iota(jnp.int32, (rows, cols), 0)
    c = lax.broadcasted_iota(jnp.int32, (rows, cols), 1)
    return (r // rper) == (c // cper)


def _bd(y, mask01):
    if y.shape[0] % (2 * SUBLANES) == 0:
        return jnp.concatenate([y.astype(BF16)] * R_GROUP, axis=0) * mask01
    return jnp.concatenate([y] * R_GROUP, axis=0).astype(BF16) * mask01


def _segsums(xs, ones_bd):
    c = xs[0].shape[0]
    if c % (2 * SUBLANES) == 0:
        lhs = jnp.concatenate([x.astype(BF16) for x in xs], axis=0)
    else:
        lhs = jnp.concatenate(xs, axis=0).astype(BF16)
    res = jnp.dot(lhs, ones_bd, preferred_element_type=F32)
    return [res[i * c:(i + 1) * c] for i in range(len(xs))]


def _rwkv_body(r_ref, k_ref, v_ref, lw_ref, a_ref, g_ref, kkp_ref, kap_ref, rk_ref,
               lnw_ref, lnb_ref, S0_ref, y_ref, S_ref, Sbd, *, c, nchunk, ng, gs):
    N = R_HEAD_DIM
    GW = R_GROUP_W
    j = pl.program_id(2)
    m_state = _head_mask(GW, GW, N, N)
    chains = [(q, gi) for q in range(gs) for gi in range(ng)]
    ids = range(len(chains))
    lanes = [slice(gi * GW, (gi + 1) * GW) for _, gi in chains]
    seqs = [q for q, _ in chains]

    @pl.when(j == 0)
    def _():
        for i, (q, gi) in enumerate(chains):
            s0 = S0_ref[q, gi * R_GROUP:(gi + 1) * R_GROUP].reshape(GW, N)
            Sbd[i] = jnp.where(m_state, jnp.concatenate([s0] * R_GROUP, axis=1), 0.0)

    ones_bd = jnp.where(m_state, 1.0, 0.0).astype(BF16)
    m_vec = jnp.where(_head_mask(R_GROUP * c, GW, c, N), 1.0, 0.0).astype(BF16)
    m_mat = jnp.where(_head_mask(R_GROUP * c, R_GROUP * c, c, c), 1.0, 0.0).astype(BF16)
    t_idx = lax.broadcasted_iota(jnp.int32, (2 * c, R_GROUP * c), 0)
    s_idx = lax.broadcasted_iota(jnp.int32, (2 * c, R_GROUP * c), 1) % c
    causal = jnp.where(t_idx < c, jnp.where(s_idx < t_idx, 1.0, 0.0), jnp.where(s_idx <= t_idx - c, 1.0, 0.0))
    nt = (((1,), (1,)), ((), ()))
    mm = lambda a, b: jnp.dot(a.astype(BF16), b, preferred_element_type=F32)

    tril = jnp.where(lax.broadcasted_iota(jnp.int32, (c, c), 1) <= lax.broadcasted_iota(jnp.int32, (c, c), 0),
                     1.0, 0.0).astype(BF16)
    eye = jnp.where(lax.broadcasted_iota(jnp.int32, (c, R_GROUP * c), 1) % c
                    == lax.broadcasted_iota(jnp.int32, (c, R_GROUP * c), 0), 1.0, 0.0)
    keys = ("ar", "bdb", "bdk", "bdv", "vb", "gam", "tail")

    def prepare_steps(ci, out):
        rows = pl.ds(pl.multiple_of(ci * c, c), c)
        ld = lambda ref: [ref[seqs[i], rows, lanes[i]] for i in ids]
        r, k, a = ld(r_ref), ld(k_ref), ld(a_ref)
        kk = [k[i] * kkp_ref[:, lanes[i]] for i in ids]
        k2 = [k[i] * (1.0 + (a[i] - 1.0) * kap_ref[:, lanes[i]]) for i in ids]
        sums = _segsums([kk[i] * kk[i] for i in ids] + [r[i] * k2[i] * rk_ref[:, lanes[i]] for i in ids], ones_bd)
        yield
        for i in ids:
            v, lw, g = v_ref[seqs[i], rows, lanes[i]], lw_ref[seqs[i], rows, lanes[i]], g_ref[seqs[i], rows, lanes[i]]
            hi = lw.astype(BF16)
            lo = (lw - hi.astype(F32)).astype(BF16)
            cw2 = jnp.dot(tril, jnp.concatenate([hi, lo], axis=1), preferred_element_type=F32)
            cw = cw2[:, :GW] + cw2[:, GW:]
            gam = jnp.exp(cw[c - 1:c, :])
            kn = kk[i] * lax.rsqrt(jnp.maximum(sums[i], 1e-24))
            bb = kn * a[i]
            e_out = jnp.exp(-cw)
            e_end = gam * e_out
            out["ar"].append(jnp.concatenate([-kn * jnp.exp(cw - lw), r[i] * jnp.exp(cw)], axis=0).astype(BF16))
            out["gam"].append(gam)
            out["tail"].append(jnp.concatenate([sums[len(chains) + i] * v, g], axis=0))
            yield
            out["bdb"].append(_bd(bb * e_out, m_vec))
            out["bdk"].append(_bd(k2[i] * e_out, m_vec))
            out["bdv"].append(_bd(v, m_vec))
            out["vb"].append(jnp.concatenate([v, bb * e_end, k2[i] * e_end], axis=0).astype(BF16))
            yield

    def prepare(ci):
        out = {key: [] for key in keys}
        for _ in prepare_steps(ci, out):
            pass
        return out

    def chain(ci, p, side_work=None):
        def tick():
            if side_work is not None:
                next(side_work, None)

        rows = pl.ds(pl.multiple_of(ci * c, c), c)
        ar, bdv = p["ar"], p["bdv"]
        sbd = [Sbd[i] for i in ids]
        pb = [lax.dot_general(ar[i], p["bdb"][i], nt, preferred_element_type=F32) * causal for i in ids]
        pk = [(lax.dot_general(ar[i], p["bdk"][i], nt, preferred_element_type=F32) * causal).astype(BF16) for i in ids]
        tick()
        s0p = [lax.dot_general(ar[i], sbd[i].astype(BF16), nt, preferred_element_type=F32) for i in ids]
        kv = [mm(pk[i], bdv[i]) for i in ids]
        w = [s0p[i][:c] + kv[i][:c] for i in ids]
        tick()
        pw = [pb[i][:c] for i in ids]
        tinv = [eye + pw[i] for i in ids]
        levels = c.bit_length() - 1
        bdp = [_bd(pw[i], m_mat) for i in ids]
        pw = [mm(pw[i], bdp[i]) for i in ids]
        tick()
        for lvl in range(1, levels):
            bdp = [_bd(pw[i], m_mat) for i in ids]
            if lvl < levels - 1:
                res = [mm(jnp.concatenate([tinv[i], pw[i]], axis=0), bdp[i]) for i in ids]
                tinv = [tinv[i] + res[i][:c] for i in ids]
                pw = [res[i][c:] for i in ids]
            else:
                tinv = [tinv[i] + mm(tinv[i], bdp[i]) for i in ids]
            tick()
        u = [mm(tinv[i], _bd(w[i], m_vec)).astype(BF16) for i in ids]
        tick()
        y = [s0p[i][c:] + mm(pb[i][c:], _bd(u[i], m_vec)) + kv[i][c:] for i in ids]
        tick()
        upd = [_dot_tn(jnp.concatenate([u[i], p["vb"][i][:c]], axis=0), p["vb"][i][c:]) for i in ids]
        for i in ids:
            Sbd[i] = sbd[i] * p["gam"][i] + jnp.where(m_state, upd[i], 0.0)
        tick()
        ysum = _segsums(y, ones_bd)
        dlt = [y[i] - ysum[i] * (1.0 / N) for i in ids]
        vsum = _segsums([dlt[i] * dlt[i] for i in ids], ones_bd)
        for i in ids:
            yn = dlt[i] * lax.rsqrt(vsum[i] * (1.0 / N) + LN_X_EPS) * lnw_ref[:, lanes[i]] + lnb_ref[:, lanes[i]]
            y_ref[seqs[i], rows, lanes[i]] = ((yn + p["tail"][i][:c]) * p["tail"][i][c:]).astype(y_ref.dtype)
        if side_work is not None:
            for _ in side_work:
                pass

    if nchunk == 1:
        chain(0, prepare(0))
    else:
        def body(ci, p):
            nxt = {key: [] for key in keys}
            chain(ci, p, prepare_steps(jnp.minimum(ci + 1, nchunk - 1), nxt))
            return nxt

        lax.fori_loop(0, nchunk, body, prepare(0))

    @pl.when(j == pl.num_programs(2) - 1)
    def _():
        for i, (q, gi) in enumerate(chains):
            sbd = Sbd[i]
            for h in range(R_GROUP):
                S_ref[q, gi * R_GROUP + h] = sbd[h * N:(h + 1) * N, h * N:(h + 1) * N]


def _rwkv(r, k, v, lw, a, g, kkp, kap, rk, lnw, lnb, S0):
    G, R, D = r.shape
    c = math.gcd(R, CHUNK)
    GW = R_GROUP_W
    if R > SUBLANES:
        gs, ng, Rc = 1, 8, min(R, 256)
    else:
        gs, ng, Rc = 2, D // GW, R
    bw = ng * GW
    tok = pl.BlockSpec((gs, Rc, bw), lambda a_, b, j: (a_, j, b))
    par = pl.BlockSpec((1, bw), lambda a_, b, j: (0, b))
    st = pl.BlockSpec((gs, ng * R_GROUP, R_HEAD_DIM, R_HEAD_DIM), lambda a_, b, j: (a_, b, 0, 0))
    row = lambda p: p.reshape(1, D).astype(F32)
    return pl.pallas_call(
        functools.partial(_rwkv_body, c=c, nchunk=Rc // c, ng=ng, gs=gs),
        grid=(G // gs, D // bw, R // Rc),
        in_specs=[tok] * 6 + [par] * 5 + [st],
        out_specs=[tok, st],
        out_shape=[jax.ShapeDtypeStruct((G, R, D), BF16),
                   jax.ShapeDtypeStruct((G, R_HEADS, R_HEAD_DIM, R_HEAD_DIM), F32)],
        scratch_shapes=[pltpu.VMEM((gs * ng, GW, GW), F32)],
        compiler_params=_cparams("arbitrary", "arbitrary", "arbitrary"),
        name="rwkv7",
    )(r, k, v, lw, a, g, row(kkp), row(kap), row(rk), row(lnw), row(lnb), S0)


def _trunk(x, mod, m_C, m_n, m_m, m_conv, g_S, r_S, r_shift, p, lbs, tm, w2_bf16=None):
    make_w2 = w2_bf16 is None
    w2_bf16 = [] if make_w2 else w2_bf16
    G, R, D = x.shape
    md = mod[0]
    h = _modnorm(x, p["norm_mix"][0], md, 1, 0, tm)
    w_in_t = jnp.swapaxes(p["ab_w_in"], 1, 2)
    zm = _mm(h, w_in_t, 0, n_cols=4 * M_WIDTH, tm=tm, tn=1024, name="mm_in_m", w_transposed=True)
    zgate = _mm(h, w_in_t, 0, n_cols=LANES, tm=tm, tn=LANES, name="mm_in_gate", w_transposed=True, col0=4 * M_WIDTH)
    zg = _mm(h, w_in_t, 0, n_cols=4 * G_WIDTH, tm=tm, tn=1024, name="mm_in_g", w_transposed=True,
             col0=4 * M_WIDTH + 2 * M_HEADS)
    hm, C, n, m = _mlstm(zm, zgate, p["m_conv_w"][0], p["ab_gate_b"][0], p["m_norm"][0],
                         m_conv[0], m_C[0], m_n[0], m_m[0])
    conv_new = zm[:, R - (M_CONV - 1):, :2 * M_WIDTH]
    og, S = _hgrn(zg, lbs[0], p["g_norm"][0], g_S[0])
    x = _mm_residual([hm, og], p["ab_w_out"], 0, x, md, 2, tm=tm, tn=1024, name="mm_out0")
    h = _modnorm(x, p["norm_ffn"][0], md, 4, 3, tm)
    act = _mm(h, p["ffn_w1"], 0, n_cols=4 * D, tm=tm, tn=1024, act="relu2", out_dtype=BF16, name="ffn_up",
              cast_src=p["ffn_w2"] if make_w2 else None)
    if make_w2:
        act, w2 = act
        w2_bf16.append(w2[None])
    x = _mm_residual([act], w2_bf16[0], 0, x, md, 5, tm=256, tn=1024, name="ffn_down")
    md = mod[1]
    (xr, xw, xk, xv, xa, xg), shift_new = _rwkv_prep(x, p["norm_mix"][1], md, 1, 0, r_shift[0], p["r_mu"][0],
                                                     min(tm, 256))
    r = _mm(xr, p["r_wr"], 0, n_cols=D, tm=tm, tn=1024, name="mm_r")
    k = _mm(xk, p["r_wk"], 0, n_cols=D, tm=tm, tn=1024, name="mm_k")
    v = _mm(xv, p["r_wv"], 0, n_cols=D, tm=tm, tn=1024, name="mm_v")
    lw = _lora(xw, p["r_w1"][0], p["r_w2"][0], "tanh", min(tm, 512), post="log_decay", bias=p["r_w0"][0])
    aa = _lora(xa, p["r_a1"][0], p["r_a2"][0], None, min(tm, 512), post="sigmoid", bias=p["r_a0"][0])
    gg = _lora(xg, p["r_g1"][0], p["r_g2"][0], "sigmoid", min(tm, 512))
    yg, rS = _rwkv(r, k, v, lw, aa, gg, p["r_kk"][0], p["r_ka"][0], p["r_rk"][0].reshape(-1), p["r_lnw"][0],
                   p["r_lnb"][0], r_S[0])
    x = _mm_residual([yg], p["r_wo"], 0, x, md, 2, tm=tm, tn=1024, name="mm_out1")
    h = _modnorm(x, p["norm_ffn"][1], md, 4, 3, tm)
    act = _mm(h, p["ffn_w1"], 1, n_cols=4 * D, tm=tm, tn=1024, act="relu2", out_dtype=BF16, name="ffn_up",
              cast_src=p["ffn_w2"] if make_w2 else None)
    if make_w2:
        act, w2 = act
        w2_bf16.append(w2[None])
    x = _mm_residual([act], w2_bf16[1], 0, x, md, 5, tm=256, tn=1024, name="ffn_down")
    y = _rmsnorm(x, p["final_norm"], min(tm, 512))
    return y, (C[None], n[None], m.reshape(1, G, M_HEADS), conv_new[None], S[None], rS[None],
               shift_new.reshape(1, G, D)), w2_bf16


def kernel(x_prompt, x_sample, c_prompt, c_sample, state_mlstm_C, state_mlstm_n, state_mlstm_m, state_mlstm_conv, state_hgrn_S, state_rwkv_S, state_rwkv_shift, mod_w, mod_b, norm_mix, norm_ffn, ffn_w1, ffn_w2, final_norm, ab_w_in, ab_gate_b, m_conv_w, m_norm, g_lb, g_norm, ab_w_out, r_mu, r_w0, r_w1, r_w2, r_a0, r_a1, r_a2, r_g1, r_g2, r_kk, r_ka, r_rk, r_wr, r_wk, r_wv, r_wo, r_lnw, r_lnb):
    p = dict(norm_mix=norm_mix, norm_ffn=norm_ffn, ffn_w1=ffn_w1, ffn_w2=ffn_w2, final_norm=final_norm,
             ab_w_in=ab_w_in, ab_gate_b=ab_gate_b, m_conv_w=m_conv_w, m_norm=m_norm, g_norm=g_norm,
             ab_w_out=ab_w_out, r_mu=r_mu, r_w0=r_w0, r_w1=r_w1, r_w2=r_w2, r_a0=r_a0, r_a1=r_a1, r_a2=r_a2,
             r_g1=r_g1, r_g2=r_g2, r_kk=r_kk, r_ka=r_ka, r_rk=r_rk, r_wr=r_wr, r_wk=r_wk, r_wv=r_wv,
             r_wo=r_wo, r_lnw=r_lnw, r_lnb=r_lnb)
    B, T, D = x_prompt.shape
    Bs = x_sample.shape[0]
    depth = mod_w.shape[0]
    pad = (-(Bs + B)) % SUBLANES
    c_all = jnp.concatenate([c_sample, c_prompt, jnp.zeros((pad, D), F32)], axis=0)
    mod = _modulation(c_all, mod_w, mod_b)
    mod_s = mod[:, :Bs].reshape(depth, Bs, 1, N_MOD * D)
    mod_p = mod[:, Bs:Bs + B].reshape(depth, B, 1, N_MOD * D)
    lbs = jnp.cumsum(jax.nn.softmax(g_lb.astype(F32), axis=0), axis=0)

    n_even = state_mlstm_C.shape[0]
    n_odd = state_rwkv_S.shape[0]
    z = lambda *s: jnp.zeros(s, F32)
    yp, sp, w2_bf16 = _trunk(x_prompt, mod_p,
                    z(n_even, B, M_HEADS, M_HEAD_DIM, M_HEAD_DIM), z(n_even, B, M_HEADS, M_HEAD_DIM),
                    z(n_even, B, M_HEADS), z(n_even, B, M_CONV - 1, 2 * M_WIDTH),
                    z(n_even, B, G_HEADS, G_HEAD_DIM, G_HEAD_DIM), z(n_odd, B, R_HEADS, R_HEAD_DIM, R_HEAD_DIM),
                    z(n_odd, B, D), p, lbs, 1024)
    ys, ss, _ = _trunk(x_sample, mod_s, state_mlstm_C, state_mlstm_n, state_mlstm_m, state_mlstm_conv,
                       state_hgrn_S, state_rwkv_S, state_rwkv_shift, p, lbs, 1024, w2_bf16)
    return (yp, ys) + tuple(sp) + tuple(ss)
```

```python
import functools
import math

import jax
import jax.numpy as jnp
from jax import lax
from jax.experimental import pallas as pl
from jax.experimental.pallas import tpu as pltpu

F32 = jnp.float32
BF16 = jnp.bfloat16

D_MODEL = 2048
M_HEADS = 4
M_HEAD_DIM = 256
M_WIDTH = M_HEADS * M_HEAD_DIM
M_CONV = 4
G_HEADS = 8
G_HEAD_DIM = 128
G_WIDTH = G_HEADS * G_HEAD_DIM
R_HEAD_DIM = 64
R_HEADS = D_MODEL // R_HEAD_DIM
R_GROUP = 4
R_GROUP_W = R_GROUP * R_HEAD_DIM
N_MOD = 6
RMS_EPS = 1e-6
LN_X_EPS = 64e-5
CHUNK = 64
NEG_BIG = -1e30

V7X_VMEM_LIMIT_BYTES = 56 * 1024 * 1024
SUBLANES = 8
LANES = 128


def _cparams(*sem):
    return pltpu.CompilerParams(dimension_semantics=sem, vmem_limit_bytes=V7X_VMEM_LIMIT_BYTES)


def _sigmoid(x):
    return 1.0 / (1.0 + jnp.exp(-x))


def _silu(x):
    return x * _sigmoid(x)


def _softplus(x):
    return jnp.maximum(x, 0.0) + jnp.log1p(jnp.exp(-jnp.abs(x)))


def _dot(a, b):
    return jnp.dot(a.astype(BF16), b.astype(BF16), preferred_element_type=F32)


def _dot_nt(a, b):
    return lax.dot_general(a.astype(BF16), b.astype(BF16), (((1,), (1,)), ((), ())), preferred_element_type=F32)


def _dot_tn(a, b):
    return lax.dot_general(a.astype(BF16), b.astype(BF16), (((0,), (0,)), ((), ())), preferred_element_type=F32)


def _cumsum_rows(x):
    n = x.shape[0]
    row = lax.broadcasted_iota(jnp.int32, x.shape, 0)
    s = 1
    while s < n:
        x = x + jnp.where(row >= s, pltpu.roll(x, s, axis=0), 0.0)
        s *= 2
    return x


def _row_blocking(G, R, tm):
    if R >= tm:
        assert R % tm == 0
        return 1, tm, R // tm
    assert tm % R == 0 and G % (tm // R) == 0
    return tm // R, R, 1


def _mod_body(c_ref, w_ref, b_ref, o_ref):
    sc = _silu(c_ref[...])
    o_ref[...] = _dot(sc, w_ref[...]) + b_ref[...]


def _modulation(c_all, mod_w, mod_b):
    L, K, N = mod_w.shape
    Mc = c_all.shape[0]
    tn = 1024
    return pl.pallas_call(
        _mod_body,
        grid=(L, N // tn),
        in_specs=[pl.BlockSpec((Mc, K), lambda l, j: (0, 0)),
                  pl.BlockSpec((None, K, tn), lambda l, j: (l, 0, j)),
                  pl.BlockSpec((None, 1, tn), lambda l, j: (l, 0, j))],
        out_specs=pl.BlockSpec((None, Mc, tn), lambda l, j: (l, 0, j)),
        out_shape=jax.ShapeDtypeStruct((L, Mc, N), F32),
        compiler_params=_cparams("arbitrary", "arbitrary"),
        name="modulation",
    )(c_all, mod_w, mod_b.reshape(L, 1, N))


def _rms(x, g):
    ms = jnp.mean(x * x, axis=-1, keepdims=True)
    return x * lax.rsqrt(ms + RMS_EPS) * g


STRIP_ROWS = 2 * SUBLANES


def _strips(gb, rb):
    if gb == 1:
        n = rb // STRIP_ROWS
        tok = lambda i: (slice(None), pl.ds(pl.multiple_of(i * STRIP_ROWS, STRIP_ROWS), STRIP_ROWS), slice(None))
        seq = lambda i: (slice(None), slice(None), slice(None))
    else:
        assert STRIP_ROWS % rb == 0
        per = STRIP_ROWS // rb
        n = gb // per
        tok = seq = lambda i: (pl.ds(i * per, per), slice(None), slice(None))
    return n, tok, seq


def _modnorm_body(x_ref, g_ref, sc_ref, sh_ref, o_ref, *, gb, rb):
    n, tok, seq = _strips(gb, rb)

    def strip(i, carry):
        y = _rms(x_ref[tok(i)], g_ref[...])
        o_ref[tok(i)] = (y * (1.0 + sc_ref[seq(i)]) + sh_ref[seq(i)]).astype(o_ref.dtype)
        return carry

    lax.fori_loop(0, n, strip, 0, unroll=4)


def _modnorm(x, g, mod, i_scale, i_shift, tm):
    G, R, D = x.shape
    gb, rb, nrb = _row_blocking(G, R, tm)
    return pl.pallas_call(
        functools.partial(_modnorm_body, gb=gb, rb=rb),
        grid=(G // gb, nrb),
        in_specs=[pl.BlockSpec((gb, rb, D), lambda a, b: (a, b, 0)),
                  pl.BlockSpec((1, 1, D), lambda a, b: (0, 0, 0)),
                  pl.BlockSpec((gb, 1, D), lambda a, b: (a, 0, i_scale)),
                  pl.BlockSpec((gb, 1, D), lambda a, b: (a, 0, i_shift))],
        out_specs=pl.BlockSpec((gb, rb, D), lambda a, b: (a, b, 0)),
        out_shape=jax.ShapeDtypeStruct((G, R, D), BF16),
        compiler_params=_cparams("arbitrary", "arbitrary"),
        name="modnorm",
    )(x, g.reshape(1, 1, D), mod, mod)


def _rmsnorm_body(x_ref, g_ref, o_ref, *, gb, rb):
    n, tok, _ = _strips(gb, rb)

    def strip(i, carry):
        o_ref[tok(i)] = _rms(x_ref[tok(i)], g_ref[...])
        return carry

    lax.fori_loop(0, n, strip, 0, unroll=4)


def _rmsnorm(x, g, tm):
    G, R, D = x.shape
    gb, rb, nrb = _row_blocking(G, R, tm)
    return pl.pallas_call(
        functools.partial(_rmsnorm_body, gb=gb, rb=rb),
        grid=(G // gb, nrb),
        in_specs=[pl.BlockSpec((gb, rb, D), lambda a, b: (a, b, 0)),
                  pl.BlockSpec((1, 1, D), lambda a, b: (0, 0, 0))],
        out_specs=pl.BlockSpec((gb, rb, D), lambda a, b: (a, b, 0)),
        out_shape=jax.ShapeDtypeStruct((G, R, D), F32),
        compiler_params=_cparams("arbitrary", "arbitrary"),
        name="final_norm",
    )(x, g.reshape(1, 1, D))


def _mm_body(a_ref, w_ref, *rest, act, w_transposed, side_cast):
    if side_cast:
        src_ref, o_ref, dst_ref, wbf = rest
        dst_ref[...] = src_ref[...].astype(BF16)
    else:
        o_ref, wbf = rest

    @pl.when(pl.program_id(1) == 0)
    def _():
        wbf[...] = (w_ref[0].T if w_transposed else w_ref[...]).astype(BF16)

    acc = jnp.dot(a_ref[...].astype(BF16), wbf[...], preferred_element_type=F32)
    if act == "relu2":
        acc = jnp.square(jnp.maximum(acc, 0.0))
    o_ref[...] = acc.astype(o_ref.dtype)


def _mm(a, w, layer, *, n_cols, tm, tn, act=None, out_dtype=F32, name="mm", w_transposed=False, col0=0,
        cast_src=None):
    G, R, K = a.shape
    M = G * R
    assert M % tm == 0 and n_cols % tn == 0 and w.shape[2 if w_transposed else 1] == K
    nm = M // tm
    if w_transposed:
        assert col0 % SUBLANES == 0
        w_spec = pl.BlockSpec((pl.Element(1), pl.Element(tn), pl.Element(K)),
                              lambda j, i: (layer, pl.multiple_of(col0 + j * tn, SUBLANES), 0))
    else:
        assert col0 == 0
        w_spec = pl.BlockSpec((None, K, tn), lambda j, i: (layer, 0, j))
    in_specs = [pl.BlockSpec((tm, K), lambda j, i: (i, 0)), w_spec]
    out_specs = [pl.BlockSpec((tm, tn), lambda j, i: (i, j))]
    out_shape = [jax.ShapeDtypeStruct((M, n_cols), out_dtype)]
    args = [a.reshape(M, K), w]
    if cast_src is not None:
        _, P, Q = cast_src.shape
        steps = (n_cols // tn) * nm
        assert P % steps == 0
        in_specs.append(pl.BlockSpec((None, P // steps, Q), lambda j, i: (layer, j * nm + i, 0)))
        out_specs.append(pl.BlockSpec((P // steps, Q), lambda j, i: (j * nm + i, 0)))
        out_shape.append(jax.ShapeDtypeStruct((P, Q), BF16))
        args.append(cast_src)
    res = pl.pallas_call(
        functools.partial(_mm_body, act=act, w_transposed=w_transposed, side_cast=cast_src is not None),
        grid=(n_cols // tn, nm),
        in_specs=in_specs,
        out_specs=out_specs,
        out_shape=out_shape,
        scratch_shapes=[pltpu.VMEM((K, tn), BF16)],
        compiler_params=_cparams("arbitrary", "arbitrary"),
        name=name,
    )(*args)
    out = res[0].reshape(G, R, n_cols)
    return (out, res[1]) if cast_src is not None else out


def _mm_res_body(*refs, n_lhs, gb, rb, cast_w):
    a_refs = refs[:n_lhs]
    w_ref, x_ref, gt_ref, o_ref = refs[n_lhs:n_lhs + 4]
    if cast_w:
        wbf = refs[n_lhs + 4]

        @pl.when(pl.program_id(1) == 0)
        def _():
            wbf[...] = w_ref[...].astype(BF16)
    else:
        wbf = w_ref

    acc = None
    k0 = 0
    for a_ref in a_refs:
        kk = a_ref.shape[-1]
        part = jnp.dot(a_ref[...].astype(BF16), wbf[k0:k0 + kk, :], preferred_element_type=F32)
        acc = part if acc is None else acc + part
        k0 += kk
    tn = acc.shape[-1]
    o_ref[...] = x_ref[...] + gt_ref[...] * acc.reshape(gb, rb, tn)


def _mm_residual(a_list, w, layer, x, mod, i_gate, *, tm, tn, name="mm_res"):
    G, R, N = x.shape
    M = G * R
    K = w.shape[1]
    assert sum(a.shape[-1] for a in a_list) == K and w.shape[2] == N
    gb, rb, nrb = _row_blocking(G, R, tm)
    ntn = N // tn
    cast_w = w.dtype != BF16

    def xmap(j, i):
        return (i // nrb, i % nrb, j)

    return pl.pallas_call(
        functools.partial(_mm_res_body, n_lhs=len(a_list), gb=gb, rb=rb, cast_w=cast_w),
        grid=(ntn, M // tm),
        in_specs=[pl.BlockSpec((tm, a.shape[-1]), lambda j, i: (i, 0)) for a in a_list]
        + [pl.BlockSpec((None, K, tn), lambda j, i: (layer, 0, j)),
           pl.BlockSpec((gb, rb, tn), xmap),
           pl.BlockSpec((gb, 1, tn), lambda j, i: (i // nrb, 0, i_gate * ntn + j))],
        out_specs=pl.BlockSpec((gb, rb, tn), xmap),
        out_shape=jax.ShapeDtypeStruct((G, R, N), F32),
        scratch_shapes=[pltpu.VMEM((K, tn), BF16)] if cast_w else [],
        compiler_params=_cparams("arbitrary", "arbitrary"),
        name=name,
    )(*[a.reshape(M, a.shape[-1]) for a in a_list], w, x, mod)


def _mlstm_body(zq_ref, zk_ref, zv_ref, zo_ref, zg_ref, cw_ref, gbias_ref, gain_ref, conv0_ref,
                C0_ref, n0_ref, m0_ref, hm_ref, C_ref, n_ref, m_ref, ext, *, L, gs):
    W = M_WIDTH
    Dh = M_HEAD_DIM
    H = M_HEADS
    keep = M_CONV - 1
    c = pl.program_id(1)

    @pl.when(c == 0)
    def _():
        C_ref[...] = C0_ref[...]
        n_ref[...] = n0_ref[...]
        m_ref[...] = m0_ref[...]
        ext[:, SUBLANES - keep:SUBLANES, :] = conv0_ref[...]

    ext[:, SUBLANES:SUBLANES + L, 0:W] = zq_ref[...]
    ext[:, SUBLANES:SUBLANES + L, W:2 * W] = zk_ref[...]
    row = lax.broadcasted_iota(jnp.int32, (L, L), 0)
    col = lax.broadcasted_iota(jnp.int32, (L, L), 1)
    eye = row == col
    tri = col <= row
    chains = [(g, h) for g in range(gs) for h in range(H)]
    ids = range(len(chains))
    qk, gates = [], []
    for g in range(gs):
        conv = ext[g, SUBLANES:SUBLANES + L, :] * cw_ref[keep:keep + 1, :]
        for j in range(keep):
            conv = conv + ext[g, SUBLANES - keep + j:SUBLANES - keep + j + L, :] * cw_ref[j:j + 1, :]
        qk.append(_silu(conv))
        gates.append(zg_ref[g] + gbias_ref[...])
    ext[:, SUBLANES - keep:SUBLANES, :] = ext[:, SUBLANES + L - keep:SUBLANES + L, :]

    sl = [slice(h * Dh, (h + 1) * Dh) for _, h in chains]
    q = [qk[g][:, h * Dh:(h + 1) * Dh] for g, h in chains]
    k = [qk[g][:, W + h * Dh:W + (h + 1) * Dh] * (Dh ** -0.5) for g, h in chains]
    v = [zv_ref[g, :, sl[i]] for i, (g, h) in enumerate(chains)]
    ig_col = [gates[g][:, h:h + 1] for g, h in chains]
    fpre = [gates[g][:, H + h:H + h + 1] for g, h in chains]
    lf_col = [jnp.minimum(x, 0.0) - jnp.log1p(jnp.exp(-jnp.abs(x))) for x in fpre]
    lf_row = [jnp.sum(jnp.where(eye, x, 0.0), axis=0, keepdims=True) for x in lf_col]
    ig_row = [jnp.sum(jnp.where(eye, x, 0.0), axis=0, keepdims=True) for x in ig_col]
    b_col = [jnp.sum(jnp.where(tri, x, 0.0), axis=1, keepdims=True) for x in lf_row]
    b_row = [jnp.sum(jnp.where(row <= col, x, 0.0), axis=0, keepdims=True) for x in lf_col]
    dmat = [jnp.where(tri, b_col[i] - b_row[i] + ig_row[i], NEG_BIG) for i in ids]
    m_prev = [m_ref[g, :, h:h + 1] for g, h in chains]
    inter = [b_col[i] + m_prev[i] for i in ids]
    m_t = [jnp.maximum(inter[i], jnp.max(dmat[i], axis=1, keepdims=True)) for i in ids]
    w_inter = [jnp.exp(inter[i] - m_t[i]) for i in ids]
    Cm = [C_ref[g, h] for g, h in chains]
    n_row = [n_ref[g, h:h + 1, :] for g, h in chains]
    s = [_dot_nt(q[i], k[i]) * jnp.exp(dmat[i] - m_t[i]) for i in ids]
    qc = [_dot(q[i], Cm[i]) for i in ids]
    num = [_dot(s[i], v[i]) + w_inter[i] * qc[i] for i in ids]
    den = [jnp.sum(s[i], axis=1, keepdims=True) + w_inter[i] * jnp.sum(q[i] * n_row[i], axis=1, keepdims=True)
           for i in ids]
    hh = [num[i] / jnp.maximum(jnp.abs(den[i]), jnp.exp(-m_t[i])) for i in ids]
    m_new = [x[L - 1:L, :] for x in m_t]
    b_last = [x[L - 1:L, :] for x in b_col]
    wk = [jnp.exp(b_last[i] - b_col[i] + ig_col[i] - m_new[i]) * k[i] for i in ids]
    decay = [jnp.exp(b_last[i] + m_prev[i] - m_new[i]) for i in ids]
    upd = [_dot_tn(wk[i], v[i]) for i in ids]
    for i, (g, h) in enumerate(chains):
        C_ref[g, h] = decay[i] * Cm[i] + upd[i]
        n_ref[g, h:h + 1, :] = decay[i] * n_row[i] + jnp.sum(wk[i], axis=0, keepdims=True)
        m_ref[g, :, h:h + 1] = m_new[i]
        dlt = hh[i] - jnp.mean(hh[i], axis=-1, keepdims=True)
        ln = dlt * lax.rsqrt(jnp.mean(dlt * dlt, axis=-1, keepdims=True) + RMS_EPS)
        hm_ref[g, :, sl[i]] = (_sigmoid(zo_ref[g, :, sl[i]]) * ln * gain_ref[:, sl[i]]).astype(hm_ref.dtype)


def _mlstm(zm, zgate, conv_w, gate_b, m_gain, conv0, C0, n0, m0):
    G, R, _ = zm.shape
    L = math.gcd(R, CHUNK)
    W = M_WIDTH
    gs = 2 if R > SUBLANES else 4
    gbias = jnp.zeros((1, LANES), F32).at[0, :2 * M_HEADS].set(gate_b.astype(F32))
    zspec = lambda blk: pl.BlockSpec((gs, L, W), lambda g, c: (g, c, blk))
    st4 = lambda g, c: (g, 0, 0, 0)
    st3 = lambda g, c: (g, 0, 0)
    return pl.pallas_call(
        functools.partial(_mlstm_body, L=L, gs=gs),
        grid=(G // gs, R // L),
        in_specs=[zspec(0), zspec(1), zspec(2), zspec(3),
                  pl.BlockSpec((gs, L, LANES), lambda g, c: (g, c, 0)),
                  pl.BlockSpec((M_CONV, 2 * W), lambda g, c: (0, 0)),
                  pl.BlockSpec((1, LANES), lambda g, c: (0, 0)),
                  pl.BlockSpec((1, W), lambda g, c: (0, 0)),
                  pl.BlockSpec((gs, M_CONV - 1, 2 * W), st3),
                  pl.BlockSpec((gs, M_HEADS, M_HEAD_DIM, M_HEAD_DIM), st4),
                  pl.BlockSpec((gs, M_HEADS, M_HEAD_DIM), st3),
                  pl.BlockSpec((gs, 1, M_HEADS), st3)],
        out_specs=[pl.BlockSpec((gs, L, W), lambda g, c: (g, c, 0)),
                   pl.BlockSpec((gs, M_HEADS, M_HEAD_DIM, M_HEAD_DIM), st4),
                   pl.BlockSpec((gs, M_HEADS, M_HEAD_DIM), st3),
                   pl.BlockSpec((gs, 1, M_HEADS), st3)],
        out_shape=[jax.ShapeDtypeStruct((G, R, W), BF16),
                   jax.ShapeDtypeStruct((G, M_HEADS, M_HEAD_DIM, M_HEAD_DIM), F32),
                   jax.ShapeDtypeStruct((G, M_HEADS, M_HEAD_DIM), F32),
                   jax.ShapeDtypeStruct((G, 1, M_HEADS), F32)],
        scratch_shapes=[pltpu.VMEM((gs, SUBLANES + L, 2 * W), F32)],
        compiler_params=_cparams("arbitrary", "arbitrary"),
        name="mlstm",
    )(zm, zm, zm, zm, zgate, conv_w, gbias, m_gain.reshape(1, W), conv0, C0, n0, m0.reshape(G, 1, M_HEADS))


def _hgrn_body(zq_ref, zf_ref, zi_ref, zgg_ref, lb_ref, gain_ref, S0_ref, og_ref, S_ref, ST, *, nblk, gs, st_t, nh):
    Dh = G_HEAD_DIM
    B = SUBLANES
    c = pl.program_id(1)
    chains = [(g, h) for g in range(gs) for h in range(G_HEADS)]

    @pl.when(c == 0)
    def _():
        for i, (g, h) in enumerate(chains):
            ST[i] = S0_ref[g, h].T if st_t else S0_ref[g, h]

    rowi = lax.broadcasted_iota(jnp.int32, (B, Dh), 0)
    ids = range(len(chains))
    sl = [slice(h * Dh, (h + 1) * Dh) for _, h in chains]
    gi = [g for g, _ in chains]

    def diag(bc, qh, kk, v, o):
        for s in range(B):
            p = [jnp.exp(jnp.where(rowi >= s, bc[i] - bc[i][s:s + 1, :], NEG_BIG)) * qh[i] * kk[i][s:s + 1, :]
                 for i in ids]
            o = [o[i] + jnp.sum(p[i], axis=-1, keepdims=True) * v[i][s:s + 1, :] for i in ids]
        return o

    def blk(bi, carry):
        r0 = pl.multiple_of(bi * (nh * B), nh * B)
        st = [ST[i] for i in ids]
        f, kk, qh, v, bc = [], [], [], [], []
        for hf in range(nh):
            rows = pl.ds(r0 + hf * B, B)
            f_h = [lb_ref[:, sl[i]] + (1.0 - lb_ref[:, sl[i]]) * _sigmoid(zf_ref[gi[i], rows, sl[i]]) for i in ids]
            kk.append([1.0 - x for x in f_h])
            loc = [_cumsum_rows(jnp.log(x)) for x in f_h]
            bc.append(loc if hf == 0 else [loc[i] + bc[hf - 1][i][B - 1:B, :] for i in ids])
            qh.append([_silu(zq_ref[gi[i], rows, sl[i]]) * (Dh ** -0.5) for i in ids])
            v.append([zi_ref[gi[i], rows, sl[i]] for i in ids])
        btot = [bc[nh - 1][i][B - 1:B, :] for i in ids]
        cat = lambda parts, i: parts[0][i] if nh == 1 else jnp.concatenate([pt[i] for pt in parts], axis=0)
        qd = [cat([[qh[hf][i] * jnp.exp(bc[hf][i]) for i in ids] for hf in range(nh)], i) for i in ids]
        kd = [cat([[kk[hf][i] * jnp.exp(btot[i] - bc[hf][i]) for i in ids] for hf in range(nh)], i) for i in ids]
        vv = [cat(v, i) for i in ids]
        if st_t:
            os_ = [_dot_nt(qd[i], st[i]) for i in ids]
            upd = [_dot_tn(vv[i], kd[i]) for i in ids]
            dec = [jnp.exp(btot[i]) for i in ids]
        else:
            os_ = [_dot(qd[i], st[i]) for i in ids]
            upd = [_dot_tn(kd[i], vv[i]) for i in ids]
            dec = [jnp.transpose(jnp.broadcast_to(jnp.exp(btot[i]), (B, Dh)))[:, 0:1] for i in ids]
        o = []
        for hf in range(nh):
            oh = diag(bc[hf], qh[hf], kk[hf], v[hf], [os_[i][hf * B:(hf + 1) * B, :] for i in ids])
            for prev in range(hf):
                edge = [bc[prev][i][B - 1:B, :] for i in ids]
                ax = [_dot_nt(qh[hf][i] * jnp.exp(bc[hf][i] - edge[i]), kk[prev][i] * jnp.exp(edge[i] - bc[prev][i]))
                      for i in ids]
                oh = [oh[i] + _dot(ax[i], v[prev][i]) for i in ids]
            o.append(oh)
        for i in ids:
            ST[i] = st[i] * dec[i] + upd[i]
        per = min(nh, 2)
        for h0 in range(0, nh, per):
            rows = pl.ds(r0 + h0 * B, per * B)
            for i in ids:
                oi = o[h0][i] if per == 1 else jnp.concatenate([o[h0][i], o[h0 + 1][i]], axis=0)
                rms = oi * lax.rsqrt(jnp.mean(oi * oi, axis=-1, keepdims=True) + RMS_EPS)
                og_ref[gi[i], rows, sl[i]] = (rms * gain_ref[:, sl[i]]
                                              * _silu(zgg_ref[gi[i], rows, sl[i]])).astype(og_ref.dtype)
        return carry

    if nblk == 1:
        blk(0, 0)
    else:
        lax.fori_loop(0, nblk, blk, 0)

    @pl.when(c == pl.num_programs(1) - 1)
    def _():
        for i, (g, h) in enumerate(chains):
            S_ref[g, h] = ST[i].T if st_t else ST[i]


def _hgrn(zg, lb, g_gain, S0):
    G, R, _ = zg.shape
    W = G_WIDTH
    gs = 4
    Rc = min(R, 128)
    nh = 4 if R > SUBLANES else 1
    zspec = lambda blk: pl.BlockSpec((gs, Rc, W), lambda g, c: (g, c, blk))
    st4 = lambda g, c: (g, 0, 0, 0)
    return pl.pallas_call(
        functools.partial(_hgrn_body, nblk=Rc // (nh * SUBLANES), gs=gs, st_t=R > SUBLANES, nh=nh),
        grid=(G // gs, R // Rc),
        in_specs=[zspec(0), zspec(1), zspec(2), zspec(3),
                  pl.BlockSpec((1, W), lambda g, c: (0, 0)),
                  pl.BlockSpec((1, W), lambda g, c: (0, 0)),
                  pl.BlockSpec((gs, G_HEADS, G_HEAD_DIM, G_HEAD_DIM), st4)],
        out_specs=[pl.BlockSpec((gs, Rc, W), lambda g, c: (g, c, 0)),
                   pl.BlockSpec((gs, G_HEADS, G_HEAD_DIM, G_HEAD_DIM), st4)],
        out_shape=[jax.ShapeDtypeStruct((G, R, W), BF16),
                   jax.ShapeDtypeStruct((G, G_HEADS, G_HEAD_DIM, G_HEAD_DIM), F32)],
        scratch_shapes=[pltpu.VMEM((gs * G_HEADS, G_HEAD_DIM, G_HEAD_DIM), F32)],
        compiler_params=_cparams("arbitrary", "arbitrary"),
        name="hgrn2",
    )(zg, zg, zg, zg, lb.reshape(1, W), g_gain.reshape(1, W), S0)


def _rprep_body(x_ref, g_ref, sc_ref, sh_ref, shift0_ref, mu_ref, *refs, gb, rb):
    outs = refs[:6]
    hlast_ref, hbuf = refs[6:]
    B = SUBLANES
    n, tok, seq = _strips(gb, rb)
    S = hbuf.shape[1] - B
    whole_sequences = gb > 1

    if not whole_sequences:
        @pl.when(pl.program_id(1) == 0)
        def _():
            hbuf[:, B - 1:B, :] = shift0_ref[...]

    first = lax.broadcasted_iota(jnp.int32, (hbuf.shape[0], S, 1), 1) == 0

    def strip(i, carry):
        h = _rms(x_ref[tok(i)], g_ref[...]) * (1.0 + sc_ref[seq(i)]) + sh_ref[seq(i)]
        before = shift0_ref[seq(i)] if whole_sequences else hbuf[:, B - 1:B, :]
        xx = jnp.where(first, before, pltpu.roll(h, 1, axis=1)) - h
        last = h[:, S - 1:S, :]
        if whole_sequences:
            hlast_ref[seq(i)] = last
        else:
            hbuf[:, B - 1:B, :] = last
        for m in range(6):
            outs[m][tok(i)] = (h + xx * mu_ref[m:m + 1, :]).astype(BF16)
        return carry

    lax.fori_loop(0, n, strip, 0, unroll=4)
    if not whole_sequences:
        hlast_ref[...] = hbuf[:, B - 1:B, :]


def _rwkv_prep(x, g, mod, i_scale, i_shift, shift0, mu, tm):
    G, R, D = x.shape
    gb, rb, nrb = _row_blocking(G, R, tm)
    tok = pl.BlockSpec((gb, rb, D), lambda a, b: (a, b, 0))
    one = pl.BlockSpec((gb, 1, D), lambda a, b: (a, 0, 0))
    strip_shape = (1, SUBLANES + STRIP_ROWS, D) if gb == 1 else (STRIP_ROWS // rb, SUBLANES + rb, D)
    res = pl.pallas_call(
        functools.partial(_rprep_body, gb=gb, rb=rb),
        grid=(G // gb, nrb),
        in_specs=[tok,
                  pl.BlockSpec((1, 1, D), lambda a, b: (0, 0, 0)),
                  pl.BlockSpec((gb, 1, D), lambda a, b: (a, 0, i_scale)),
                  pl.BlockSpec((gb, 1, D), lambda a, b: (a, 0, i_shift)),
                  one,
                  pl.BlockSpec((6, D), lambda a, b: (0, 0))],
        out_specs=[tok] * 6 + [one],
        out_shape=[jax.ShapeDtypeStruct((G, R, D), BF16)] * 6 + [jax.ShapeDtypeStruct((G, 1, D), F32)],
        scratch_shapes=[pltpu.VMEM(strip_shape, F32)],
        compiler_params=_cparams("arbitrary", "arbitrary"),
        name="rwkv_prep",
    )(x, g.reshape(1, 1, D), mod, mod, shift0.reshape(G, 1, D), mu)
    return res[:6], res[6]


def _lora_body(x_ref, w1_ref, w2_ref, b_ref, o_ref, *, act, post):
    t = _dot(x_ref[...], w1_ref[...])
    if act == "tanh":
        t = jnp.tanh(t)
    elif act == "sigmoid":
        t = _sigmoid(t)
    y = _dot(t, w2_ref[...])
    if post == "log_decay":
        y = -math.exp(-0.5) * _sigmoid(b_ref[...] + y)
    elif post == "sigmoid":
        y = _sigmoid(b_ref[...] + y)
    o_ref[...] = y


def _lora(x, w1, w2, act, tm, post=None, bias=None):
    G, R, K = x.shape
    M = G * R
    r = w1.shape[1]
    rp = -(-r // LANES) * LANES
    w1p = jnp.zeros((K, rp), F32).at[:, :r].set(w1)
    w2p = jnp.zeros((rp, w2.shape[1]), F32).at[:r, :].set(w2)
    N = w2.shape[1]
    b = jnp.zeros((1, N), F32) if bias is None else bias.reshape(1, N).astype(F32)
    out = pl.pallas_call(
        functools.partial(_lora_body, act=act, post=post),
        grid=(M // tm,),
        in_specs=[pl.BlockSpec((tm, K), lambda i: (i, 0)),
                  pl.BlockSpec((K, rp), lambda i: (0, 0)),
                  pl.BlockSpec((rp, N), lambda i: (0, 0)),
                  pl.BlockSpec((1, N), lambda i: (0, 0))],
        out_specs=pl.BlockSpec((tm, N), lambda i: (i, 0)),
        out_shape=jax.ShapeDtypeStruct((M, N), F32),
        compiler_params=_cparams("arbitrary"),
        name="lora_" + str(act),
    )(x.reshape(M, K), w1p, w2p, b)
    return out.reshape(G, R, N)


def _head_mask(rows, cols, rper, cper):
    r = lax.broadcasted_iota(jnp.int32, (rows, cols), 0)
    c = lax.broadcasted_iota(jnp.int32, (rows, cols), 1)
    return (r // rper) == (c // cper)


def _bd(y, mask01):
    if y.shape[0] % (2 * SUBLANES) == 0:
        return jnp.concatenate([y.astype(BF16)] * R_GROUP, axis=0) * mask01
    return jnp.concatenate([y] * R_GROUP, axis=0).astype(BF16) * mask01


def _segsums(xs, ones_bd):
    c = xs[0].shape[0]
    if c % (2 * SUBLANES) == 0:
        lhs = jnp.concatenate([x.astype(BF16) for x in xs], axis=0)
    else:
        lhs = jnp.concatenate(xs, axis=0).astype(BF16)
    res = jnp.dot(lhs, ones_bd, preferred_element_type=F32)
    return [res[i * c:(i + 1) * c] for i in range(len(xs))]


def _rwkv_body(r_ref, k_ref, v_ref, lw_ref, a_ref, g_ref, kkp_ref, kap_ref, rk_ref,
               lnw_ref, lnb_ref, S0_ref, y_ref, S_ref, Sbd, *, c, nchunk, ng, gs):
    N = R_HEAD_DIM
    GW = R_GROUP_W
    j = pl.program_id(2)
    m_state = _head_mask(GW, GW, N, N)
    chains = [(q, gi) for q in range(gs) for gi in range(ng)]
    ids = range(len(chains))
    lanes = [slice(gi * GW, (gi + 1) * GW) for _, gi in chains]
    seqs = [q for q, _ in chains]

    @pl.when(j == 0)
    def _():
        for i, (q, gi) in enumerate(chains):
            s0 = S0_ref[q, gi * R_GROUP:(gi + 1) * R_GROUP].reshape(GW, N)
            Sbd[i] = jnp.where(m_state, jnp.concatenate([s0] * R_GROUP, axis=1), 0.0)

    ones_bd = jnp.where(m_state, 1.0, 0.0).astype(BF16)
    m_vec = jnp.where(_head_mask(R_GROUP * c, GW, c, N), 1.0, 0.0).astype(BF16)
    m_mat = jnp.where(_head_mask(R_GROUP * c, R_GROUP * c, c, c), 1.0, 0.0).astype(BF16)
    t_idx = lax.broadcasted_iota(jnp.int32, (2 * c, R_GROUP * c), 0)
    s_idx = lax.broadcasted_iota(jnp.int32, (2 * c, R_GROUP * c), 1) % c
    causal = jnp.where(t_idx < c, jnp.where(s_idx < t_idx, 1.0, 0.0), jnp.where(s_idx <= t_idx - c, 1.0, 0.0))
    nt = (((1,), (1,)), ((), ()))
    mm = lambda a, b: jnp.dot(a.astype(BF16), b, preferred_element_type=F32)

    tril = jnp.where(lax.broadcasted_iota(jnp.int32, (c, c), 1) <= lax.broadcasted_iota(jnp.int32, (c, c), 0),
                     1.0, 0.0).astype(BF16)
    eye = jnp.where(lax.broadcasted_iota(jnp.int32, (c, R_GROUP * c), 1) % c
                    == lax.broadcasted_iota(jnp.int32, (c, R_GROUP * c), 0), 1.0, 0.0)
    keys = ("ar", "bdb", "bdk", "bdv", "vb", "gam", "tail")

    def prepare_steps(ci, out):
        rows = pl.ds(pl.multiple_of(ci * c, c), c)
        ld = lambda ref: [ref[seqs[i], rows, lanes[i]] for i in ids]
        r, k, a = ld(r_ref), ld(k_ref), ld(a_ref)
        kk = [k[i] * kkp_ref[:, lanes[i]] for i in ids]
        k2 = [k[i] * (1.0 + (a[i] - 1.0) * kap_ref[:, lanes[i]]) for i in ids]
        sums = _segsums([kk[i] * kk[i] for i in ids] + [r[i] * k2[i] * rk_ref[:, lanes[i]] for i in ids], ones_bd)
        yield
        for i in ids:
            v, lw, g = v_ref[seqs[i], rows, lanes[i]], lw_ref[seqs[i], rows, lanes[i]], g_ref[seqs[i], rows, lanes[i]]
            hi = lw.astype(BF16)
            lo = (lw - hi.astype(F32)).astype(BF16)
            cw2 = jnp.dot(tril, jnp.concatenate([hi, lo], axis=1), preferred_element_type=F32)
            cw = cw2[:, :GW] + cw2[:, GW:]
            gam = jnp.exp(cw[c - 1:c, :])
            kn = kk[i] * lax.rsqrt(jnp.maximum(sums[i], 1e-24))
            bb = kn * a[i]
            e_out = jnp.exp(-cw)
            e_end = gam * e_out
            out["ar"].append(jnp.concatenate([-kn * jnp.exp(cw - lw), r[i] * jnp.exp(cw)], axis=0).astype(BF16))
            out["gam"].append(gam)
            out["tail"].append(jnp.concatenate([sums[len(chains) + i] * v, g], axis=0))
            yield
            out["bdb"].append(_bd(bb * e_out, m_vec))
            out["bdk"].append(_bd(k2[i] * e_out, m_vec))
            out["bdv"].append(_bd(v, m_vec))
            out["vb"].append(jnp.concatenate([v, bb * e_end, k2[i] * e_end], axis=0).astype(BF16))
            yield

    def prepare(ci):
        out = {key: [] for key in keys}
        for _ in prepare_steps(ci, out):
            pass
        return out

    def chain(ci, p, side_work=None):
        def tick():
            if side_work is not None:
                next(side_work, None)

        rows = pl.ds(pl.multiple_of(ci * c, c), c)
        ar, bdv = p["ar"], p["bdv"]
        sbd = [Sbd[i] for i in ids]
        pb = [lax.dot_general(ar[i], p["bdb"][i], nt, preferred_element_type=F32) * causal for i in ids]
        pk = [(lax.dot_general(ar[i], p["bdk"][i], nt, preferred_element_type=F32) * causal).astype(BF16) for i in ids]
        tick()
        s0p = [lax.dot_general(ar[i], sbd[i].astype(BF16), nt, preferred_element_type=F32) for i in ids]
        kv = [mm(pk[i], bdv[i]) for i in ids]
        w = [s0p[i][:c] + kv[i][:c] for i in ids]
        tick()
        pw = [pb[i][:c] for i in ids]
        tinv = [eye + pw[i] for i in ids]
        levels = c.bit_length() - 1
        bdp = [_bd(pw[i], m_mat) for i in ids]
        pw = [mm(pw[i], bdp[i]) for i in ids]
        tick()
        for lvl in range(1, levels):
            bdp = [_bd(pw[i], m_mat) for i in ids]
            if lvl < levels - 1:
                res = [mm(jnp.concatenate([tinv[i], pw[i]], axis=0), bdp[i]) for i in ids]
                tinv = [tinv[i] + res[i][:c] for i in ids]
                pw = [res[i][c:] for i in ids]
            else:
                tinv = [tinv[i] + mm(tinv[i], bdp[i]) for i in ids]
            tick()
        u = [mm(tinv[i], _bd(w[i], m_vec)).astype(BF16) for i in ids]
        tick()
        y = [s0p[i][c:] + mm(pb[i][c:], _bd(u[i], m_vec)) + kv[i][c:] for i in ids]
        tick()
        upd = [_dot_tn(jnp.concatenate([u[i], p["vb"][i][:c]], axis=0), p["vb"][i][c:]) for i in ids]
        for i in ids:
            Sbd[i] = sbd[i] * p["gam"][i] + jnp.where(m_state, upd[i], 0.0)
        tick()
        ysum = _segsums(y, ones_bd)
        dlt = [y[i] - ysum[i] * (1.0 / N) for i in ids]
        vsum = _segsums([dlt[i] * dlt[i] for i in ids], ones_bd)
        for i in ids:
            yn = dlt[i] * lax.rsqrt(vsum[i] * (1.0 / N) + LN_X_EPS) * lnw_ref[:, lanes[i]] + lnb_ref[:, lanes[i]]
            y_ref[seqs[i], rows, lanes[i]] = ((yn + p["tail"][i][:c]) * p["tail"][i][c:]).astype(y_ref.dtype)
        if side_work is not None:
            for _ in side_work:
                pass

    if nchunk == 1:
        chain(0, prepare(0))
    else:
        def body(ci, p):
            nxt = {key: [] for key in keys}
            chain(ci, p, prepare_steps(jnp.minimum(ci + 1, nchunk - 1), nxt))
            return nxt

        lax.fori_loop(0, nchunk, body, prepare(0))

    @pl.when(j == pl.num_programs(2) - 1)
    def _():
        for i, (q, gi) in enumerate(chains):
            sbd = Sbd[i]
            for h in range(R_GROUP):
                S_ref[q, gi * R_GROUP + h] = sbd[h * N:(h + 1) * N, h * N:(h + 1) * N]


def _rwkv(r, k, v, lw, a, g, kkp, kap, rk, lnw, lnb, S0):
    G, R, D = r.shape
    c = math.gcd(R, CHUNK)
    GW = R_GROUP_W
    if R > SUBLANES:
        gs, ng, Rc = 1, 8, min(R, 256)
    else:
        gs, ng, Rc = 4, D // GW, R
    bw = ng * GW
    tok = pl.BlockSpec((gs, Rc, bw), lambda a_, b, j: (a_, j, b))
    par = pl.BlockSpec((1, bw), lambda a_, b, j: (0, b))
    st = pl.BlockSpec((gs, ng * R_GROUP, R_HEAD_DIM, R_HEAD_DIM), lambda a_, b, j: (a_, b, 0, 0))
    row = lambda p: p.reshape(1, D).astype(F32)
    return pl.pallas_call(
        functools.partial(_rwkv_body, c=c, nchunk=Rc // c, ng=ng, gs=gs),
        grid=(G // gs, D // bw, R // Rc),
        in_specs=[tok] * 6 + [par] * 5 + [st],
        out_specs=[tok, st],
        out_shape=[jax.ShapeDtypeStruct((G, R, D), BF16),
                   jax.ShapeDtypeStruct((G, R_HEADS, R_HEAD_DIM, R_HEAD_DIM), F32)],
        scratch_shapes=[pltpu.VMEM((gs * ng, GW, GW), F32)],
        compiler_params=_cparams("arbitrary", "arbitrary", "arbitrary"),
        name="rwkv7",
    )(r, k, v, lw, a, g, row(kkp), row(kap), row(rk), row(lnw), row(lnb), S0)


def _trunk(x, mod, m_C, m_n, m_m, m_conv, g_S, r_S, r_shift, p, lbs, tm, w2_bf16=None):
    make_w2 = w2_bf16 is None
    w2_bf16 = [] if make_w2 else w2_bf16
    G, R, D = x.shape
    md = mod[0]
    h = _modnorm(x, p["norm_mix"][0], md, 1, 0, tm)
    w_in_t = jnp.swapaxes(p["ab_w_in"], 1, 2)
    zm = _mm(h, w_in_t, 0, n_cols=4 * M_WIDTH, tm=tm, tn=1024, name="mm_in_m", w_transposed=True)
    zgate = _mm(h, w_in_t, 0, n_cols=LANES, tm=tm, tn=LANES, name="mm_in_gate", w_transposed=True, col0=4 * M_WIDTH)
    zg = _mm(h, w_in_t, 0, n_cols=4 * G_WIDTH, tm=tm, tn=1024, name="mm_in_g", w_transposed=True,
             col0=4 * M_WIDTH + 2 * M_HEADS)
    hm, C, n, m = _mlstm(zm, zgate, p["m_conv_w"][0], p["ab_gate_b"][0], p["m_norm"][0],
                         m_conv[0], m_C[0], m_n[0], m_m[0])
    conv_new = zm[:, R - (M_CONV - 1):, :2 * M_WIDTH]
    og, S = _hgrn(zg, lbs[0], p["g_norm"][0], g_S[0])
    x = _mm_residual([hm, og], p["ab_w_out"], 0, x, md, 2, tm=tm, tn=1024, name="mm_out0")
    h = _modnorm(x, p["norm_ffn"][0], md, 4, 3, tm)
    act = _mm(h, p["ffn_w1"], 0, n_cols=4 * D, tm=tm, tn=1024, act="relu2", out_dtype=BF16, name="ffn_up",
              cast_src=p["ffn_w2"] if make_w2 else None)
    if make_w2:
        act, w2 = act
        w2_bf16.append(w2[None])
    x = _mm_residual([act], w2_bf16[0], 0, x, md, 5, tm=256, tn=1024, name="ffn_down")
    md = mod[1]
    (xr, xw, xk, xv, xa, xg), shift_new = _rwkv_prep(x, p["norm_mix"][1], md, 1, 0, r_shift[0], p["r_mu"][0],
                                                     min(tm, 256))
    r = _mm(xr, p["r_wr"], 0, n_cols=D, tm=tm, tn=1024, name="mm_r")
    k = _mm(xk, p["r_wk"], 0, n_cols=D, tm=tm, tn=1024, name="mm_k")
    v = _mm(xv, p["r_wv"], 0, n_cols=D, tm=tm, tn=1024, name="mm_v")
    lw = _lora(xw, p["r_w1"][0], p["r_w2"][0], "tanh", min(tm, 512), post="log_decay", bias=p["r_w0"][0])
    aa = _lora(xa, p["r_a1"][0], p["r_a2"][0], None, min(tm, 512), post="sigmoid", bias=p["r_a0"][0])
    gg = _lora(xg, p["r_g1"][0], p["r_g2"][0], "sigmoid", min(tm, 512))
    yg, rS = _rwkv(r, k, v, lw, aa, gg, p["r_kk"][0], p["r_ka"][0], p["r_rk"][0].reshape(-1), p["r_lnw"][0],
                   p["r_lnb"][0], r_S[0])
    x = _mm_residual([yg], p["r_wo"], 0, x, md, 2, tm=tm, tn=1024, name="mm_out1")
    h = _modnorm(x, p["norm_ffn"][1], md, 4, 3, tm)
    act = _mm(h, p["ffn_w1"], 1, n_cols=4 * D, tm=tm, tn=1024, act="relu2", out_dtype=BF16, name="ffn_up",
              cast_src=p["ffn_w2"] if make_w2 else None)
    if make_w2:
        act, w2 = act
        w2_bf16.append(w2[None])
    x = _mm_residual([act], w2_bf16[1], 0, x, md, 5, tm=256, tn=1024, name="ffn_down")
    y = _rmsnorm(x, p["final_norm"], min(tm, 512))
    return y, (C[None], n[None], m.reshape(1, G, M_HEADS), conv_new[None], S[None], rS[None],
               shift_new.reshape(1, G, D)), w2_bf16


def kernel(x_prompt, x_sample, c_prompt, c_sample, state_mlstm_C, state_mlstm_n, state_mlstm_m, state_mlstm_conv, state_hgrn_S, state_rwkv_S, state_rwkv_shift, mod_w, mod_b, norm_mix, norm_ffn, ffn_w1, ffn_w2, final_norm, ab_w_in, ab_gate_b, m_conv_w, m_norm, g_lb, g_norm, ab_w_out, r_mu, r_w0, r_w1, r_w2, r_a0, r_a1, r_a2, r_g1, r_g2, r_kk, r_ka, r_rk, r_wr, r_wk, r_wv, r_wo, r_lnw, r_lnb):
    p = dict(norm_mix=norm_mix, norm_ffn=norm_ffn, ffn_w1=ffn_w1, ffn_w2=ffn_w2, final_norm=final_norm,
             ab_w_in=ab_w_in, ab_gate_b=ab_gate_b, m_conv_w=m_conv_w, m_norm=m_norm, g_norm=g_norm,
             ab_w_out=ab_w_out, r_mu=r_mu, r_w0=r_w0, r_w1=r_w1, r_w2=r_w2, r_a0=r_a0, r_a1=r_a1, r_a2=r_a2,
             r_g1=r_g1, r_g2=r_g2, r_kk=r_kk, r_ka=r_ka, r_rk=r_rk, r_wr=r_wr, r_wk=r_wk, r_wv=r_wv,
             r_wo=r_wo, r_lnw=r_lnw, r_lnb=r_lnb)
    B, T, D = x_prompt.shape
    Bs = x_sample.shape[0]
    depth = mod_w.shape[0]
    pad = (-(Bs + B)) % SUBLANES
    c_all = jnp.concatenate([c_sample, c_prompt, jnp.zeros((pad, D), F32)], axis=0)
    mod = _modulation(c_all, mod_w, mod_b)
    mod_s = mod[:, :Bs].reshape(depth, Bs, 1, N_MOD * D)
    mod_p = mod[:, Bs:Bs + B].reshape(depth, B, 1, N_MOD * D)
    lbs = jnp.cumsum(jax.nn.softmax(g_lb.astype(F32), axis=0), axis=0)

    n_even = state_mlstm_C.shape[0]
    n_odd = state_rwkv_S.shape[0]
    z = lambda *s: jnp.zeros(s, F32)
    yp, sp, w2_bf16 = _trunk(x_prompt, mod_p,
                    z(n_even, B, M_HEADS, M_HEAD_DIM, M_HEAD_DIM), z(n_even, B, M_HEADS, M_HEAD_DIM),
                    z(n_even, B, M_HEADS), z(n_even, B, M_CONV - 1, 2 * M_WIDTH),
                    z(n_even, B, G_HEADS, G_HEAD_DIM, G_HEAD_DIM), z(n_odd, B, R_HEADS, R_HEAD_DIM, R_HEAD_DIM),
                    z(n_odd, B, D), p, lbs, 1024)
    ys, ss, _ = _trunk(x_sample, mod_s, state_mlstm_C, state_mlstm_n, state_mlstm_m, state_mlstm_conv,
                       state_hgrn_S, state_rwkv_S, state_rwkv_shift, p, lbs, 1024, w2_bf16)
    return (yp, ys) + tuple(sp) + tuple(ss)
```

```python
import functools
import math

import jax
import jax.numpy as jnp
from jax import lax
from jax.experimental import pallas as pl
from jax.experimental.pallas import tpu as pltpu

F32 = jnp.float32
BF16 = jnp.bfloat16

D_MODEL = 2048
M_HEADS = 4
M_HEAD_DIM = 256
M_WIDTH = M_HEADS * M_HEAD_DIM
M_CONV = 4
G_HEADS = 8
G_HEAD_DIM = 128
G_WIDTH = G_HEADS * G_HEAD_DIM
R_HEAD_DIM = 64
R_HEADS = D_MODEL // R_HEAD_DIM
R_GROUP = 4
R_GROUP_W = R_GROUP * R_HEAD_DIM
N_MOD = 6
RMS_EPS = 1e-6
LN_X_EPS = 64e-5
CHUNK = 64
NEG_BIG = -1e30

V7X_VMEM_LIMIT_BYTES = 56 * 1024 * 1024
SUBLANES = 8
LANES = 128


def _cparams(*sem):
    return pltpu.CompilerParams(dimension_semantics=sem, vmem_limit_bytes=V7X_VMEM_LIMIT_BYTES)


def _sigmoid(x):
    return 1.0 / (1.0 + jnp.exp(-x))


def _silu(x):
    return x * _sigmoid(x)


def _softplus(x):
    return jnp.maximum(x, 0.0) + jnp.log1p(jnp.exp(-jnp.abs(x)))


def _dot(a, b):
    return jnp.dot(a.astype(BF16), b.astype(BF16), preferred_element_type=F32)


def _dot_nt(a, b):
    return lax.dot_general(a.astype(BF16), b.astype(BF16), (((1,), (1,)), ((), ())), preferred_element_type=F32)


def _dot_tn(a, b):
    return lax.dot_general(a.astype(BF16), b.astype(BF16), (((0,), (0,)), ((), ())), preferred_element_type=F32)


def _cumsum_rows(x):
    n = x.shape[0]
    row = lax.broadcasted_iota(jnp.int32, x.shape, 0)
    s = 1
    while s < n:
        x = x + jnp.where(row >= s, pltpu.roll(x, s, axis=0), 0.0)
        s *= 2
    return x


def _row_blocking(G, R, tm):
    if R >= tm:
        assert R % tm == 0
        return 1, tm, R // tm
    assert tm % R == 0 and G % (tm // R) == 0
    return tm // R, R, 1


def _mod_body(c_ref, w_ref, b_ref, o_ref):
    sc = _silu(c_ref[...])
    o_ref[...] = _dot(sc, w_ref[...]) + b_ref[...]


def _modulation(c_all, mod_w, mod_b):
    L, K, N = mod_w.shape
    Mc = c_all.shape[0]
    tn = 1024
    return pl.pallas_call(
        _mod_body,
        grid=(L, N // tn),
        in_specs=[pl.BlockSpec((Mc, K), lambda l, j: (0, 0)),
                  pl.BlockSpec((None, K, tn), lambda l, j: (l, 0, j)),
                  pl.BlockSpec((None, 1, tn), lambda l, j: (l, 0, j))],
        out_specs=pl.BlockSpec((None, Mc, tn), lambda l, j: (l, 0, j)),
        out_shape=jax.ShapeDtypeStruct((L, Mc, N), F32),
        compiler_params=_cparams("arbitrary", "arbitrary"),
        name="modulation",
    )(c_all, mod_w, mod_b.reshape(L, 1, N))


def _rms(x, g):
    ms = jnp.mean(x * x, axis=-1, keepdims=True)
    return x * lax.rsqrt(ms + RMS_EPS) * g


STRIP_ROWS = 2 * SUBLANES


def _strips(gb, rb):
    if gb == 1:
        n = rb // STRIP_ROWS
        tok = lambda i: (slice(None), pl.ds(pl.multiple_of(i * STRIP_ROWS, STRIP_ROWS), STRIP_ROWS), slice(None))
        seq = lambda i: (slice(None), slice(None), slice(None))
    else:
        assert STRIP_ROWS % rb == 0
        per = STRIP_ROWS // rb
        n = gb // per
        tok = seq = lambda i: (pl.ds(i * per, per), slice(None), slice(None))
    return n, tok, seq


def _modnorm_body(x_ref, g_ref, sc_ref, sh_ref, o_ref, *, gb, rb):
    n, tok, seq = _strips(gb, rb)

    def strip(i, carry):
        y = _rms(x_ref[tok(i)], g_ref[...])
        o_ref[tok(i)] = (y * (1.0 + sc_ref[seq(i)]) + sh_ref[seq(i)]).astype(o_ref.dtype)
        return carry

    lax.fori_loop(0, n, strip, 0, unroll=4)


def _modnorm(x, g, mod, i_scale, i_shift, tm):
    G, R, D = x.shape
    gb, rb, nrb = _row_blocking(G, R, tm)
    return pl.pallas_call(
        functools.partial(_modnorm_body, gb=gb, rb=rb),
        grid=(G // gb, nrb),
        in_specs=[pl.BlockSpec((gb, rb, D), lambda a, b: (a, b, 0)),
                  pl.BlockSpec((1, 1, D), lambda a, b: (0, 0, 0)),
                  pl.BlockSpec((gb, 1, D), lambda a, b: (a, 0, i_scale)),
                  pl.BlockSpec((gb, 1, D), lambda a, b: (a, 0, i_shift))],
        out_specs=pl.BlockSpec((gb, rb, D), lambda a, b: (a, b, 0)),
        out_shape=jax.ShapeDtypeStruct((G, R, D), BF16),
        compiler_params=_cparams("arbitrary", "arbitrary"),
        name="modnorm",
    )(x, g.reshape(1, 1, D), mod, mod)


def _rmsnorm_body(x_ref, g_ref, o_ref, *, gb, rb):
    n, tok, _ = _strips(gb, rb)

    def strip(i, carry):
        o_ref[tok(i)] = _rms(x_ref[tok(i)], g_ref[...])
        return carry

    lax.fori_loop(0, n, strip, 0, unroll=4)


def _rmsnorm(x, g, tm):
    G, R, D = x.shape
    gb, rb, nrb = _row_blocking(G, R, tm)
    return pl.pallas_call(
        functools.partial(_rmsnorm_body, gb=gb, rb=rb),
        grid=(G // gb, nrb),
        in_specs=[pl.BlockSpec((gb, rb, D), lambda a, b: (a, b, 0)),
                  pl.BlockSpec((1, 1, D), lambda a, b: (0, 0, 0))],
        out_specs=pl.BlockSpec((gb, rb, D), lambda a, b: (a, b, 0)),
        out_shape=jax.ShapeDtypeStruct((G, R, D), F32),
        compiler_params=_cparams("arbitrary", "arbitrary"),
        name="final_norm",
    )(x, g.reshape(1, 1, D))


def _mm_body(a_ref, w_ref, *rest, act, w_transposed, side_cast):
    if side_cast:
        src_ref, o_ref, dst_ref, wbf = rest
        dst_ref[...] = src_ref[...].astype(BF16)
    else:
        o_ref, wbf = rest

    @pl.when(pl.program_id(1) == 0)
    def _():
        wbf[...] = (w_ref[0].T if w_transposed else w_ref[...]).astype(BF16)

    acc = jnp.dot(a_ref[...].astype(BF16), wbf[...], preferred_element_type=F32)
    if act == "relu2":
        acc = jnp.square(jnp.maximum(acc, 0.0))
    o_ref[...] = acc.astype(o_ref.dtype)


def _mm(a, w, layer, *, n_cols, tm, tn, act=None, out_dtype=F32, name="mm", w_transposed=False, col0=0,
        cast_src=None):
    G, R, K = a.shape
    M = G * R
    assert M % tm == 0 and n_cols % tn == 0 and w.shape[2 if w_transposed else 1] == K
    nm = M // tm
    if w_transposed:
        assert col0 % SUBLANES == 0
        w_spec = pl.BlockSpec((pl.Element(1), pl.Element(tn), pl.Element(K)),
                              lambda j, i: (layer, pl.multiple_of(col0 + j * tn, SUBLANES), 0))
    else:
        assert col0 == 0
        w_spec = pl.BlockSpec((None, K, tn), lambda j, i: (layer, 0, j))
    in_specs = [pl.BlockSpec((tm, K), lambda j, i: (i, 0)), w_spec]
    out_specs = [pl.BlockSpec((tm, tn), lambda j, i: (i, j))]
    out_shape = [jax.ShapeDtypeStruct((M, n_cols), out_dtype)]
    args = [a.reshape(M, K), w]
    if cast_src is not None:
        _, P, Q = cast_src.shape
        steps = (n_cols // tn) * nm
        assert P % steps == 0
        in_specs.append(pl.BlockSpec((None, P // steps, Q), lambda j, i: (layer, j * nm + i, 0)))
        out_specs.append(pl.BlockSpec((P // steps, Q), lambda j, i: (j * nm + i, 0)))
        out_shape.append(jax.ShapeDtypeStruct((P, Q), BF16))
        args.append(cast_src)
    res = pl.pallas_call(
        functools.partial(_mm_body, act=act, w_transposed=w_transposed, side_cast=cast_src is not None),
        grid=(n_cols // tn, nm),
        in_specs=in_specs,
        out_specs=out_specs,
        out_shape=out_shape,
        scratch_shapes=[pltpu.VMEM((K, tn), BF16)],
        compiler_params=_cparams("arbitrary", "arbitrary"),
        name=name,
    )(*args)
    out = res[0].reshape(G, R, n_cols)
    return (out, res[1]) if cast_src is not None else out


def _mm_res_body(*refs, n_lhs, gb, rb, cast_w):
    a_refs = refs[:n_lhs]
    w_ref, x_ref, gt_ref, o_ref = refs[n_lhs:n_lhs + 4]
    if cast_w:
        wbf = refs[n_lhs + 4]

        @pl.when(pl.program_id(1) == 0)
        def _():
            wbf[...] = w_ref[...].astype(BF16)
    else:
        wbf = w_ref

    acc = None
    k0 = 0
    for a_ref in a_refs:
        kk = a_ref.shape[-1]
        part = jnp.dot(a_ref[...].astype(BF16), wbf[k0:k0 + kk, :], preferred_element_type=F32)
        acc = part if acc is None else acc + part
        k0 += kk
    tn = acc.shape[-1]
    o_ref[...] = x_ref[...] + gt_ref[...] * acc.reshape(gb, rb, tn)


def _mm_residual(a_list, w, layer, x, mod, i_gate, *, tm, tn, name="mm_res"):
    G, R, N = x.shape
    M = G * R
    K = w.shape[1]
    assert sum(a.shape[-1] for a in a_list) == K and w.shape[2] == N
    gb, rb, nrb = _row_blocking(G, R, tm)
    ntn = N // tn
    cast_w = w.dtype != BF16

    def xmap(j, i):
        return (i // nrb, i % nrb, j)

    return pl.pallas_call(
        functools.partial(_mm_res_body, n_lhs=len(a_list), gb=gb, rb=rb, cast_w=cast_w),
        grid=(ntn, M // tm),
        in_specs=[pl.BlockSpec((tm, a.shape[-1]), lambda j, i: (i, 0)) for a in a_list]
        + [pl.BlockSpec((None, K, tn), lambda j, i: (layer, 0, j)),
           pl.BlockSpec((gb, rb, tn), xmap),
           pl.BlockSpec((gb, 1, tn), lambda j, i: (i // nrb, 0, i_gate * ntn + j))],
        out_specs=pl.BlockSpec((gb, rb, tn), xmap),
        out_shape=jax.ShapeDtypeStruct((G, R, N), F32),
        scratch_shapes=[pltpu.VMEM((K, tn), BF16)] if cast_w else [],
        compiler_params=_cparams("arbitrary", "arbitrary"),
        name=name,
    )(*[a.reshape(M, a.shape[-1]) for a in a_list], w, x, mod)


def _mlstm_body(zq_ref, zk_ref, zv_ref, zo_ref, zg_ref, cw_ref, gbias_ref, gain_ref, conv0_ref,
                C0_ref, n0_ref, m0_ref, hm_ref, C_ref, n_ref, m_ref, ext, *, L, gs):
    W = M_WIDTH
    Dh = M_HEAD_DIM
    H = M_HEADS
    keep = M_CONV - 1
    c = pl.program_id(1)

    @pl.when(c == 0)
    def _():
        C_ref[...] = C0_ref[...]
        n_ref[...] = n0_ref[...]
        m_ref[...] = m0_ref[...]
        ext[:, SUBLANES - keep:SUBLANES, :] = conv0_ref[...]

    ext[:, SUBLANES:SUBLANES + L, 0:W] = zq_ref[...]
    ext[:, SUBLANES:SUBLANES + L, W:2 * W] = zk_ref[...]
    row = lax.broadcasted_iota(jnp.int32, (L, L), 0)
    col = lax.broadcasted_iota(jnp.int32, (L, L), 1)
    eye = row == col
    tri = col <= row
    chains = [(g, h) for g in range(gs) for h in range(H)]
    ids = range(len(chains))
    qk, gates = [], []
    for g in range(gs):
        conv = ext[g, SUBLANES:SUBLANES + L, :] * cw_ref[keep:keep + 1, :]
        for j in range(keep):
            conv = conv + ext[g, SUBLANES - keep + j:SUBLANES - keep + j + L, :] * cw_ref[j:j + 1, :]
        qk.append(_silu(conv))
        gates.append(zg_ref[g] + gbias_ref[...])
    ext[:, SUBLANES - keep:SUBLANES, :] = ext[:, SUBLANES + L - keep:SUBLANES + L, :]

    sl = [slice(h * Dh, (h + 1) * Dh) for _, h in chains]
    q = [qk[g][:, h * Dh:(h + 1) * Dh] for g, h in chains]
    k = [qk[g][:, W + h * Dh:W + (h + 1) * Dh] * (Dh ** -0.5) for g, h in chains]
    v = [zv_ref[g, :, sl[i]] for i, (g, h) in enumerate(chains)]
    ig_col = [gates[g][:, h:h + 1] for g, h in chains]
    fpre = [gates[g][:, H + h:H + h + 1] for g, h in chains]
    lf_col = [jnp.minimum(x, 0.0) - jnp.log1p(jnp.exp(-jnp.abs(x))) for x in fpre]
    lf_row = [jnp.sum(jnp.where(eye, x, 0.0), axis=0, keepdims=True) for x in lf_col]
    ig_row = [jnp.sum(jnp.where(eye, x, 0.0), axis=0, keepdims=True) for x in ig_col]
    b_col = [jnp.sum(jnp.where(tri, x, 0.0), axis=1, keepdims=True) for x in lf_row]
    b_row = [jnp.sum(jnp.where(row <= col, x, 0.0), axis=0, keepdims=True) for x in lf_col]
    dmat = [jnp.where(tri, b_col[i] - b_row[i] + ig_row[i], NEG_BIG) for i in ids]
    m_prev = [m_ref[g, :, h:h + 1] for g, h in chains]
    inter = [b_col[i] + m_prev[i] for i in ids]
    m_t = [jnp.maximum(inter[i], jnp.max(dmat[i], axis=1, keepdims=True)) for i in ids]
    w_inter = [jnp.exp(inter[i] - m_t[i]) for i in ids]
    Cm = [C_ref[g, h] for g, h in chains]
    n_row = [n_ref[g, h:h + 1, :] for g, h in chains]
    s = [_dot_nt(q[i], k[i]) * jnp.exp(dmat[i] - m_t[i]) for i in ids]
    qc = [_dot(q[i], Cm[i]) for i in ids]
    num = [_dot(s[i], v[i]) + w_inter[i] * qc[i] for i in ids]
    den = [jnp.sum(s[i], axis=1, keepdims=True) + w_inter[i] * jnp.sum(q[i] * n_row[i], axis=1, keepdims=True)
           for i in ids]
    hh = [num[i] / jnp.maximum(jnp.abs(den[i]), jnp.exp(-m_t[i])) for i in ids]
    m_new = [x[L - 1:L, :] for x in m_t]
    b_last = [x[L - 1:L, :] for x in b_col]
    wk = [jnp.exp(b_last[i] - b_col[i] + ig_col[i] - m_new[i]) * k[i] for i in ids]
    decay = [jnp.exp(b_last[i] + m_prev[i] - m_new[i]) for i in ids]
    upd = [_dot_tn(wk[i], v[i]) for i in ids]
    for i, (g, h) in enumerate(chains):
        C_ref[g, h] = decay[i] * Cm[i] + upd[i]
        n_ref[g, h:h + 1, :] = decay[i] * n_row[i] + jnp.sum(wk[i], axis=0, keepdims=True)
        m_ref[g, :, h:h + 1] = m_new[i]
        dlt = hh[i] - jnp.mean(hh[i], axis=-1, keepdims=True)
        ln = dlt * lax.rsqrt(jnp.mean(dlt * dlt, axis=-1, keepdims=True) + RMS_EPS)
        hm_ref[g, :, sl[i]] = (_sigmoid(zo_ref[g, :, sl[i]]) * ln * gain_ref[:, sl[i]]).astype(hm_ref.dtype)


def _mlstm(zm, zgate, conv_w, gate_b, m_gain, conv0, C0, n0, m0):
    G, R, _ = zm.shape
    L = math.gcd(R, CHUNK)
    W = M_WIDTH
    gs = 2 if R > SUBLANES else 4
    gbias = jnp.zeros((1, LANES), F32).at[0, :2 * M_HEADS].set(gate_b.astype(F32))
    zspec = lambda blk: pl.BlockSpec((gs, L, W), lambda g, c: (g, c, blk))
    st4 = lambda g, c: (g, 0, 0, 0)
    st3 = lambda g, c: (g, 0, 0)
    return pl.pallas_call(
        functools.partial(_mlstm_body, L=L, gs=gs),
        grid=(G // gs, R // L),
        in_specs=[zspec(0), zspec(1), zspec(2), zspec(3),
                  pl.BlockSpec((gs, L, LANES), lambda g, c: (g, c, 0)),
                  pl.BlockSpec((M_CONV, 2 * W), lambda g, c: (0, 0)),
                  pl.BlockSpec((1, LANES), lambda g, c: (0, 0)),
                  pl.BlockSpec((1, W), lambda g, c: (0, 0)),
                  pl.BlockSpec((gs, M_CONV - 1, 2 * W), st3),
                  pl.BlockSpec((gs, M_HEADS, M_HEAD_DIM, M_HEAD_DIM), st4),
                  pl.BlockSpec((gs, M_HEADS, M_HEAD_DIM), st3),
                  pl.BlockSpec((gs, 1, M_HEADS), st3)],
        out_specs=[pl.BlockSpec((gs, L, W), lambda g, c: (g, c, 0)),
                   pl.BlockSpec((gs, M_HEADS, M_HEAD_DIM, M_HEAD_DIM), st4),
                   pl.BlockSpec((gs, M_HEADS, M_HEAD_DIM), st3),
                   pl.BlockSpec((gs, 1, M_HEADS), st3)],
        out_shape=[jax.ShapeDtypeStruct((G, R, W), BF16),
                   jax.ShapeDtypeStruct((G, M_HEADS, M_HEAD_DIM, M_HEAD_DIM), F32),
                   jax.ShapeDtypeStruct((G, M_HEADS, M_HEAD_DIM), F32),
                   jax.ShapeDtypeStruct((G, 1, M_HEADS), F32)],
        scratch_shapes=[pltpu.VMEM((gs, SUBLANES + L, 2 * W), F32)],
        compiler_params=_cparams("arbitrary", "arbitrary"),
        name="mlstm",
    )(zm, zm, zm, zm, zgate, conv_w, gbias, m_gain.reshape(1, W), conv0, C0, n0, m0.reshape(G, 1, M_HEADS))


def _hgrn_body(zq_ref, zf_ref, zi_ref, zgg_ref, lb_ref, gain_ref, S0_ref, og_ref, S_ref, ST, *, nblk, gs, st_t, nh):
    Dh = G_HEAD_DIM
    B = SUBLANES
    c = pl.program_id(1)
    chains = [(g, h) for g in range(gs) for h in range(G_HEADS)]

    @pl.when(c == 0)
    def _():
        for i, (g, h) in enumerate(chains):
            ST[i] = S0_ref[g, h].T if st_t else S0_ref[g, h]

    rowi = lax.broadcasted_iota(jnp.int32, (B, Dh), 0)
    ids = range(len(chains))
    sl = [slice(h * Dh, (h + 1) * Dh) for _, h in chains]
    gi = [g for g, _ in chains]

    def diag(bc, qh, kk, v, o):
        for s in range(B):
            p = [jnp.exp(jnp.where(rowi >= s, bc[i] - bc[i][s:s + 1, :], NEG_BIG)) * qh[i] * kk[i][s:s + 1, :]
                 for i in ids]
            o = [o[i] + jnp.sum(p[i], axis=-1, keepdims=True) * v[i][s:s + 1, :] for i in ids]
        return o

    def blk(bi, carry):
        r0 = pl.multiple_of(bi * (nh * B), nh * B)
        st = [ST[i] for i in ids]
        f, kk, qh, v, bc = [], [], [], [], []
        for hf in range(nh):
            rows = pl.ds(r0 + hf * B, B)
            f_h = [lb_ref[:, sl[i]] + (1.0 - lb_ref[:, sl[i]]) * _sigmoid(zf_ref[gi[i], rows, sl[i]]) for i in ids]
            kk.append([1.0 - x for x in f_h])
            loc = [_cumsum_rows(jnp.log(x)) for x in f_h]
            bc.append(loc if hf == 0 else [loc[i] + bc[hf - 1][i][B - 1:B, :] for i in ids])
            qh.append([_silu(zq_ref[gi[i], rows, sl[i]]) * (Dh ** -0.5) for i in ids])
            v.append([zi_ref[gi[i], rows, sl[i]] for i in ids])
        btot = [bc[nh - 1][i][B - 1:B, :] for i in ids]
        cat = lambda parts, i: parts[0][i] if nh == 1 else jnp.concatenate([pt[i] for pt in parts], axis=0)
        qd = [cat([[qh[hf][i] * jnp.exp(bc[hf][i]) for i in ids] for hf in range(nh)], i) for i in ids]
        kd = [cat([[kk[hf][i] * jnp.exp(btot[i] - bc[hf][i]) for i in ids] for hf in range(nh)], i) for i in ids]
        vv = [cat(v, i) for i in ids]
        if st_t:
            os_ = [_dot_nt(qd[i], st[i]) for i in ids]
            upd = [_dot_tn(vv[i], kd[i]) for i in ids]
            dec = [jnp.exp(btot[i]) for i in ids]
        else:
            os_ = [_dot(qd[i], st[i]) for i in ids]
            upd = [_dot_tn(kd[i], vv[i]) for i in ids]
            dec = [jnp.transpose(jnp.broadcast_to(jnp.exp(btot[i]), (B, Dh)))[:, 0:1] for i in ids]
        o = []
        for hf in range(nh):
            oh = diag(bc[hf], qh[hf], kk[hf], v[hf], [os_[i][hf * B:(hf + 1) * B, :] for i in ids])
            for prev in range(hf):
                edge = [bc[prev][i][B - 1:B, :] for i in ids]
                ax = [_dot_nt(qh[hf][i] * jnp.exp(bc[hf][i] - edge[i]), kk[prev][i] * jnp.exp(edge[i] - bc[prev][i]))
                      for i in ids]
                oh = [oh[i] + _dot(ax[i], v[prev][i]) for i in ids]
            o.append(oh)
        for i in ids:
            ST[i] = st[i] * dec[i] + upd[i]
        per = min(nh, 2)
        for h0 in range(0, nh, per):
            rows = pl.ds(r0 + h0 * B, per * B)
            for i in ids:
                oi = o[h0][i] if per == 1 else jnp.concatenate([o[h0][i], o[h0 + 1][i]], axis=0)
                rms = oi * lax.rsqrt(jnp.mean(oi * oi, axis=-1, keepdims=True) + RMS_EPS)
                og_ref[gi[i], rows, sl[i]] = (rms * gain_ref[:, sl[i]]
                                              * _silu(zgg_ref[gi[i], rows, sl[i]])).astype(og_ref.dtype)
        return carry

    if nblk == 1:
        blk(0, 0)
    else:
        lax.fori_loop(0, nblk, blk, 0)

    @pl.when(c == pl.num_programs(1) - 1)
    def _():
        for i, (g, h) in enumerate(chains):
            S_ref[g, h] = ST[i].T if st_t else ST[i]


def _hgrn(zg, lb, g_gain, S0):
    G, R, _ = zg.shape
    W = G_WIDTH
    gs = 4
    Rc = min(R, 128)
    nh = 4 if R > SUBLANES else 1
    zspec = lambda blk: pl.BlockSpec((gs, Rc, W), lambda g, c: (g, c, blk))
    st4 = lambda g, c: (g, 0, 0, 0)
    return pl.pallas_call(
        functools.partial(_hgrn_body, nblk=Rc // (nh * SUBLANES), gs=gs, st_t=R > SUBLANES, nh=nh),
        grid=(G // gs, R // Rc),
        in_specs=[zspec(0), zspec(1), zspec(2), zspec(3),
                  pl.BlockSpec((1, W), lambda g, c: (0, 0)),
                  pl.BlockSpec((1, W), lambda g, c: (0, 0)),
                  pl.BlockSpec((gs, G_HEADS, G_HEAD_DIM, G_HEAD_DIM), st4)],
        out_specs=[pl.BlockSpec((gs, Rc, W), lambda g, c: (g, c, 0)),
                   pl.BlockSpec((gs, G_HEADS, G_HEAD_DIM, G_HEAD_DIM), st4)],
        out_shape=[jax.ShapeDtypeStruct((G, R, W), BF16),
                   jax.ShapeDtypeStruct((G, G_HEADS, G_HEAD_DIM, G_HEAD_DIM), F32)],
        scratch_shapes=[pltpu.VMEM((gs * G_HEADS, G_HEAD_DIM, G_HEAD_DIM), F32)],
        compiler_params=_cparams("arbitrary", "arbitrary"),
        name="hgrn2",
    )(zg, zg, zg, zg, lb.reshape(1, W), g_gain.reshape(1, W), S0)


def _rprep_body(x_ref, g_ref, sc_ref, sh_ref, shift0_ref, mu_ref, *refs, gb, rb):
    outs = refs[:6]
    hlast_ref, hbuf = refs[6:]
    B = SUBLANES
    n, tok, seq = _strips(gb, rb)
    S = hbuf.shape[1] - B
    whole_sequences = gb > 1

    if not whole_sequences:
        @pl.when(pl.program_id(1) == 0)
        def _():
            hbuf[:, B - 1:B, :] = shift0_ref[...]

    first = lax.broadcasted_iota(jnp.int32, (hbuf.shape[0], S, 1), 1) == 0

    def strip(i, carry):
        h = _rms(x_ref[tok(i)], g_ref[...]) * (1.0 + sc_ref[seq(i)]) + sh_ref[seq(i)]
        before = shift0_ref[seq(i)] if whole_sequences else hbuf[:, B - 1:B, :]
        xx = jnp.where(first, before, pltpu.roll(h, 1, axis=1)) - h
        last = h[:, S - 1:S, :]
        if whole_sequences:
            hlast_ref[seq(i)] = last
        else:
            hbuf[:, B - 1:B, :] = last
        for m in range(6):
            outs[m][tok(i)] = (h + xx * mu_ref[m:m + 1, :]).astype(BF16)
        return carry

    lax.fori_loop(0, n, strip, 0, unroll=4)
    if not whole_sequences:
        hlast_ref[...] = hbuf[:, B - 1:B, :]


def _rwkv_prep(x, g, mod, i_scale, i_shift, shift0, mu, tm):
    G, R, D = x.shape
    gb, rb, nrb = _row_blocking(G, R, tm)
    tok = pl.BlockSpec((gb, rb, D), lambda a, b: (a, b, 0))
    one = pl.BlockSpec((gb, 1, D), lambda a, b: (a, 0, 0))
    strip_shape = (1, SUBLANES + STRIP_ROWS, D) if gb == 1 else (STRIP_ROWS // rb, SUBLANES + rb, D)
    res = pl.pallas_call(
        functools.partial(_rprep_body, gb=gb, rb=rb),
        grid=(G // gb, nrb),
        in_specs=[tok,
                  pl.BlockSpec((1, 1, D), lambda a, b: (0, 0, 0)),
                  pl.BlockSpec((gb, 1, D), lambda a, b: (a, 0, i_scale)),
                  pl.BlockSpec((gb, 1, D), lambda a, b: (a, 0, i_shift)),
                  one,
                  pl.BlockSpec((6, D), lambda a, b: (0, 0))],
        out_specs=[tok] * 6 + [one],
        out_shape=[jax.ShapeDtypeStruct((G, R, D), BF16)] * 6 + [jax.ShapeDtypeStruct((G, 1, D), F32)],
        scratch_shapes=[pltpu.VMEM(strip_shape, F32)],
        compiler_params=_cparams("arbitrary", "arbitrary"),
        name="rwkv_prep",
    )(x, g.reshape(1, 1, D), mod, mod, shift0.reshape(G, 1, D), mu)
    return res[:6], res[6]


def _lora_body(x_ref, w1_ref, w2_ref, b_ref, o_ref, *, act, post):
    t = _dot(x_ref[...], w1_ref[...])
    if act == "tanh":
        t = jnp.tanh(t)
    elif act == "sigmoid":
        t = _sigmoid(t)
    y = _dot(t, w2_ref[...])
    if post == "log_decay":
        y = -math.exp(-0.5) * _sigmoid(b_ref[...] + y)
    elif post == "sigmoid":
        y = _sigmoid(b_ref[...] + y)
    o_ref[...] = y


def _lora(x, w1, w2, act, tm, post=None, bias=None):
    G, R, K = x.shape
    M = G * R
    r = w1.shape[1]
    rp = -(-r // LANES) * LANES
    w1p = jnp.zeros((K, rp), F32).at[:, :r].set(w1)
    w2p = jnp.zeros((rp, w2.shape[1]), F32).at[:r, :].set(w2)
    N = w2.shape[1]
    b = jnp.zeros((1, N), F32) if bias is None else bias.reshape(1, N).astype(F32)
    out = pl.pallas_call(
        functools.partial(_lora_body, act=act, post=post),
        grid=(M // tm,),
        in_specs=[pl.BlockSpec((tm, K), lambda i: (i, 0)),
                  pl.BlockSpec((K, rp), lambda i: (0, 0)),
                  pl.BlockSpec((rp, N), lambda i: (0, 0)),
                  pl.BlockSpec((1, N), lambda i: (0, 0))],
        out_specs=pl.BlockSpec((tm, N), lambda i: (i, 0)),
        out_shape=jax.ShapeDtypeStruct((M, N), F32),
        compiler_params=_cparams("arbitrary"),
        name="lora_" + str(act),
    )(x.reshape(M, K), w1p, w2p, b)
    return out.reshape(G, R, N)


def _head_mask(rows, cols, rper, cper):
    r = lax.broadcasted_iota(jnp.int32, (rows, cols), 0)
    c = lax.broadcasted_iota(jnp.int32, (rows, cols), 1)
    return (r // rper) == (c // cper)


def _bd(y, mask01):
    if y.shape[0] % (2 * SUBLANES) == 0:
        return jnp.concatenate([y.astype(BF16)] * R_GROUP, axis=0) * mask01
    return jnp.concatenate([y] * R_GROUP, axis=0).astype(BF16) * mask01


def _segsums(xs, ones_bd):
    c = xs[0].shape[0]
    if c % (2 * SUBLANES) == 0:
        lhs = jnp.concatenate([x.astype(BF16) for x in xs], axis=0)
    else:
        lhs = jnp.concatenate(xs, axis=0).astype(BF16)
    res = jnp.dot(lhs, ones_bd, preferred_element_type=F32)
    return [res[i * c:(i + 1) * c] for i in range(len(xs))]


def _rwkv_body(r_ref, k_ref, v_ref, lw_ref, a_ref, g_ref, kkp_ref, kap_ref, rk_ref,
               lnw_ref, lnb_ref, S0_ref, y_ref, S_ref, Sbd, *, c, nchunk, ng, gs):
    N = R_HEAD_DIM
    GW = R_GROUP_W
    j = pl.program_id(2)
    m_state = _head_mask(GW, GW, N, N)
    chains = [(q, gi) for q in range(gs) for gi in range(ng)]
    ids = range(len(chains))
    lanes = [slice(gi * GW, (gi + 1) * GW) for _, gi in chains]
    seqs = [q for q, _ in chains]

    @pl.when(j == 0)
    def _():
        for i, (q, gi) in enumerate(chains):
            s0 = S0_ref[q, gi * R_GROUP:(gi + 1) * R_GROUP].reshape(GW, N)
            Sbd[i] = jnp.where(m_state, jnp.concatenate([s0] * R_GROUP, axis=1), 0.0)

    ones_bd = jnp.where(m_state, 1.0, 0.0).astype(BF16)
    m_vec = jnp.where(_head_mask(R_GROUP * c, GW, c, N), 1.0, 0.0).astype(BF16)
    m_mat = jnp.where(_head_mask(R_GROUP * c, R_GROUP * c, c, c), 1.0, 0.0).astype(BF16)
    t_idx = lax.broadcasted_iota(jnp.int32, (2 * c, R_GROUP * c), 0)
    s_idx = lax.broadcasted_iota(jnp.int32, (2 * c, R_GROUP * c), 1) % c
    causal = jnp.where(t_idx < c, jnp.where(s_idx < t_idx, 1.0, 0.0), jnp.where(s_idx <= t_idx - c, 1.0, 0.0))
    nt = (((1,), (1,)), ((), ()))
    mm = lambda a, b: jnp.dot(a.astype(BF16), b, preferred_element_type=F32)

    tril = jnp.where(lax.broadcasted_iota(jnp.int32, (c, c), 1) <= lax.broadcasted_iota(jnp.int32, (c, c), 0),
                     1.0, 0.0).astype(BF16)
    eye = jnp.where(lax.broadcasted_iota(jnp.int32, (c, R_GROUP * c), 1) % c
                    == lax.broadcasted_iota(jnp.int32, (c, R_GROUP * c), 0), 1.0, 0.0)
    keys = ("ar", "bdb", "bdk", "bdv", "vb", "gam", "tail")

    def prepare_steps(ci, out):
        rows = pl.ds(pl.multiple_of(ci * c, c), c)
        ld = lambda ref: [ref[seqs[i], rows, lanes[i]] for i in ids]
        r, k, a = ld(r_ref), ld(k_ref), ld(a_ref)
        kk = [k[i] * kkp_ref[:, lanes[i]] for i in ids]
        k2 = [k[i] * (1.0 + (a[i] - 1.0) * kap_ref[:, lanes[i]]) for i in ids]
        sums = _segsums([kk[i] * kk[i] for i in ids] + [r[i] * k2[i] * rk_ref[:, lanes[i]] for i in ids], ones_bd)
        yield
        for i in ids:
            v, lw, g = v_ref[seqs[i], rows, lanes[i]], lw_ref[seqs[i], rows, lanes[i]], g_ref[seqs[i], rows, lanes[i]]
            hi = lw.astype(BF16)
            lo = (lw - hi.astype(F32)).astype(BF16)
            cw2 = jnp.dot(tril, jnp.concatenate([hi, lo], axis=1), preferred_element_type=F32)
            cw = cw2[:, :GW] + cw2[:, GW:]
            gam = jnp.exp(cw[c - 1:c, :])
            kn = kk[i] * lax.rsqrt(jnp.maximum(sums[i], 1e-24))
            bb = kn * a[i]
            e_out = jnp.exp(-cw)
            e_end = gam * e_out
            out["ar"].append(jnp.concatenate([-kn * jnp.exp(cw - lw), r[i] * jnp.exp(cw)], axis=0).astype(BF16))
            out["gam"].append(gam)
            out["tail"].append(jnp.concatenate([sums[len(chains) + i] * v, g], axis=0))
            yield
            out["bdb"].append(_bd(bb * e_out, m_vec))
            out["bdk"].append(_bd(k2[i] * e_out, m_vec))
            out["bdv"].append(_bd(v, m_vec))
            out["vb"].append(jnp.concatenate([v, bb * e_end, k2[i] * e_end], axis=0).astype(BF16))
            yield

    def prepare(ci):
        out = {key: [] for key in keys}
        for _ in prepare_steps(ci, out):
            pass
        return out

    def chain(ci, p, side_work=None):
        def tick():
            if side_work is not None:
                next(side_work, None)

        rows = pl.ds(pl.multiple_of(ci * c, c), c)
        ar, bdv = p["ar"], p["bdv"]
        sbd = [Sbd[i] for i in ids]
        pb = [lax.dot_general(ar[i], p["bdb"][i], nt, preferred_element_type=F32) * causal for i in ids]
        pk = [(lax.dot_general(ar[i], p["bdk"][i], nt, preferred_element_type=F32) * causal).astype(BF16) for i in ids]
        tick()
        s0p = [lax.dot_general(ar[i], sbd[i].astype(BF16), nt, preferred_element_type=F32) for i in ids]
        kv = [mm(pk[i], bdv[i]) for i in ids]
        w = [s0p[i][:c] + kv[i][:c] for i in ids]
        tick()
        pw = [pb[i][:c] for i in ids]
        tinv = [eye + pw[i] for i in ids]
        levels = c.bit_length() - 1
        bdp = [_bd(pw[i], m_mat) for i in ids]
        pw = [mm(pw[i], bdp[i]) for i in ids]
        tick()
        for lvl in range(1, levels):
            bdp = [_bd(pw[i], m_mat) for i in ids]
            if lvl < levels - 1:
                res = [mm(jnp.concatenate([tinv[i], pw[i]], axis=0), bdp[i]) for i in ids]
                tinv = [tinv[i] + res[i][:c] for i in ids]
                pw = [res[i][c:] for i in ids]
            else:
                tinv = [tinv[i] + mm(tinv[i], bdp[i]) for i in ids]
            tick()
        u = [mm(tinv[i], _bd(w[i], m_vec)).astype(BF16) for i in ids]
        tick()
        y = [s0p[i][c:] + mm(pb[i][c:], _bd(u[i], m_vec)) + kv[i][c:] for i in ids]
        tick()
        upd = [_dot_tn(jnp.concatenate([u[i], p["vb"][i][:c]], axis=0), p["vb"][i][c:]) for i in ids]
        for i in ids:
            Sbd[i] = sbd[i] * p["gam"][i] + jnp.where(m_state, upd[i], 0.0)
        tick()
        ysum = _segsums(y, ones_bd)
        dlt = [y[i] - ysum[i] * (1.0 / N) for i in ids]
        vsum = _segsums([dlt[i] * dlt[i] for i in ids], ones_bd)
        for i in ids:
            yn = dlt[i] * lax.rsqrt(vsum[i] * (1.0 / N) + LN_X_EPS) * lnw_ref[:, lanes[i]] + lnb_ref[:, lanes[i]]
            y_ref[seqs[i], rows, lanes[i]] = ((yn + p["tail"][i][:c]) * p["tail"][i][c:]).astype(y_ref.dtype)
        if side_work is not None:
            for _ in side_work:
                pass

    if nchunk == 1:
        chain(0, prepare(0))
    else:
        def body(ci, p):
            nxt = {key: [] for key in keys}
            chain(ci, p, prepare_steps(ci + 1, nxt))
            return nxt

        chain(nchunk - 1, lax.fori_loop(0, nchunk - 1, body, prepare(0)))

    @pl.when(j == pl.num_programs(2) - 1)
    def _():
        for i, (q, gi) in enumerate(chains):
            sbd = Sbd[i]
            for h in range(R_GROUP):
                S_ref[q, gi * R_GROUP + h] = sbd[h * N:(h + 1) * N, h * N:(h + 1) * N]


def _rwkv(r, k, v, lw, a, g, kkp, kap, rk, lnw, lnb, S0):
    G, R, D = r.shape
    c = math.gcd(R, CHUNK)
    GW = R_GROUP_W
    if R > SUBLANES:
        gs, ng, Rc = 1, 8, min(R, 256)
    else:
        gs, ng, Rc = 4, D // GW, R
    bw = ng * GW
    tok = pl.BlockSpec((gs, Rc, bw), lambda a_, b, j: (a_, j, b))
    par = pl.BlockSpec((1, bw), lambda a_, b, j: (0, b))
    st = pl.BlockSpec((gs, ng * R_GROUP, R_HEAD_DIM, R_HEAD_DIM), lambda a_, b, j: (a_, b, 0, 0))
    row = lambda p: p.reshape(1, D).astype(F32)
    return pl.pallas_call(
        functools.partial(_rwkv_body, c=c, nchunk=Rc // c, ng=ng, gs=gs),
        grid=(G // gs, D // bw, R // Rc),
        in_specs=[tok] * 6 + [par] * 5 + [st],
        out_specs=[tok, st],
        out_shape=[jax.ShapeDtypeStruct((G, R, D), BF16),
                   jax.ShapeDtypeStruct((G, R_HEADS, R_HEAD_DIM, R_HEAD_DIM), F32)],
        scratch_shapes=[pltpu.VMEM((gs * ng, GW, GW), F32)],
        compiler_params=_cparams("arbitrary", "arbitrary", "arbitrary"),
        name="rwkv7",
    )(r, k, v, lw, a, g, row(kkp), row(kap), row(rk), row(lnw), row(lnb), S0)


def _trunk(x, mod, m_C, m_n, m_m, m_conv, g_S, r_S, r_shift, p, lbs, tm, w2_bf16=None):
    make_w2 = w2_bf16 is None
    w2_bf16 = [] if make_w2 else w2_bf16
    G, R, D = x.shape
    md = mod[0]
    h = _modnorm(x, p["norm_mix"][0], md, 1, 0, tm)
    w_in_t = jnp.swapaxes(p["ab_w_in"], 1, 2)
    zm = _mm(h, w_in_t, 0, n_cols=4 * M_WIDTH, tm=tm, tn=1024, name="mm_in_m", w_transposed=True)
    zgate = _mm(h, w_in_t, 0, n_cols=LANES, tm=tm, tn=LANES, name="mm_in_gate", w_transposed=True, col0=4 * M_WIDTH)
    zg = _mm(h, w_in_t, 0, n_cols=4 * G_WIDTH, tm=tm, tn=1024, name="mm_in_g", w_transposed=True,
             col0=4 * M_WIDTH + 2 * M_HEADS)
    hm, C, n, m = _mlstm(zm, zgate, p["m_conv_w"][0], p["ab_gate_b"][0], p["m_norm"][0],
                         m_conv[0], m_C[0], m_n[0], m_m[0])
    conv_new = zm[:, R - (M_CONV - 1):, :2 * M_WIDTH]
    og, S = _hgrn(zg, lbs[0], p["g_norm"][0], g_S[0])
    x = _mm_residual([hm, og], p["ab_w_out"], 0, x, md, 2, tm=tm, tn=1024, name="mm_out0")
    h = _modnorm(x, p["norm_ffn"][0], md, 4, 3, tm)
    act = _mm(h, p["ffn_w1"], 0, n_cols=4 * D, tm=tm, tn=1024, act="relu2", out_dtype=BF16, name="ffn_up",
              cast_src=p["ffn_w2"] if make_w2 else None)
    if make_w2:
        act, w2 = act
        w2_bf16.append(w2[None])
    x = _mm_residual([act], w2_bf16[0], 0, x, md, 5, tm=256, tn=1024, name="ffn_down")
    md = mod[1]
    (xr, xw, xk, xv, xa, xg), shift_new = _rwkv_prep(x, p["norm_mix"][1], md, 1, 0, r_shift[0], p["r_mu"][0],
                                                     min(tm, 256))
    r = _mm(xr, p["r_wr"], 0, n_cols=D, tm=tm, tn=1024, name="mm_r")
    k = _mm(xk, p["r_wk"], 0, n_cols=D, tm=tm, tn=1024, name="mm_k")
    v = _mm(xv, p["r_wv"], 0, n_cols=D, tm=tm, tn=1024, name="mm_v")
    lw = _lora(xw, p["r_w1"][0], p["r_w2"][0], "tanh", min(tm, 512), post="log_decay", bias=p["r_w0"][0])
    aa = _lora(xa, p["r_a1"][0], p["r_a2"][0], None, min(tm, 512), post="sigmoid", bias=p["r_a0"][0])
    gg = _lora(xg, p["r_g1"][0], p["r_g2"][0], "sigmoid", min(tm, 512))
    yg, rS = _rwkv(r, k, v, lw, aa, gg, p["r_kk"][0], p["r_ka"][0], p["r_rk"][0].reshape(-1), p["r_lnw"][0],
                   p["r_lnb"][0], r_S[0])
    x = _mm_residual([yg], p["r_wo"], 0, x, md, 2, tm=tm, tn=1024, name="mm_out1")
    h = _modnorm(x, p["norm_ffn"][1], md, 4, 3, tm)
    act = _mm(h, p["ffn_w1"], 1, n_cols=4 * D, tm=tm, tn=1024, act="relu2", out_dtype=BF16, name="ffn_up",
              cast_src=p["ffn_w2"] if make_w2 else None)
    if make_w2:
        act, w2 = act
        w2_bf16.append(w2[None])
    x = _mm_residual([act], w2_bf16[1], 0, x, md, 5, tm=256, tn=1024, name="ffn_down")
    y = _rmsnorm(x, p["final_norm"], min(tm, 512))
    return y, (C[None], n[None], m.reshape(1, G, M_HEADS), conv_new[None], S[None], rS[None],
               shift_new.reshape(1, G, D)), w2_bf16


def kernel(x_prompt, x_sample, c_prompt, c_sample, state_mlstm_C, state_mlstm_n, state_mlstm_m, state_mlstm_conv, state_hgrn_S, state_rwkv_S, state_rwkv_shift, mod_w, mod_b, norm_mix, norm_ffn, ffn_w1, ffn_w2, final_norm, ab_w_in, ab_gate_b, m_conv_w, m_norm, g_lb, g_norm, ab_w_out, r_mu, r_w0, r_w1, r_w2, r_a0, r_a1, r_a2, r_g1, r_g2, r_kk, r_ka, r_rk, r_wr, r_wk, r_wv, r_wo, r_lnw, r_lnb):
    p = dict(norm_mix=norm_mix, norm_ffn=norm_ffn, ffn_w1=ffn_w1, ffn_w2=ffn_w2, final_norm=final_norm,
             ab_w_in=ab_w_in, ab_gate_b=ab_gate_b, m_conv_w=m_conv_w, m_norm=m_norm, g_norm=g_norm,
             ab_w_out=ab_w_out, r_mu=r_mu, r_w0=r_w0, r_w1=r_w1, r_w2=r_w2, r_a0=r_a0, r_a1=r_a1, r_a2=r_a2,
             r_g1=r_g1, r_g2=r_g2, r_kk=r_kk, r_ka=r_ka, r_rk=r_rk, r_wr=r_wr, r_wk=r_wk, r_wv=r_wv,
             r_wo=r_wo, r_lnw=r_lnw, r_lnb=r_lnb)
    B, T, D = x_prompt.shape
    Bs = x_sample.shape[0]
    depth = mod_w.shape[0]
    pad = (-(Bs + B)) % SUBLANES
    c_all = jnp.concatenate([c_sample, c_prompt, jnp.zeros((pad, D), F32)], axis=0)
    mod = _modulation(c_all, mod_w, mod_b)
    mod_s = mod[:, :Bs].reshape(depth, Bs, 1, N_MOD * D)
    mod_p = mod[:, Bs:Bs + B].reshape(depth, B, 1, N_MOD * D)
    lbs = jnp.cumsum(jax.nn.softmax(g_lb.astype(F32), axis=0), axis=0)

    n_even = state_mlstm_C.shape[0]
    n_odd = state_rwkv_S.shape[0]
    z = lambda *s: jnp.zeros(s, F32)
    yp, sp, w2_bf16 = _trunk(x_prompt, mod_p,
                    z(n_even, B, M_HEADS, M_HEAD_DIM, M_HEAD_DIM), z(n_even, B, M_HEADS, M_HEAD_DIM),
                    z(n_even, B, M_HEADS), z(n_even, B, M_CONV - 1, 2 * M_WIDTH),
                    z(n_even, B, G_HEADS, G_HEAD_DIM, G_HEAD_DIM), z(n_odd, B, R_HEADS, R_HEAD_DIM, R_HEAD_DIM),
                    z(n_odd, B, D), p, lbs, 1024)
    ys, ss, _ = _trunk(x_sample, mod_s, state_mlstm_C, state_mlstm_n, state_mlstm_m, state_mlstm_conv,
                       state_hgrn_S, state_rwkv_S, state_rwkv_shift, p, lbs, 1024, w2_bf16)
    return (yp, ys) + tuple(sp) + tuple(ss)
```
